```python
import math
import jax, jax.numpy as jnp
from jax import lax
import numpy as np

D_MODEL = 1024
BATCH = 8
SEQ = 4096
DEPTH = 4

HEAD_DIM = 64
RWKV_HEADS = 4
ATTN_HEADS = 8
RET_HEADS = 4
RWKV_DIM = RWKV_HEADS * HEAD_DIM
ATTN_DIM = ATTN_HEADS * HEAD_DIM
RET_DIM = RET_HEADS * HEAD_DIM
D_MIX = RWKV_DIM + ATTN_DIM + RET_DIM
DECAY_LORA = 64
ICL_LORA = 64
GATE_LORA = 128
RWKV_SPLITS = (RWKV_DIM, 2 * RWKV_DIM, 3 * RWKV_DIM, 3 * RWKV_DIM + DECAY_LORA, 3 * RWKV_DIM + DECAY_LORA + ICL_LORA)
RWKV_IN = 3 * RWKV_DIM + DECAY_LORA + ICL_LORA + GATE_LORA
ATTN_IN = 3 * ATTN_DIM
RET_IN = 4 * RET_DIM
N_IN = RWKV_IN + ATTN_IN + RET_IN
RWKV_GN_EPS = 64e-5
DECAY_SCALE = math.exp(-0.5)
DILATED_PATTERNS = ((128, 1), (512, 4), (2048, 16))
NUM_BUCKETS = 32
MAX_DISTANCE = 2048
ROPE_BASE = 10000.0
RET_CHUNK = 128
FF_DENSE = 2816
N_EXPERTS = 8
TOP_K = 2
FF_EXPERT = 3584
MOE_BLOCK = 256
LN_EPS = 1e-5
DEEPNORM_ALPHA = (2 * DEPTH) ** 0.25
DEEPNORM_BETA = (8 * DEPTH) ** -0.25
N_DENSE = (DEPTH + 1) // 2
N_MOE = DEPTH // 2

kernel_name = 'hybrid_rwkv7_dilated_retention_moe_deepnorm'


def layer_norm(x, g, b):
    xf = x.astype(jnp.float32)
    mu = xf.mean(-1, keepdims=True)
    var = jnp.square(xf - mu).mean(-1, keepdims=True)
    return ((xf - mu) * lax.rsqrt(var + LN_EPS) * g + b).astype(x.dtype)


def head_group_norm(y, g, b, eps):
    b_, s_, h, n = y.shape
    yf = y.astype(jnp.float32)
    mu = yf.mean(-1, keepdims=True)
    var = jnp.square(yf - mu).mean(-1, keepdims=True)
    return ((yf - mu) * lax.rsqrt(var + eps)).reshape(b_, s_, h * n) * g + b


def token_shift(p):
    return jnp.pad(p, ((0, 0), (1, 0), (0, 0)))[:, :-1]


def swiglu(x, wg, wu, wd):
    return (jax.nn.silu(x @ wg) * (x @ wu)) @ wd


def wkv7_scan(r, w, k, v, kap, a):
    def step(state, inp):
        r_t, w_t, k_t, v_t, kap_t, a_t = inp
        sa = jnp.einsum('bhvk,bhk->bhv', state, -kap_t)
        state = (state * w_t[:, :, None, :]
                 + jnp.einsum('bhv,bhk->bhvk', sa, kap_t * a_t)
                 + jnp.einsum('bhv,bhk->bhvk', v_t, k_t))
        return state, jnp.einsum('bhvk,bhk->bhv', state, r_t)
    b_, s_, h, n = r.shape
    s0 = jnp.zeros((b_, h, n, n), jnp.float32)
    xs = tuple(jnp.moveaxis(t, 1, 0) for t in (r, w, k, v, kap, a))
    _, ys = lax.scan(step, s0, xs)
    return jnp.moveaxis(ys, 0, 1)


def rwkv7_time_mix(p, mu, w0, w_up, a0, a_up, g_up, k_k, k_a, r_k, ln_g, ln_b):
    b_, s_, _ = p.shape
    p = p + (token_shift(p) - p) * mu
    r, k, v, xw, xa, xg = jnp.split(p, RWKV_SPLITS, axis=-1)
    decay = jnp.exp(-DECAY_SCALE * jax.nn.sigmoid((w0 + jnp.tanh(xw) @ w_up).astype(jnp.float32)))
    a = jax.nn.sigmoid((a0 + xa @ a_up).astype(jnp.float32))
    g = (jax.nn.sigmoid(xg) @ g_up).astype(jnp.float32)
    heads = lambda t: t.astype(jnp.float32).reshape(b_, s_, RWKV_HEADS, HEAD_DIM)
    hv = lambda t: t.astype(jnp.float32).reshape(RWKV_HEADS, HEAD_DIM)
    r, k, v, a, decay = heads(r), heads(k), heads(v), heads(a), heads(decay)
    kap = k * hv(k_k)
    kap = kap / jnp.maximum(jnp.sqrt(jnp.sum(kap * kap, -1, keepdims=True)), 1e-12)
    k = k * (1.0 + (a - 1.0) * hv(k_a))
    y = wkv7_scan(r, decay, k, v, kap, a)
    y = head_group_norm(y, ln_g, ln_b, RWKV_GN_EPS)
    bonus = (jnp.sum(r * k * hv(r_k), -1, keepdims=True) * v).reshape(b_, s_, RWKV_DIM)
    return (y + bonus) * g


def t5_bucket(dist):
    max_exact = NUM_BUCKETS // 2
    large = max_exact + (np.log(np.maximum(dist, max_exact) / max_exact)
                         / math.log(MAX_DISTANCE / max_exact) * (NUM_BUCKETS - max_exact)).astype(np.int32)
    return np.where(dist < max_exact, dist, np.minimum(large, NUM_BUCKETS - 1)).astype(np.int32)


def dilated_window_attention(q, k, v, rel_bias, window, dilation):
    b_, s_, h, dh = q.shape
    w = window // dilation
    l = s_ // dilation
    nb = -(-l // w)
    lp = nb * w

    def blocks(t):
        t = t.reshape(b_, l, dilation, h, dh)
        t = jnp.pad(t, ((0, 0), (0, lp - l), (0, 0), (0, 0), (0, 0)))
        return t.reshape(b_, nb, w, dilation, h, dh)

    qb, kb, vb = blocks(q), blocks(k), blocks(v)
    prev = lambda t: jnp.concatenate([jnp.zeros_like(t[:, :1]), t[:, :-1]], axis=1)
    kk = jnp.concatenate([prev(kb), kb], axis=2)
    vv = jnp.concatenate([prev(vb), vb], axis=2)
    i = np.arange(w)[:, None]
    j = np.arange(2 * w)[None, :]
    rel = i + w - j
    band = (rel >= 0) & (rel <= w)
    first = (np.arange(nb) == 0)[:, None, None]
    valid = band[None] & ~(first & (j < w)[None])
    bucket = t5_bucket(np.clip(rel, 0, None) * dilation)
    bias = jnp.transpose(rel_bias[bucket], (2, 0, 1)).astype(jnp.float32)
    s = jnp.einsum('bnidhe,bnjdhe->bndhij', qb, kk).astype(jnp.float32) * (dh ** -0.5) + bias
    s = jnp.where(valid[None, :, None, None], s, -jnp.inf)
    m = s.max(-1, keepdims=True)
    pr = jnp.exp(s - m)
    den = pr.sum(-1)
    o = jnp.einsum('bndhij,bnjdhe->bndhie', pr, vv.astype(jnp.float32)) / den[..., None]
    lse = m[..., 0] + jnp.log(den)
    o = jnp.transpose(o, (0, 1, 4, 2, 3, 5)).reshape(b_, lp, dilation, h, dh)[:, :l].reshape(b_, s_, h, dh)
    lse = jnp.transpose(lse, (0, 1, 4, 2, 3)).reshape(b_, lp, dilation, h)[:, :l].reshape(b_, s_, h)
    return o, lse


def dilated_attention_mixture(p, rel_bias):
    b_, s_, _ = p.shape
    q, k, v = (t.reshape(b_, s_, ATTN_HEADS, HEAD_DIM) for t in jnp.split(p, 3, axis=-1))
    outs, lses = [], []
    for window, dilation in DILATED_PATTERNS:
        o, lse = dilated_window_attention(q, k, v, rel_bias, window, dilation)
        outs.append(o)
        lses.append(lse)
    wts = jax.nn.softmax(jnp.stack(lses), axis=0)
    out = jnp.einsum('pbsh,pbshe->bshe', wts, jnp.stack(outs))
    return out.reshape(b_, s_, ATTN_DIM)


def rotary(x):
    s_, dh = x.shape[1], x.shape[-1]
    half = dh // 2
    inv = ROPE_BASE ** (-jnp.arange(half, dtype=jnp.float32) / half)
    ang = jnp.arange(s_, dtype=jnp.float32)[:, None] * inv
    cos, sin = jnp.cos(ang)[None, :, None, :], jnp.sin(ang)[None, :, None, :]
    x1, x2 = x[..., :half], x[..., half:]
    return jnp.concatenate([x1 * cos - x2 * sin, x1 * sin + x2 * cos], axis=-1)


def retention_chunkwise(q, k, v):
    b_, s_, h, dk = q.shape
    c = RET_CHUNK
    nc = s_ // c
    log_g = jnp.log1p(-jnp.exp2(-5.0 - jnp.arange(h, dtype=jnp.float32)))
    n = jnp.arange(c, dtype=jnp.float32)
    diff = n[:, None] - n[None, :]
    d_intra = jnp.where(diff >= 0, jnp.exp(log_g[:, None, None] * jnp.maximum(diff, 0.0)), 0.0)
    zeta = jnp.exp(log_g[:, None] * (c - 1 - n))
    xi = jnp.exp(log_g[:, None] * (n + 1))
    chunk_decay = jnp.exp(log_g * c)
    qc, kc, vc = (t.reshape(b_, nc, c, h, -1) for t in (q, k, v))
    sc = jnp.einsum('bcnhd,bcmhd->bchnm', qc, kc) * d_intra
    intra = jnp.einsum('bchnm,bcmhe->bcnhe', sc, vc)
    kv = jnp.einsum('bcmhd,bcmhe->bchde', kc * zeta.T[:, :, None], vc)

    def step(state, kv_c):
        return state * chunk_decay[None, :, None, None] + kv_c, state

    _, r_prev = lax.scan(step, jnp.zeros_like(kv[:, 0]), jnp.moveaxis(kv, 1, 0))
    cross = jnp.einsum('bcnhd,cbhde->bcnhe', qc * xi.T[:, :, None], r_prev)
    return (intra + cross).reshape(b_, s_, h, -1)


def retention_mixer(p, gn_g, gn_b):
    b_, s_, _ = p.shape
    q, k, v, g = jnp.split(p, 4, axis=-1)
    hd = lambda t: t.astype(jnp.float32).reshape(b_, s_, RET_HEADS, HEAD_DIM)
    q = rotary(hd(q))
    k = rotary(hd(k)) * (HEAD_DIM ** -0.5)
    y = retention_chunkwise(q, k, hd(v))
    y = head_group_norm(y, gn_g, gn_b, LN_EPS)
    return jax.nn.silu(g.astype(jnp.float32)) * y


def hybrid_mixer(x, w_in, w_out, mu, w0, w_up, a0, a_up, g_up, k_k, k_a, r_k, rln_g, rln_b,
                 gn_g, gn_b, rel_bias):
    p = x @ w_in
    p_rwkv, p_attn, p_ret = jnp.split(p, [RWKV_IN, RWKV_IN + ATTN_IN], axis=-1)
    y_a = rwkv7_time_mix(p_rwkv, mu, w0, w_up, a0, a_up, g_up, k_k, k_a, r_k, rln_g, rln_b)
    y_b = dilated_attention_mixture(p_attn, rel_bias)
    y_c = retention_mixer(p_ret, gn_g, gn_b)
    y = jnp.concatenate([y_a.astype(x.dtype), y_b.astype(x.dtype), y_c.astype(x.dtype)], axis=-1)
    return y @ w_out


def moe_ffn(x, router, wg, wu, wd):
    b_, s_, d = x.shape
    t = b_ * s_
    xt = x.reshape(t, d)
    logits = (xt @ router).astype(jnp.float32)
    top_val, top_idx = lax.top_k(logits, TOP_K)
    gates = jax.nn.softmax(top_val, axis=-1)
    e_flat = top_idx.reshape(-1)
    tok_flat = jnp.repeat(jnp.arange(t), TOP_K)
    g_flat = gates.reshape(-1)
    order = jnp.argsort(e_flat)
    e_sorted, tok_sorted, g_sorted = e_flat[order], tok_flat[order], g_flat[order]
    counts = jnp.bincount(e_flat, length=N_EXPERTS)
    starts = jnp.cumsum(counts) - counts
    padded = (counts + MOE_BLOCK - 1) // MOE_BLOCK * MOE_BLOCK
    pad_starts = jnp.cumsum(padded) - padded
    pad_ends = pad_starts + padded
    dest = pad_starts[e_sorted] + jnp.arange(t * TOP_K) - starts[e_sorted]
    rows = t * TOP_K + N_EXPERTS * MOE_BLOCK
    n_blk = rows // MOE_BLOCK
    buf = jnp.zeros((rows, d), x.dtype).at[dest].set(xt[tok_sorted])
    blk_e = jnp.minimum(jnp.searchsorted(pad_ends, jnp.arange(n_blk) * MOE_BLOCK, side='right'), N_EXPERTS - 1)

    def run(args):
        xb, e = args
        return swiglu(xb, wg[e], wu[e], wd[e])

    yb = lax.map(run, (buf.reshape(n_blk, MOE_BLOCK, d), blk_e)).reshape(rows, d)
    y_assign = yb[dest] * g_sorted[:, None].astype(yb.dtype)
    out = jnp.zeros((t, d), yb.dtype).at[tok_sorted].add(y_assign)
    return out.reshape(b_, s_, d)


def setup_inputs(seed: int = 0) -> dict:
    key = jax.random.key(seed)
    ks = jax.random.split(key, 27)
    f32 = jnp.float32
    nrm = lambda i, shape, scale: jax.random.normal(ks[i], shape, f32) * scale
    beta = DEEPNORM_BETA
    return {
        'x': nrm(0, (BATCH, SEQ, D_MODEL), 1.0),
        'w_in': nrm(1, (DEPTH, D_MODEL, N_IN), D_MODEL ** -0.5),
        'w_out': nrm(2, (DEPTH, D_MIX, D_MODEL), beta * D_MIX ** -0.5),
        'rwkv_mu': jax.random.uniform(ks[3], (DEPTH, RWKV_IN), f32),
        'rwkv_w0': jax.random.uniform(ks[4], (DEPTH, RWKV_DIM), f32, -4.0, 1.0),
        'rwkv_w_up': nrm(5, (DEPTH, DECAY_LORA, RWKV_DIM), 0.1),
        'rwkv_a0': nrm(6, (DEPTH, RWKV_DIM), 0.1),
        'rwkv_a_up': nrm(7, (DEPTH, ICL_LORA, RWKV_DIM), ICL_LORA ** -0.5),
        'rwkv_g_up': nrm(8, (DEPTH, GATE_LORA, RWKV_DIM), GATE_LORA ** -0.5),
        'rwkv_k_k': 0.85 + nrm(9, (DEPTH, RWKV_DIM), 0.05),
        'rwkv_k_a': 1.0 + nrm(10, (DEPTH, RWKV_DIM), 0.05),
        'rwkv_r_k': nrm(11, (DEPTH, RWKV_DIM), 0.1),
        'rwkv_ln_g': 1.0 + nrm(12, (DEPTH, RWKV_DIM), 0.02),
        'rwkv_ln_b': nrm(13, (DEPTH, RWKV_DIM), 0.02),
        'ret_gn_g': 1.0 + nrm(14, (DEPTH, RET_DIM), 0.02),
        'ret_gn_b': nrm(15, (DEPTH, RET_DIM), 0.02),
        'rel_bias': nrm(16, (NUM_BUCKETS, ATTN_HEADS), 0.2),
        'ln_g': 1.0 + nrm(17, (DEPTH, 2, D_MODEL), 0.02),
        'ln_b': nrm(18, (DEPTH, 2, D_MODEL), 0.02),
        'ffn_w_gate': nrm(19, (N_DENSE, D_MODEL, FF_DENSE), D_MODEL ** -0.5),
        'ffn_w_up': nrm(20, (N_DENSE, D_MODEL, FF_DENSE), D_MODEL ** -0.5),
        'ffn_w_down': nrm(21, (N_DENSE, FF_DENSE, D_MODEL), beta * FF_DENSE ** -0.5),
        'moe_router': nrm(22, (N_MOE, D_MODEL, N_EXPERTS), D_MODEL ** -0.5),
        'moe_w_gate': nrm(23, (N_MOE, N_EXPERTS, D_MODEL, FF_EXPERT), D_MODEL ** -0.5),
        'moe_w_up': nrm(24, (N_MOE, N_EXPERTS, D_MODEL, FF_EXPERT), D_MODEL ** -0.5),
        'moe_w_down': nrm(25, (N_MOE, N_EXPERTS, FF_EXPERT, D_MODEL), beta * FF_EXPERT ** -0.5),
    }


def reference(x, w_in, w_out, rwkv_mu, rwkv_w0, rwkv_w_up, rwkv_a0, rwkv_a_up, rwkv_g_up, rwkv_k_k,
              rwkv_k_a, rwkv_r_k, rwkv_ln_g, rwkv_ln_b, ret_gn_g, ret_gn_b, rel_bias, ln_g, ln_b,
              ffn_w_gate, ffn_w_up, ffn_w_down, moe_router, moe_w_gate, moe_w_up, moe_w_down):
    for layer in range(DEPTH):
        h = hybrid_mixer(x, w_in[layer], w_out[layer], rwkv_mu[layer], rwkv_w0[layer], rwkv_w_up[layer],
                         rwkv_a0[layer], rwkv_a_up[layer], rwkv_g_up[layer], rwkv_k_k[layer], rwkv_k_a[layer],
                         rwkv_r_k[layer], rwkv_ln_g[layer], rwkv_ln_b[layer], ret_gn_g[layer], ret_gn_b[layer],
                         rel_bias)
        x = layer_norm(DEEPNORM_ALPHA * x + h, ln_g[layer, 0], ln_b[layer, 0])
        j = layer // 2
        if layer % 2 == 0:
            f = swiglu(x, ffn_w_gate[j], ffn_w_up[j], ffn_w_down[j])
        else:
            f = moe_ffn(x, moe_router[j], moe_w_gate[j], moe_w_up[j], moe_w_down[j])
        x = layer_norm(DEEPNORM_ALPHA * x + f, ln_g[layer, 1], ln_b[layer, 1])
    return x
```

```python
import functools
import math

import numpy as np
import jax
import jax.numpy as jnp
from jax import lax
from jax.experimental import pallas as pl
from jax.experimental.pallas import tpu as pltpu

F32 = jnp.float32
BF16 = jnp.bfloat16
HI = lax.Precision.HIGHEST

HEAD_DIM = 64
RWKV_HEADS = 4
ATTN_HEADS = 8
RET_HEADS = 4
RWKV_DIM = RWKV_HEADS * HEAD_DIM
ATTN_DIM = ATTN_HEADS * HEAD_DIM
RET_DIM = RET_HEADS * HEAD_DIM
DECAY_LORA = 64
ICL_LORA = 64
GATE_LORA = 128
RWKV_IN = 3 * RWKV_DIM + DECAY_LORA + ICL_LORA + GATE_LORA
ATTN_IN = 3 * ATTN_DIM
RET_IN = 4 * RET_DIM
RWKV_GN_EPS = 64e-5
DECAY_SCALE = math.exp(-0.5)
DILATED_PATTERNS = ((128, 1), (512, 4), (2048, 16))
NUM_BUCKETS = 32
MAX_DISTANCE = 2048
ROPE_BASE = 10000.0
N_EXPERTS = 8
TOP_K = 2
LN_EPS = 1e-5

LANES = 128
WKV_CHUNK = 64
WKV_BLOCK = 256
ATTN_W = 128
RET_CHUNK = 128
RET_BLOCK = 1024
ROW_TILE = 512
MOE_TILE = 512
MASK_VALUE = -1e30
VMEM_LIMIT = 56 * 1024 * 1024


def _dot(a, b, prec=None):
    return jnp.dot(a, b, preferred_element_type=F32, precision=prec)


def _dot_nt(a, b, prec=None):
    return lax.dot_general(a, b, (((1,), (1,)), ((), ())), preferred_element_type=F32, precision=prec)


def _dot_tn(a, b, prec=None):
    return lax.dot_general(a, b, (((0,), (0,)), ((), ())), preferred_element_type=F32, precision=prec)


def _sigmoid(x):
    return 1.0 / (1.0 + jnp.exp(-x))


def _layer_norm(z, g, b):
    mu = jnp.mean(z, axis=-1, keepdims=True)
    d = z - mu
    var = jnp.mean(d * d, axis=-1, keepdims=True)
    return d * lax.rsqrt(var + LN_EPS) * g + b


def _params(sem, vmem=VMEM_LIMIT):
    return pltpu.CompilerParams(dimension_semantics=sem, vmem_limit_bytes=vmem)


def _const_spec(shape):
    nd = len(shape)
    return pl.BlockSpec(shape, lambda *_: (0,) * nd)


def _inproj_kernel(x_ref, w_ref, o_ref, *, n_chunk):
    xb = x_ref[...].astype(BF16)
    for n0 in range(0, o_ref.shape[1], n_chunk):
        o_ref[:, n0:n0 + n_chunk] = _dot(xb, w_ref[:, n0:n0 + n_chunk])


def in_projection(x2d, w_bf16):
    t, d = x2d.shape
    n = w_bf16.shape[1]
    tm = min(ROW_TILE, t)
    return pl.pallas_call(
        functools.partial(_inproj_kernel, n_chunk=512),
        grid=(t // tm,),
        in_specs=[pl.BlockSpec((tm, d), lambda i: (i, 0)), _const_spec((d, n))],
        out_specs=pl.BlockSpec((tm, n), lambda i: (i, 0)),
        out_shape=jax.ShapeDtypeStruct((t, n), F32),
        compiler_params=_params(("parallel",)),
        name="in_projection",
    )(x2d, w_bf16)


def _rwkv_kernel(p_ref, mu_ref, w0_ref, wup_ref, a0_ref, aup_ref, gup_ref, kk_ref, ka_ref, rk_ref,
                 lng_ref, lnb_ref, ltri_ref, same_ref, hsum_ref, o_ref,
                 state_s, prev_s, kt_s, rt_s, bt_s, kn_s, v_s, btg_s, kng_s, etot_s, y_s):
    c = WKV_CHUNK
    tb = p_ref.shape[0]
    d = RWKV_DIM

    @pl.when(pl.program_id(1) == 0)
    def _():
        state_s[...] = jnp.zeros_like(state_s)
        prev_s[...] = jnp.zeros_like(prev_s)

    p = p_ref[...]
    row = lax.broadcasted_iota(jnp.int32, p.shape, 0)
    shifted = jnp.where(row == 0, prev_s[...], pltpu.roll(p, 1, axis=0))
    prev_s[...] = p[tb - 1:tb, :]
    ps = p + (shifted - p) * mu_ref[...]
    r = ps[:, 0:d]
    k = ps[:, d:2 * d]
    v = ps[:, 2 * d:3 * d]
    xw = ps[:, 3 * d:3 * d + DECAY_LORA]
    xa = ps[:, 3 * d + DECAY_LORA:3 * d + DECAY_LORA + ICL_LORA]
    xg = ps[:, 3 * d + DECAY_LORA + ICL_LORA:]

    hsum = hsum_ref[...]
    logw = -DECAY_SCALE * _sigmoid(w0_ref[...] + _dot(jnp.tanh(xw), wup_ref[...], HI))
    a = _sigmoid(a0_ref[...] + _dot(xa, aup_ref[...], HI))
    g = _dot(_sigmoid(xg).astype(BF16), gup_ref[...].astype(BF16))
    kap = k * kk_ref[...]
    kap = kap / jnp.maximum(jnp.sqrt(_dot(kap * kap, hsum, HI)), 1e-12)
    kn = k * (1.0 + (a - 1.0) * ka_ref[...])
    cum = _dot(ltri_ref[...], logw, HI)
    tot = _dot(same_ref[...], logw, HI)
    e_neg = jnp.exp(-cum)
    e_rem = jnp.exp(tot - cum)
    nb = -(a * kap)
    kt_s[...] = kap * jnp.exp(cum - logw)
    rt_s[...] = r * jnp.exp(cum)
    bt_s[...] = nb * e_neg
    kn_s[...] = kn * e_neg
    btg_s[...] = nb * e_rem
    kng_s[...] = kn * e_rem
    etot_s[...] = jnp.exp(tot)
    v_s[...] = v

    ri = lax.broadcasted_iota(jnp.int32, (c, c), 0)
    ci = lax.broadcasted_iota(jnp.int32, (c, c), 1)
    strict = ci < ri
    incl = ci <= ri
    eye = (ci == ri).astype(F32)

    def chunk_body(j, carry):
        rows = pl.ds(pl.multiple_of(j * c, c), c)
        kt, rt, bt, knn = kt_s[rows, :], rt_s[rows, :], bt_s[rows, :], kn_s[rows, :]
        vv, btg, kng, etot = v_s[rows, :], btg_s[rows, :], kng_s[rows, :], etot_s[rows, :]
        for h in range(RWKV_HEADS):
            sl = slice(h * HEAD_DIM, (h + 1) * HEAD_DIM)
            lhs = jnp.concatenate([kt[:, sl], rt[:, sl]], axis=0)
            rhs = jnp.concatenate([bt[:, sl], knn[:, sl]], axis=0)
            amat = _dot_nt(lhs, rhs, HI)
            a_ab = jnp.where(strict, amat[:c, :c], 0.0)
            a_ak = jnp.where(strict, amat[:c, c:], 0.0)
            a_rb = jnp.where(incl, amat[c:, :c], 0.0)
            a_rk = jnp.where(incl, amat[c:, c:], 0.0)
            inv = eye + a_ab
            pw = a_ab
            for _ in range(int(math.log2(c)) - 1):
                pw = _dot(pw, pw, HI)
                inv = inv + _dot(inv, pw, HI)
            s0 = state_s[h]
            m1 = _dot_nt(lhs, s0, HI)
            vh = vv[:, sl]
            u = _dot(inv, m1[:c] + _dot(a_ak, vh, HI), HI)
            y = m1[c:] + _dot(a_rb, u, HI) + _dot(a_rk, vh, HI)
            upd = _dot_tn(jnp.concatenate([u, vh], axis=0),
                          jnp.concatenate([btg[:, sl], kng[:, sl]], axis=0), HI)
            state_s[h] = s0 * etot[0:1, sl] + upd
            y_s[rows, sl] = y
        return carry

    lax.fori_loop(0, tb // c, chunk_body, 0)

    y = y_s[...]
    mean = _dot(y, hsum, HI) * (1.0 / HEAD_DIM)
    dy = y - mean
    var = _dot(dy * dy, hsum, HI) * (1.0 / HEAD_DIM)
    yn = dy * lax.rsqrt(var + RWKV_GN_EPS) * lng_ref[...] + lnb_ref[...]
    bonus = _dot(r * kn * rk_ref[...], hsum, HI) * v
    o_ref[...] = ((yn + bonus) * g).astype(o_ref.dtype)


def _chunk_masks(tb, c):
    i = np.arange(tb)
    same = (i[:, None] // c) == (i[None, :] // c)
    ltri = same & (i[None, :] <= i[:, None])
    return jnp.asarray(ltri, F32), jnp.asarray(same, F32)


def _head_sum_matrix(width):
    i = np.arange(width)
    return jnp.asarray((i[:, None] // HEAD_DIM) == (i[None, :] // HEAD_DIM), F32)


def rwkv_time_mix(p, batch, seq, mu, w0, w_up, a0, a_up, g_up, k_k, k_a, r_k, ln_g, ln_b):
    t = batch * seq
    tb = min(WKV_BLOCK, seq)
    nblk = seq // tb
    ltri, same = _chunk_masks(tb, WKV_CHUNK)
    hsum = _head_sum_matrix(RWKV_DIM)
    row = lambda a: a.reshape(1, -1)
    consts = [row(mu), row(w0), w_up, row(a0), a_up, g_up, row(k_k), row(k_a), row(r_k), row(ln_g), row(ln_b),
              ltri, same, hsum]
    buf = lambda: pltpu.VMEM((tb, RWKV_DIM), F32)
    return pl.pallas_call(
        _rwkv_kernel,
        grid=(batch, nblk),
        in_specs=[pl.BlockSpec((tb, RWKV_IN), lambda b, j: (b * nblk + j, 0))]
                 + [_const_spec(a.shape) for a in consts],
        out_specs=pl.BlockSpec((tb, RWKV_DIM), lambda b, j: (b * nblk + j, 0)),
        out_shape=jax.ShapeDtypeStruct((t, RWKV_DIM), BF16),
        scratch_shapes=[pltpu.VMEM((RWKV_HEADS, HEAD_DIM, HEAD_DIM), F32), pltpu.VMEM((1, RWKV_IN), F32)]
                       + [buf() for _ in range(9)],
        compiler_params=_params(("parallel", "arbitrary")),
        name="rwkv_time_mix",
    )(p, *consts)


def _attn_kernel(q_ref, k_ref, v_ref, bias_ref, o_ref, acc_s, m_s, l_s):
    seq = q_ref.shape[0]
    w = ATTN_W
    scale = HEAD_DIM ** -0.5

    def rows_of(start, dil):
        return pl.ds(start, w) if dil == 1 else pl.ds(start, w, stride=dil)

    def block(pi, dil, start, with_prev):
        rows = rows_of(start, dil)
        q = (q_ref[rows, :] * scale).astype(BF16)
        kk = k_ref[rows, :].astype(BF16)
        vv = v_ref[rows, :].astype(BF16)
        if with_prev:
            prev = rows_of(start - dil * w, dil)
            kk = jnp.concatenate([k_ref[prev, :].astype(BF16), kk], axis=0)
            vv = jnp.concatenate([v_ref[prev, :].astype(BF16), vv], axis=0)
        accs, ms, ls = [], [], []
        for h in range(2):
            sl = slice(h * HEAD_DIM, (h + 1) * HEAD_DIM)
            bias = bias_ref[pi, h] if with_prev else bias_ref[pi, h, :, w:]
            s = _dot_nt(q[:, sl], kk[:, sl]) + bias
            m = jnp.max(s, axis=-1, keepdims=True)
            pr = jnp.exp(s - m)
            l = jnp.sum(pr, axis=-1, keepdims=True)
            accs.append(_dot(pr.astype(BF16), vv[:, sl]))
            ms.append(jnp.broadcast_to(m, (w, HEAD_DIM)))
            ls.append(jnp.broadcast_to(l, (w, HEAD_DIM)))
        acc_s[pi, rows, :] = jnp.concatenate(accs, axis=1)
        m_s[pi, rows, :] = jnp.concatenate(ms, axis=1)
        l_s[pi, rows, :] = jnp.concatenate(ls, axis=1)

    for pi, (window, dil) in enumerate(DILATED_PATTERNS):
        nb = seq // (dil * w)

        def first_body(r, carry, pi=pi, dil=dil):
            block(pi, dil, r, False)
            return carry

        def rest_body(i, carry, pi=pi, dil=dil):
            r = i % dil
            n = 1 + i // dil
            block(pi, dil, r + n * (dil * w), True)
            return carry

        lax.fori_loop(0, dil, first_body, 0)
        lax.fori_loop(0, dil * (nb - 1), rest_body, 0)

    mt = 256

    def merge_body(i, carry):
        rows = pl.ds(pl.multiple_of(i * mt, mt), mt)
        m0, m1, m2 = m_s[0, rows, :], m_s[1, rows, :], m_s[2, rows, :]
        mx = jnp.maximum(jnp.maximum(m0, m1), m2)
        w0, w1, w2 = jnp.exp(m0 - mx), jnp.exp(m1 - mx), jnp.exp(m2 - mx)
        num = w0 * acc_s[0, rows, :] + w1 * acc_s[1, rows, :] + w2 * acc_s[2, rows, :]
        den = w0 * l_s[0, rows, :] + w1 * l_s[1, rows, :] + w2 * l_s[2, rows, :]
        o_ref[rows, :] = (num / den).astype(o_ref.dtype)
        return carry

    lax.fori_loop(0, seq // mt, merge_body, 0)


def _t5_bucket(dist):
    max_exact = NUM_BUCKETS // 2
    large = max_exact + (np.log(np.maximum(dist, max_exact) / max_exact)
                         / math.log(MAX_DISTANCE / max_exact) * (NUM_BUCKETS - max_exact)).astype(np.int32)
    return np.where(dist < max_exact, dist, np.minimum(large, NUM_BUCKETS - 1)).astype(np.int32)


def _attn_bias(rel_bias):
    w = ATTN_W
    i = np.arange(w)[:, None]
    j = np.arange(2 * w)[None, :]
    rel = i + w - j
    band = (rel >= 0) & (rel <= w)
    tabs = []
    for window, dil in DILATED_PATTERNS:
        bucket = _t5_bucket(np.clip(rel, 0, None) * dil)
        bias = jnp.transpose(rel_bias[bucket], (2, 0, 1)).astype(F32)
        tabs.append(jnp.where(band[None], bias, MASK_VALUE))
    return jnp.stack(tabs)


def dilated_attention(p, batch, seq, rel_bias):
    t = batch * seq
    bias = _attn_bias(rel_bias)
    col0 = RWKV_IN // LANES
    npair = ATTN_DIM // LANES
    spec = lambda off: pl.BlockSpec((seq, LANES), lambda b, hp: (b, col0 + off + hp))
    return pl.pallas_call(
        _attn_kernel,
        grid=(batch, npair),
        in_specs=[spec(0), spec(npair), spec(2 * npair),
                  pl.BlockSpec((3, 2, ATTN_W, 2 * ATTN_W), lambda b, hp: (0, hp, 0, 0))],
        out_specs=pl.BlockSpec((seq, LANES), lambda b, hp: (b, hp)),
        out_shape=jax.ShapeDtypeStruct((t, ATTN_DIM), BF16),
        scratch_shapes=[pltpu.VMEM((3, seq, LANES), F32) for _ in range(3)],
        compiler_params=_params(("parallel", "parallel")),
        name="dilated_attention",
    )(p, p, p, bias)


def _ret_kernel(q_ref, k_ref, v_ref, g_ref, cos_ref, sin_ref, dmat_ref, xi_ref, zeta_ref, gng_ref, gnb_ref,
                hsum_ref, o_ref, state_s):
    c = RET_CHUNK
    tb = q_ref.shape[0]

    @pl.when(pl.program_id(1) == 0)
    def _():
        state_s[...] = jnp.zeros_like(state_s)

    lane = lax.broadcasted_iota(jnp.int32, (c, RET_DIM), 1)
    first_half = (lane % HEAD_DIM) < (HEAD_DIM // 2)

    def rotate(x, cos, sin):
        swapped = jnp.where(first_half, pltpu.roll(x, RET_DIM - HEAD_DIM // 2, axis=1),
                            pltpu.roll(x, HEAD_DIM // 2, axis=1))
        return x * cos + swapped * sin

    def chunk_body(j, carry):
        rows = pl.ds(pl.multiple_of(j * c, c), c)
        cos, sin = cos_ref[rows, :], sin_ref[rows, :]
        q = rotate(q_ref[rows, :], cos, sin)
        k = rotate(k_ref[rows, :], cos, sin) * (HEAD_DIM ** -0.5)
        v = v_ref[rows, :].astype(BF16)
        qb = q.astype(BF16)
        kb = k.astype(BF16)
        qx = (q * xi_ref[...]).astype(BF16)
        kz = (k * zeta_ref[...]).astype(BF16)
        ys = []
        for h in range(RET_HEADS):
            sl = slice(h * HEAD_DIM, (h + 1) * HEAD_DIM)
            sc = _dot_nt(qb[:, sl], kb[:, sl]) * dmat_ref[h]
            s0 = state_s[h]
            ys.append(_dot(sc.astype(BF16), v[:, sl]) + _dot(qx[:, sl], s0.astype(BF16)))
            chunk_decay = (1.0 - 2.0 ** (-5.0 - h)) ** c
            state_s[h] = s0 * chunk_decay + _dot_tn(kz[:, sl], v[:, sl])
        y = jnp.concatenate(ys, axis=1)
        hsum = hsum_ref[...]
        mean = _dot(y, hsum, HI) * (1.0 / HEAD_DIM)
        dy = y - mean
        var = _dot(dy * dy, hsum, HI) * (1.0 / HEAD_DIM)
        yn = dy * lax.rsqrt(var + LN_EPS) * gng_ref[...] + gnb_ref[...]
        gate = g_ref[rows, :]
        o_ref[rows, :] = (gate * _sigmoid(gate) * yn).astype(o_ref.dtype)
        return carry

    lax.fori_loop(0, tb // c, chunk_body, 0)


def _ret_tables(seq):
    c = RET_CHUNK
    half = HEAD_DIM // 2
    inv = ROPE_BASE ** (-jnp.arange(half, dtype=F32) / half)
    ang = jnp.arange(seq, dtype=F32)[:, None] * inv
    cos, sin = jnp.cos(ang), jnp.sin(ang)
    cos_t = jnp.tile(jnp.concatenate([cos, cos], axis=1), (1, RET_HEADS))
    sin_t = jnp.tile(jnp.concatenate([-sin, sin], axis=1), (1, RET_HEADS))
    log_g = jnp.log1p(-jnp.exp2(-5.0 - jnp.arange(RET_HEADS, dtype=F32)))
    n = jnp.arange(c, dtype=F32)
    diff = n[:, None] - n[None, :]
    dmat = jnp.where(diff >= 0, jnp.exp(log_g[:, None, None] * jnp.maximum(diff, 0.0)), 0.0)
    zeta = jnp.exp(log_g[:, None] * (c - 1 - n))
    xi = jnp.exp(log_g[:, None] * (n + 1))
    widen = lambda tab: jnp.repeat(tab.T, HEAD_DIM, axis=1)
    return cos_t, sin_t, dmat, widen(xi), widen(zeta)


def retention(p, batch, seq, gn_g, gn_b):
    t = batch * seq
    tb = min(RET_BLOCK, seq)
    nblk = seq // tb
    cos_t, sin_t, dmat, xi, zeta = _ret_tables(seq)
    hsum = _head_sum_matrix(RET_DIM)
    col0 = (RWKV_IN + ATTN_IN) // RET_DIM
    spec = lambda off: pl.BlockSpec((tb, RET_DIM), lambda b, j: (b * nblk + j, col0 + off))
    tab = pl.BlockSpec((tb, RET_DIM), lambda b, j: (j, 0))
    consts = [dmat, xi, zeta, gn_g.reshape(1, -1), gn_b.reshape(1, -1), hsum]
    return pl.pallas_call(
        _ret_kernel,
        grid=(batch, nblk),
        in_specs=[spec(0), spec(1), spec(2), spec(3), tab, tab] + [_const_spec(a.shape) for a in consts],
        out_specs=pl.BlockSpec((tb, RET_DIM), lambda b, j: (b * nblk + j, 0)),
        out_shape=jax.ShapeDtypeStruct((t, RET_DIM), BF16),
        scratch_shapes=[pltpu.VMEM((RET_HEADS, HEAD_DIM, HEAD_DIM), F32)],
        compiler_params=_params(("parallel", "arbitrary")),
        name="retention",
    )(p, p, p, p, cos_t, sin_t, *consts)


def _outproj_kernel(ya_ref, yb_ref, yc_ref, x_ref, w_ref, g_ref, b_ref, o_ref, *, alpha):
    acc = _dot(ya_ref[...], w_ref[0:RWKV_DIM, :])
    acc += _dot(yb_ref[...], w_ref[RWKV_DIM:RWKV_DIM + ATTN_DIM, :])
    acc += _dot(yc_ref[...], w_ref[RWKV_DIM + ATTN_DIM:, :])
    o_ref[...] = _layer_norm(alpha * x_ref[...] + acc, g_ref[...], b_ref[...])


def out_projection_ln(ya, yb, yc, x2d, w_bf16, g, b, alpha):
    t, d = x2d.shape
    tm = min(ROW_TILE, t)
    rows = lambda width: pl.BlockSpec((tm, width), lambda i: (i, 0))
    return pl.pallas_call(
        functools.partial(_outproj_kernel, alpha=alpha),
        grid=(t // tm,),
        in_specs=[rows(RWKV_DIM), rows(ATTN_DIM), rows(RET_DIM), rows(d), _const_spec(w_bf16.shape),
                  _const_spec((1, d)), _const_spec((1, d))],
        out_specs=rows(d),
        out_shape=jax.ShapeDtypeStruct((t, d), F32),
        compiler_params=_params(("parallel",)),
        name="out_projection_ln",
    )(ya, yb, yc, x2d, w_bf16, g.reshape(1, -1), b.reshape(1, -1))


def _ffn_kernel(x_ref, wg_ref, wu_ref, wd_ref, g_ref, b_ref, o_ref, *, alpha, f_chunk):
    x = x_ref[...]
    xb = x.astype(BF16)
    acc = alpha * x
    for f0 in range(0, wg_ref.shape[1], f_chunk):
        gate = _dot(xb, wg_ref[:, f0:f0 + f_chunk])
        up = _dot(xb, wu_ref[:, f0:f0 + f_chunk])
        hid = (gate * _sigmoid(gate) * up).astype(BF16)
        acc += _dot(hid, wd_ref[f0:f0 + f_chunk, :])
    o_ref[...] = _layer_norm(acc, g_ref[...], b_ref[...])


def dense_ffn_ln(x2d, wg, wu, wd, g, b, alpha):
    t, d = x2d.shape
    ff = wg.shape[1]
    tm = min(ROW_TILE, t)
    f_chunk = ff // 2 if (ff // 2) % LANES == 0 else ff
    return pl.pallas_call(
        functools.partial(_ffn_kernel, alpha=alpha, f_chunk=f_chunk),
        grid=(t // tm,),
        in_specs=[pl.BlockSpec((tm, d), lambda i: (i, 0)), _const_spec(wg.shape), _const_spec(wu.shape),
                  _const_spec(wd.shape), _const_spec((1, d)), _const_spec((1, d))],
        out_specs=pl.BlockSpec((tm, d), lambda i: (i, 0)),
        out_shape=jax.ShapeDtypeStruct((t, d), F32),
        compiler_params=_params(("parallel",)),
        name="dense_ffn_ln",
    )(x2d, wg, wu, wd, g.reshape(1, -1), b.reshape(1, -1))


def _router_kernel(x_ref, w_ref, o_ref):
    o_ref[...] = _dot_nt(w_ref[...], x_ref[...], HI)


def router_logits(x2d, router):
    t, d = x2d.shape
    tm = min(ROW_TILE, t)
    wt = router.T
    return pl.pallas_call(
        _router_kernel,
        grid=(t // tm,),
        in_specs=[pl.BlockSpec((tm, d), lambda i: (i, 0)), _const_spec(wt.shape)],
        out_specs=pl.BlockSpec((N_EXPERTS, tm), lambda i: (0, i)),
        out_shape=jax.ShapeDtypeStruct((N_EXPERTS, t), F32),
        compiler_params=_params(("parallel",)),
        name="router_logits",
    )(x2d, wt)


def _expert_kernel(blk_e_ref, used_ref, x_ref, wg_ref, wu_ref, wd_ref, o_ref):
    i, j = pl.program_id(0), pl.program_id(1)

    @pl.when(i < used_ref[0])
    def _():
        xb = x_ref[...]
        gate = _dot(xb, wg_ref[0])
        up = _dot(xb, wu_ref[0])
        hid = (gate * _sigmoid(gate) * up).astype(BF16)
        part = _dot(hid, wd_ref[0])

        @pl.when(j == 0)
        def _():
            o_ref[...] = part

        @pl.when(j > 0)
        def _():
            o_ref[...] += part


def expert_ffn(xs, blk_e, n_used, wg, wu, wd, f_chunk=896):
    rows, d = xs.shape
    ff = wg.shape[2]
    tm = MOE_TILE
    grid_spec = pltpu.PrefetchScalarGridSpec(
        num_scalar_prefetch=2,
        grid=(rows // tm, ff // f_chunk),
        in_specs=[pl.BlockSpec((tm, d), lambda i, j, be, nu: (i, 0)),
                  pl.BlockSpec((1, d, f_chunk), lambda i, j, be, nu: (be[i], 0, j)),
                  pl.BlockSpec((1, d, f_chunk), lambda i, j, be, nu: (be[i], 0, j)),
                  pl.BlockSpec((1, f_chunk, d), lambda i, j, be, nu: (be[i], j, 0))],
        out_specs=pl.BlockSpec((tm, d), lambda i, j, be, nu: (i, 0)),
    )
    return pl.pallas_call(
        _expert_kernel,
        grid_spec=grid_spec,
        out_shape=jax.ShapeDtypeStruct((rows, d), F32),
        compiler_params=_params(("parallel", "arbitrary")),
        name="expert_ffn",
    )(blk_e, n_used, xs, wg, wu, wd)


def _residual_ln_kernel(x_ref, f_ref, g_ref, b_ref, o_ref, *, alpha):
    o_ref[...] = _layer_norm(alpha * x_ref[...] + f_ref[...], g_ref[...], b_ref[...])


def residual_ln(x2d, f2d, g, b, alpha):
    t, d = x2d.shape
    tm = min(ROW_TILE, t)
    rows = pl.BlockSpec((tm, d), lambda i: (i, 0))
    return pl.pallas_call(
        functools.partial(_residual_ln_kernel, alpha=alpha),
        grid=(t // tm,),
        in_specs=[rows, rows, _const_spec((1, d)), _const_spec((1, d))],
        out_specs=rows,
        out_shape=jax.ShapeDtypeStruct((t, d), F32),
        compiler_params=_params(("parallel",)),
        name="residual_ln",
    )(x2d, f2d, g.reshape(1, -1), b.reshape(1, -1))


def moe_ffn_ln(x2d, router, wg, wu, wd, g, b, alpha):
    t, d = x2d.shape
    tm = MOE_TILE
    logits = router_logits(x2d, router).T
    top_val, top_idx = lax.top_k(logits, TOP_K)
    gates = jax.nn.softmax(top_val, axis=-1)
    member = (top_idx[:, :, None] == jnp.arange(N_EXPERTS)[None, None, :]).any(axis=1)
    counts = member.sum(axis=0).astype(jnp.int32)
    rank = jnp.cumsum(member.astype(jnp.int32), axis=0) - member.astype(jnp.int32)
    padded = (counts + tm - 1) // tm * tm
    pad_ends = jnp.cumsum(padded)
    pad_starts = pad_ends - padded
    pos = jnp.take_along_axis(pad_starts[None, :] + rank, top_idx, axis=1)
    rows = t * TOP_K + N_EXPERTS * tm
    n_blk = rows // tm
    tok = jnp.broadcast_to(jnp.arange(t, dtype=jnp.int32)[:, None], (t, TOP_K))
    src = jnp.zeros((rows,), jnp.int32).at[pos.reshape(-1)].set(tok.reshape(-1))
    live = jnp.zeros((rows,), jnp.bool_).at[pos.reshape(-1)].set(True)
    xs = jnp.where(live[:, None], x2d.astype(BF16)[src], 0)
    blk_e = jnp.minimum(jnp.searchsorted(pad_ends, jnp.arange(n_blk, dtype=jnp.int32) * tm, side='right'),
                        N_EXPERTS - 1).astype(jnp.int32)
    n_used = (pad_ends[-1:] // tm).astype(jnp.int32)
    ys = expert_ffn(xs, blk_e, n_used, wg, wu, wd)
    f = (ys[pos.reshape(-1)].reshape(t, TOP_K, d) * gates[:, :, None]).sum(axis=1)
    return residual_ln(x2d, f, g, b, alpha)


def kernel(x, w_in, w_out, rwkv_mu, rwkv_w0, rwkv_w_up, rwkv_a0, rwkv_a_up, rwkv_g_up, rwkv_k_k, rwkv_k_a,
           rwkv_r_k, rwkv_ln_g, rwkv_ln_b, ret_gn_g, ret_gn_b, rel_bias, ln_g, ln_b, ffn_w_gate, ffn_w_up,
           ffn_w_down, moe_router, moe_w_gate, moe_w_up, moe_w_down):
    batch, seq, d = x.shape
    depth = w_in.shape[0]
    alpha = (2 * depth) ** 0.25
    h = x.reshape(batch * seq, d)
    for layer in range(depth):
        p = in_projection(h, w_in[layer].astype(BF16))
        ya = rwkv_time_mix(p, batch, seq, rwkv_mu[layer], rwkv_w0[layer], rwkv_w_up[layer], rwkv_a0[layer],
                           rwkv_a_up[layer], rwkv_g_up[layer], rwkv_k_k[layer], rwkv_k_a[layer],
                           rwkv_r_k[layer], rwkv_ln_g[layer], rwkv_ln_b[layer])
        yb = dilated_attention(p, batch, seq, rel_bias)
        yc = retention(p, batch, seq, ret_gn_g[layer], ret_gn_b[layer])
        h = out_projection_ln(ya, yb, yc, h, w_out[layer].astype(BF16), ln_g[layer, 0], ln_b[layer, 0], alpha)
        j = layer // 2
        if layer % 2 == 0:
            h = dense_ffn_ln(h, ffn_w_gate[j].astype(BF16), ffn_w_up[j].astype(BF16),
                             ffn_w_down[j].astype(BF16), ln_g[layer, 1], ln_b[layer, 1], alpha)
        else:
            h = moe_ffn_ln(h, moe_router[j], moe_w_gate[j].astype(BF16), moe_w_up[j].astype(BF16),
                           moe_w_down[j].astype(BF16), ln_g[layer, 1], ln_b[layer, 1], alpha)
    return h.reshape(batch, seq, d)
```

```python
import functools
import math

import numpy as np
import jax
import jax.numpy as jnp
from jax import lax
from jax.experimental import pallas as pl
from jax.experimental.pallas import tpu as pltpu

F32 = jnp.float32
BF16 = jnp.bfloat16
HI = lax.Precision.HIGHEST

HEAD_DIM = 64
RWKV_HEADS = 4
ATTN_HEADS = 8
RET_HEADS = 4
RWKV_DIM = RWKV_HEADS * HEAD_DIM
ATTN_DIM = ATTN_HEADS * HEAD_DIM
RET_DIM = RET_HEADS * HEAD_DIM
DECAY_LORA = 64
ICL_LORA = 64
GATE_LORA = 128
RWKV_IN = 3 * RWKV_DIM + DECAY_LORA + ICL_LORA + GATE_LORA
ATTN_IN = 3 * ATTN_DIM
RET_IN = 4 * RET_DIM
RWKV_GN_EPS = 64e-5
DECAY_SCALE = math.exp(-0.5)
DILATED_PATTERNS = ((128, 1), (512, 4), (2048, 16))
NUM_BUCKETS = 32
MAX_DISTANCE = 2048
ROPE_BASE = 10000.0
N_EXPERTS = 8
TOP_K = 2
LN_EPS = 1e-5

LANES = 128
WKV_CHUNK = 64
WKV_BLOCK = 256
WKV_GROUP = 2
ATTN_W = 128
ATTN_UNROLL = 4
RET_CHUNK = 128
RET_BLOCK = 1024
ROW_TILE = 512
MOE_TILE = 512
MASK_VALUE = -1e30
VMEM_LIMIT = 56 * 1024 * 1024


def _dot(a, b, prec=None):
    return jnp.dot(a, b, preferred_element_type=F32, precision=prec)


def _dot_nt(a, b, prec=None):
    return lax.dot_general(a, b, (((1,), (1,)), ((), ())), preferred_element_type=F32, precision=prec)


def _dot_tn(a, b, prec=None):
    return lax.dot_general(a, b, (((0,), (0,)), ((), ())), preferred_element_type=F32, precision=prec)


_DIMS = {"nn": (((1,), (0,)), ((), ())), "nt": (((1,), (1,)), ((), ())), "tn": (((0,), (0,)), ((), ()))}


def _split(x, terms):
    parts = []
    for _ in range(terms - 1):
        hi = x.astype(BF16)
        parts.append(hi)
        x = x - hi.astype(F32)
    parts.append(x.astype(BF16))
    return parts


def _mm(a, b, kind="nn", passes=3):
    dg = lambda p, q: lax.dot_general(p, q, _DIMS[kind], preferred_element_type=F32)
    if passes == 1:
        return dg(a.astype(BF16), b.astype(BF16))
    ah, al = _split(a, 2)
    bh, bl = _split(b, 2)
    return dg(ah, bh) + (dg(al, bh) + dg(ah, bl))


def _mm_ones(x, ones_bf16, ones_first=False, terms=3):
    parts = _split(x, terms)
    if ones_first:
        out = [lax.dot_general(ones_bf16, p, _DIMS["nn"], preferred_element_type=F32) for p in parts]
    else:
        out = [lax.dot_general(p, ones_bf16, _DIMS["nn"], preferred_element_type=F32) for p in parts]
    acc = out[-1]
    for o in reversed(out[:-1]):
        acc = acc + o
    return acc


def _sigmoid(x):
    return 1.0 / (1.0 + jnp.exp(-x))


def _layer_norm(z, g, b):
    mu = jnp.mean(z, axis=-1, keepdims=True)
    d = z - mu
    var = jnp.mean(d * d, axis=-1, keepdims=True)
    return d * lax.rsqrt(var + LN_EPS) * g + b


def _params(sem, vmem=VMEM_LIMIT):
    return pltpu.CompilerParams(dimension_semantics=sem, vmem_limit_bytes=vmem)


def _const_spec(shape):
    nd = len(shape)
    return pl.BlockSpec(shape, lambda *_: (0,) * nd)


def _inproj_kernel(x_ref, w_ref, o_ref, *, n_chunk):
    xb = x_ref[...].astype(BF16)
    for n0 in range(0, o_ref.shape[1], n_chunk):
        o_ref[:, n0:n0 + n_chunk] = _dot(xb, w_ref[:, n0:n0 + n_chunk])


def in_projection(x2d, w_bf16):
    t, d = x2d.shape
    n = w_bf16.shape[1]
    tm = min(ROW_TILE, t)
    return pl.pallas_call(
        functools.partial(_inproj_kernel, n_chunk=512),
        grid=(t // tm,),
        in_specs=[pl.BlockSpec((tm, d), lambda i: (i, 0)), _const_spec((d, n))],
        out_specs=pl.BlockSpec((tm, n), lambda i: (i, 0)),
        out_shape=jax.ShapeDtypeStruct((t, n), F32),
        compiler_params=_params(("parallel",)),
        name="in_projection",
    )(x2d, w_bf16)


def _rwkv_kernel(p_ref, mu_ref, w0_ref, wup_ref, a0_ref, aup_ref, gup_ref, kk_ref, ka_ref, rk_ref,
                 lng_ref, lnb_ref, ltri_ref, same_ref, hsum_ref, o_ref,
                 state_s, prev_s, kt_s, rt_s, bt_s, kn_s, v_s, btg_s, kng_s, etot_s, y_s, rp_s, y0_s, gt_s, zt_s):
    c = WKV_CHUNK
    tb = p_ref.shape[0]
    d = RWKV_DIM

    @pl.when(pl.program_id(1) == 0)
    def _():
        state_s[...] = jnp.zeros_like(state_s)
        prev_s[...] = jnp.zeros_like(prev_s)

    p = p_ref[...]
    row = lax.broadcasted_iota(jnp.int32, p.shape, 0)
    shifted = jnp.where(row == 0, prev_s[...], pltpu.roll(p, 1, axis=0))
    prev_s[...] = p[tb - 1:tb, :]
    ps = p + (shifted - p) * mu_ref[...]
    r = ps[:, 0:d]
    k = ps[:, d:2 * d]
    v = ps[:, 2 * d:3 * d]
    xw = ps[:, 3 * d:3 * d + DECAY_LORA]
    xa = ps[:, 3 * d + DECAY_LORA:3 * d + DECAY_LORA + ICL_LORA]
    xg = ps[:, 3 * d + DECAY_LORA + ICL_LORA:]

    hsum = hsum_ref[...]
    logw = -DECAY_SCALE * _sigmoid(w0_ref[...] + _mm(jnp.tanh(xw), wup_ref[...]))
    a = _sigmoid(a0_ref[...] + _mm(xa, aup_ref[...]))
    g = _dot(_sigmoid(xg).astype(BF16), gup_ref[...].astype(BF16))
    kap = k * kk_ref[...]
    kap = kap / jnp.maximum(jnp.sqrt(_mm_ones(kap * kap, hsum)), 1e-12)
    kn = k * (1.0 + (a - 1.0) * ka_ref[...])
    cum = _mm_ones(logw, ltri_ref[...], ones_first=True)
    tot = _mm_ones(logw, same_ref[...], ones_first=True)
    e_neg = jnp.exp(-cum)
    e_rem = jnp.exp(tot - cum)
    nb = -(a * kap)
    kt_s[...] = kap * jnp.exp(cum - logw)
    rt_s[...] = r * jnp.exp(cum)
    bt_s[...] = nb * e_neg
    kn_s[...] = kn * e_neg
    btg_s[...] = nb * e_rem
    kng_s[...] = kn * e_rem
    etot_s[...] = jnp.exp(tot)
    v_s[...] = v

    ri = lax.broadcasted_iota(jnp.int32, (c, c), 0)
    ci = lax.broadcasted_iota(jnp.int32, (c, c), 1)
    strict = ci < ri
    incl = ci <= ri
    eye = (ci == ri).astype(F32)

    nchunk = tb // c
    cat0 = lambda x, y: jnp.concatenate([x, y], axis=0)
    cat1 = lambda x, y: jnp.concatenate([x, y], axis=1)
    levels = int(math.log2(c)) - 1

    for j0 in range(0, nchunk, WKV_GROUP):
        probs = [(j, h) for j in range(j0, min(j0 + WKV_GROUP, nchunk)) for h in range(RWKV_HEADS)]
        tile = lambda ref, j, h: ref[j * c:(j + 1) * c, h * HEAD_DIM:(h + 1) * HEAD_DIM]
        get = lambda ref: [tile(ref, j, h) for j, h in probs]
        kt, rt, vv, btg = get(kt_s), get(rt_s), get(v_s), get(btg_s)
        amat = [_mm(cat0(k_, r_), cat0(b_, n_), "nt") for k_, r_, b_, n_ in zip(kt, rt, get(bt_s), get(kn_s))]
        a_ab = [jnp.where(strict, m[:c, :c], 0.0) for m in amat]
        a_kr = [cat0(jnp.where(strict, m[:c, c:], 0.0), jnp.where(incl, m[c:, c:], 0.0)) for m in amat]
        a_rb = [jnp.where(incl, m[c:, :c], 0.0) for m in amat]
        inv = [eye + m for m in a_ab]
        pw = [_mm(m, m) for m in a_ab]
        for lvl in range(levels):
            if lvl < levels - 1:
                both = [_mm(cat0(x_, p_), p_) for x_, p_ in zip(inv, pw)]
                inv = [x_ + b_[:c] for x_, b_ in zip(inv, both)]
                pw = [b_[c:] for b_ in both]
            else:
                inv = [x_ + _mm(x_, p_) for x_, p_ in zip(inv, pw)]
        av = [_mm(m, v_) for m, v_ in zip(a_kr, vv)]
        wu = [_mm(x_, cat1(k_, a_[:c])) for x_, k_, a_ in zip(inv, kt, av)]
        rw = [_mm(m, w_) for m, w_ in zip(a_rb, wu)]
        gz = [_mm(b_, w_, "tn") for b_, w_ in zip(btg, wu)]
        kv = [_mm(n_, v_, "tn") for n_, v_ in zip(get(kng_s), vv)]
        for i, (j, h) in enumerate(probs):
            rows, sl = slice(j * c, (j + 1) * c), slice(h * HEAD_DIM, (h + 1) * HEAD_DIM)
            rp_s[rows, sl] = rt[i] + rw[i][:, :HEAD_DIM]
            y0_s[rows, sl] = rw[i][:, HEAD_DIM:] + av[i][c:]
            g_diag = jnp.where(ci == ri, jnp.broadcast_to(etot_s[j * c:j * c + 1, sl], (c, c)), 0.0)
            gt_s[rows, sl] = g_diag + gz[i][:, :HEAD_DIM]
            zt_s[rows, sl] = gz[i][:, HEAD_DIM:] + kv[i]

    states = [state_s[h] for h in range(RWKV_HEADS)]
    for j in range(nchunk):
        rows = slice(j * c, (j + 1) * c)
        ry = [_mm(cat0(rp_s[rows, h * HEAD_DIM:(h + 1) * HEAD_DIM], gt_s[rows, h * HEAD_DIM:(h + 1) * HEAD_DIM]),
                  states[h]) for h in range(RWKV_HEADS)]
        for h in range(RWKV_HEADS):
            sl = slice(h * HEAD_DIM, (h + 1) * HEAD_DIM)
            y_s[rows, sl] = ry[h][:c] + y0_s[rows, sl]
            states[h] = ry[h][c:] + zt_s[rows, sl]
    for h in range(RWKV_HEADS):
        state_s[h] = states[h]

    y = y_s[...]
    mean = _mm_ones(y, hsum) * (1.0 / HEAD_DIM)
    dy = y - mean
    var = _mm_ones(dy * dy, hsum) * (1.0 / HEAD_DIM)
    yn = dy * lax.rsqrt(var + RWKV_GN_EPS) * lng_ref[...] + lnb_ref[...]
    bonus = _mm_ones(r * kn * rk_ref[...], hsum) * v
    o_ref[...] = ((yn + bonus) * g).astype(o_ref.dtype)


def _chunk_masks(tb, c):
    i = np.arange(tb)
    same = (i[:, None] // c) == (i[None, :] // c)
    ltri = same & (i[None, :] <= i[:, None])
    return jnp.asarray(ltri, BF16), jnp.asarray(same, BF16)


def _head_sum_matrix(width):
    i = np.arange(width)
    return jnp.asarray((i[:, None] // HEAD_DIM) == (i[None, :] // HEAD_DIM), BF16)


def rwkv_time_mix(p, batch, seq, mu, w0, w_up, a0, a_up, g_up, k_k, k_a, r_k, ln_g, ln_b):
    t = batch * seq
    tb = min(WKV_BLOCK, seq)
    nblk = seq // tb
    ltri, same = _chunk_masks(tb, WKV_CHUNK)
    hsum = _head_sum_matrix(RWKV_DIM)
    row = lambda a: a.reshape(1, -1)
    consts = [row(mu), row(w0), w_up, row(a0), a_up, g_up, row(k_k), row(k_a), row(r_k), row(ln_g), row(ln_b),
              ltri, same, hsum]
    buf = lambda: pltpu.VMEM((tb, RWKV_DIM), F32)
    return pl.pallas_call(
        _rwkv_kernel,
        grid=(batch, nblk),
        in_specs=[pl.BlockSpec((tb, RWKV_IN), lambda b, j: (b * nblk + j, 0))]
                 + [_const_spec(a.shape) for a in consts],
        out_specs=pl.BlockSpec((tb, RWKV_DIM), lambda b, j: (b * nblk + j, 0)),
        out_shape=jax.ShapeDtypeStruct((t, RWKV_DIM), BF16),
        scratch_shapes=[pltpu.VMEM((RWKV_HEADS, HEAD_DIM, HEAD_DIM), F32), pltpu.VMEM((1, RWKV_IN), F32)]
                       + [buf() for _ in range(13)],
        compiler_params=_params(("parallel", "arbitrary")),
        name="rwkv_time_mix",
    )(p, *consts)


def _attn_kernel(q_ref, k_ref, v_ref, bias_ref, o_ref, acc_s, m_s, l_s):
    seq = q_ref.shape[0]
    w = ATTN_W
    scale = HEAD_DIM ** -0.5

    def rows_of(start, dil):
        return pl.ds(start, w) if dil == 1 else pl.ds(start, w, stride=dil)

    lane = lax.broadcasted_iota(jnp.int32, (w, LANES), 1)
    head0 = lane < HEAD_DIM
    zero = jnp.zeros((), BF16)
    one = jnp.ones((), BF16)

    def group(pi, dil, g, firsts):
        rows_l, q_l, k_l, v_l = [], [], [], []
        for u, first in enumerate(firsts):
            b = g * len(firsts) + u
            start = (b % dil) + (b // dil) * (dil * w)
            rows = rows_of(start, dil)
            q = (q_ref[rows, :] * scale).astype(BF16)
            kk = k_ref[rows, :].astype(BF16)
            vv = v_ref[rows, :].astype(BF16)
            if not first:
                prev = rows_of(start - dil * w, dil)
                kk = jnp.concatenate([k_ref[prev, :].astype(BF16), kk], axis=0)
                vv = jnp.concatenate([v_ref[prev, :].astype(BF16), vv], axis=0)
            rows_l.append(rows)
            q_l.append(q)
            k_l.append(kk)
            v_l.append(vv)
        s = [[_dot_nt(jnp.where(head0 if h == 0 else ~head0, q, zero), kk)
              + (bias_ref[pi, h, :, w:] if first else bias_ref[pi, h])
              for h in range(2)] for q, kk, first in zip(q_l, k_l, firsts)]
        m = [[jnp.max(sh, axis=-1, keepdims=True) for sh in su] for su in s]
        pr = [[jnp.exp(sh - mh).astype(BF16) for sh, mh in zip(su, mu)] for su, mu in zip(s, m)]
        kmask = lambda vv: lax.broadcasted_iota(jnp.int32, vv.shape, 1) < HEAD_DIM
        res = [[_dot(pu[0], jnp.where(kmask(vv), vv, one)), _dot(pu[1], jnp.where(kmask(vv), one, vv))]
               for pu, vv in zip(pr, v_l)]
        for rows, ru, mu in zip(rows_l, res, m):
            acc_s[pi, rows, :] = jnp.where(head0, ru[0], ru[1])
            l_s[pi, rows, :] = jnp.where(head0, ru[1], ru[0])
            m_s[pi, rows, :] = jnp.where(head0, mu[0], mu[1])

    n_groups = (seq // w) // ATTN_UNROLL
    for pi, (window, dil) in enumerate(DILATED_PATTERNS):
        flags = [tuple((g * ATTN_UNROLL + u) < dil for u in range(ATTN_UNROLL)) for g in range(n_groups)]
        g0 = 0
        while g0 < n_groups:
            g1 = g0
            while g1 < n_groups and flags[g1] == flags[g0]:
                g1 += 1
            if g1 - g0 == 1:
                group(pi, dil, g0, flags[g0])
            else:
                def body(g, carry, pi=pi, dil=dil, firsts=flags[g0]):
                    group(pi, dil, g, firsts)
                    return carry
                lax.fori_loop(g0, g1, body, 0)
            g0 = g1

    mt = 256

    def merge_body(i, carry):
        rows = pl.ds(pl.multiple_of(i * mt, mt), mt)
        m0, m1, m2 = m_s[0, rows, :], m_s[1, rows, :], m_s[2, rows, :]
        mx = jnp.maximum(jnp.maximum(m0, m1), m2)
        w0, w1, w2 = jnp.exp(m0 - mx), jnp.exp(m1 - mx), jnp.exp(m2 - mx)
        num = w0 * acc_s[0, rows, :] + w1 * acc_s[1, rows, :] + w2 * acc_s[2, rows, :]
        swap = lambda x: pltpu.roll(x, HEAD_DIM, axis=1)
        den = w0 * swap(l_s[0, rows, :]) + w1 * swap(l_s[1, rows, :]) + w2 * swap(l_s[2, rows, :])
        o_ref[rows, :] = (num / den).astype(o_ref.dtype)
        return carry

    lax.fori_loop(0, seq // mt, merge_body, 0)


def _t5_bucket(dist):
    max_exact = NUM_BUCKETS // 2
    large = max_exact + (np.log(np.maximum(dist, max_exact) / max_exact)
                         / math.log(MAX_DISTANCE / max_exact) * (NUM_BUCKETS - max_exact)).astype(np.int32)
    return np.where(dist < max_exact, dist, np.minimum(large, NUM_BUCKETS - 1)).astype(np.int32)


def _attn_bias(rel_bias):
    w = ATTN_W
    i = np.arange(w)[:, None]
    j = np.arange(2 * w)[None, :]
    rel = i + w - j
    band = (rel >= 0) & (rel <= w)
    tabs = []
    for window, dil in DILATED_PATTERNS:
        bucket = _t5_bucket(np.clip(rel, 0, None) * dil)
        bias = jnp.transpose(rel_bias[bucket], (2, 0, 1)).astype(F32)
        tabs.append(jnp.where(band[None], bias, MASK_VALUE))
    return jnp.stack(tabs)


def dilated_attention(p, batch, seq, rel_bias):
    t = batch * seq
    bias = _attn_bias(rel_bias)
    col0 = RWKV_IN // LANES
    npair = ATTN_DIM // LANES
    spec = lambda off: pl.BlockSpec((seq, LANES), lambda b, hp: (b, col0 + off + hp))
    return pl.pallas_call(
        _attn_kernel,
        grid=(batch, npair),
        in_specs=[spec(0), spec(npair), spec(2 * npair),
                  pl.BlockSpec((3, 2, ATTN_W, 2 * ATTN_W), lambda b, hp: (0, hp, 0, 0))],
        out_specs=pl.BlockSpec((seq, LANES), lambda b, hp: (b, hp)),
        out_shape=jax.ShapeDtypeStruct((t, ATTN_DIM), BF16),
        scratch_shapes=[pltpu.VMEM((3, seq, LANES), F32) for _ in range(3)],
        compiler_params=_params(("parallel", "parallel")),
        name="dilated_attention",
    )(p, p, p, bias)


def _ret_kernel(q_ref, k_ref, v_ref, g_ref, cos_ref, sin_ref, dmat_ref, xi_ref, zeta_ref, gng_ref, gnb_ref,
                hsum_ref, o_ref, state_s):
    c = RET_CHUNK
    tb = q_ref.shape[0]

    @pl.when(pl.program_id(1) == 0)
    def _():
        state_s[...] = jnp.zeros_like(state_s)

    lane = lax.broadcasted_iota(jnp.int32, (c, RET_DIM), 1)
    first_half = (lane % HEAD_DIM) < (HEAD_DIM // 2)

    def rotate(x, cos, sin):
        swapped = jnp.where(first_half, pltpu.roll(x, RET_DIM - HEAD_DIM // 2, axis=1),
                            pltpu.roll(x, HEAD_DIM // 2, axis=1))
        return x * cos + swapped * sin

    def chunk_body(j, carry):
        rows = pl.ds(pl.multiple_of(j * c, c), c)
        cos, sin = cos_ref[rows, :], sin_ref[rows, :]
        q = rotate(q_ref[rows, :], cos, sin)
        k = rotate(k_ref[rows, :], cos, sin) * (HEAD_DIM ** -0.5)
        v = v_ref[rows, :].astype(BF16)
        qb = q.astype(BF16)
        kb = k.astype(BF16)
        qx = (q * xi_ref[...]).astype(BF16)
        kz = (k * zeta_ref[...]).astype(BF16)
        ys = []
        for h in range(RET_HEADS):
            sl = slice(h * HEAD_DIM, (h + 1) * HEAD_DIM)
            sc = _dot_nt(qb[:, sl], kb[:, sl]) * dmat_ref[h]
            s0 = state_s[h]
            ys.append(_dot(sc.astype(BF16), v[:, sl]) + _dot(qx[:, sl], s0.astype(BF16)))
            chunk_decay = (1.0 - 2.0 ** (-5.0 - h)) ** c
            state_s[h] = s0 * chunk_decay + _dot_tn(kz[:, sl], v[:, sl])
        y = jnp.concatenate(ys, axis=1)
        hsum = hsum_ref[...]
        mean = _mm_ones(y, hsum) * (1.0 / HEAD_DIM)
        dy = y - mean
        var = _mm_ones(dy * dy, hsum) * (1.0 / HEAD_DIM)
        yn = dy * lax.rsqrt(var + LN_EPS) * gng_ref[...] + gnb_ref[...]
        gate = g_ref[rows, :]
        o_ref[rows, :] = (gate * _sigmoid(gate) * yn).astype(o_ref.dtype)
        return carry

    lax.fori_loop(0, tb // c, chunk_body, 0)


def _ret_tables(seq):
    c = RET_CHUNK
    half = HEAD_DIM // 2
    inv = ROPE_BASE ** (-jnp.arange(half, dtype=F32) / half)
    ang = jnp.arange(seq, dtype=F32)[:, None] * inv
    cos, sin = jnp.cos(ang), jnp.sin(ang)
    cos_t = jnp.tile(jnp.concatenate([cos, cos], axis=1), (1, RET_HEADS))
    sin_t = jnp.tile(jnp.concatenate([-sin, sin], axis=1), (1, RET_HEADS))
    log_g = jnp.log1p(-jnp.exp2(-5.0 - jnp.arange(RET_HEADS, dtype=F32)))
    n = jnp.arange(c, dtype=F32)
    diff = n[:, None] - n[None, :]
    dmat = jnp.where(diff >= 0, jnp.exp(log_g[:, None, None] * jnp.maximum(diff, 0.0)), 0.0)
    zeta = jnp.exp(log_g[:, None] * (c - 1 - n))
    xi = jnp.exp(log_g[:, None] * (n + 1))
    widen = lambda tab: jnp.repeat(tab.T, HEAD_DIM, axis=1)
    return cos_t, sin_t, dmat, widen(xi), widen(zeta)


def retention(p, batch, seq, gn_g, gn_b):
    t = batch * seq
    tb = min(RET_BLOCK, seq)
    nblk = seq // tb
    cos_t, sin_t, dmat, xi, zeta = _ret_tables(seq)
    hsum = _head_sum_matrix(RET_DIM)
    col0 = (RWKV_IN + ATTN_IN) // RET_DIM
    spec = lambda off: pl.BlockSpec((tb, RET_DIM), lambda b, j: (b * nblk + j, col0 + off))
    tab = pl.BlockSpec((tb, RET_DIM), lambda b, j: (j, 0))
    consts = [dmat, xi, zeta, gn_g.reshape(1, -1), gn_b.reshape(1, -1), hsum]
    return pl.pallas_call(
        _ret_kernel,
        grid=(batch, nblk),
        in_specs=[spec(0), spec(1), spec(2), spec(3), tab, tab] + [_const_spec(a.shape) for a in consts],
        out_specs=pl.BlockSpec((tb, RET_DIM), lambda b, j: (b * nblk + j, 0)),
        out_shape=jax.ShapeDtypeStruct((t, RET_DIM), BF16),
        scratch_shapes=[pltpu.VMEM((RET_HEADS, HEAD_DIM, HEAD_DIM), F32)],
        compiler_params=_params(("parallel", "arbitrary")),
        name="retention",
    )(p, p, p, p, cos_t, sin_t, *consts)


def _outproj_kernel(ya_ref, yb_ref, yc_ref, x_ref, w_ref, g_ref, b_ref, o_ref, *, alpha):
    acc = _dot(ya_ref[...], w_ref[0:RWKV_DIM, :])
    acc += _dot(yb_ref[...], w_ref[RWKV_DIM:RWKV_DIM + ATTN_DIM, :])
    acc += _dot(yc_ref[...], w_ref[RWKV_DIM + ATTN_DIM:, :])
    o_ref[...] = _layer_norm(alpha * x_ref[...] + acc, g_ref[...], b_ref[...])


def out_projection_ln(ya, yb, yc, x2d, w_bf16, g, b, alpha):
    t, d = x2d.shape
    tm = min(ROW_TILE, t)
    rows = lambda width: pl.BlockSpec((tm, width), lambda i: (i, 0))
    return pl.pallas_call(
        functools.partial(_outproj_kernel, alpha=alpha),
        grid=(t // tm,),
        in_specs=[rows(RWKV_DIM), rows(ATTN_DIM), rows(RET_DIM), rows(d), _const_spec(w_bf16.shape),
                  _const_spec((1, d)), _const_spec((1, d))],
        out_specs=rows(d),
        out_shape=jax.ShapeDtypeStruct((t, d), F32),
        compiler_params=_params(("parallel",)),
        name="out_projection_ln",
    )(ya, yb, yc, x2d, w_bf16, g.reshape(1, -1), b.reshape(1, -1))


def _ffn_kernel(x_ref, wg_ref, wu_ref, wd_ref, g_ref, b_ref, o_ref, *, alpha, f_chunk):
    x = x_ref[...]
    xb = x.astype(BF16)
    acc = alpha * x
    for f0 in range(0, wg_ref.shape[1], f_chunk):
        gate = _dot(xb, wg_ref[:, f0:f0 + f_chunk])
        up = _dot(xb, wu_ref[:, f0:f0 + f_chunk])
        hid = (gate * _sigmoid(gate) * up).astype(BF16)
        acc += _dot(hid, wd_ref[f0:f0 + f_chunk, :])
    o_ref[...] = _layer_norm(acc, g_ref[...], b_ref[...])


def dense_ffn_ln(x2d, wg, wu, wd, g, b, alpha):
    t, d = x2d.shape
    ff = wg.shape[1]
    tm = min(ROW_TILE, t)
    f_chunk = ff // 2 if (ff // 2) % LANES == 0 else ff
    return pl.pallas_call(
        functools.partial(_ffn_kernel, alpha=alpha, f_chunk=f_chunk),
        grid=(t // tm,),
        in_specs=[pl.BlockSpec((tm, d), lambda i: (i, 0)), _const_spec(wg.shape), _const_spec(wu.shape),
                  _const_spec(wd.shape), _const_spec((1, d)), _const_spec((1, d))],
        out_specs=pl.BlockSpec((tm, d), lambda i: (i, 0)),
        out_shape=jax.ShapeDtypeStruct((t, d), F32),
        compiler_params=_params(("parallel",)),
        name="dense_ffn_ln",
    )(x2d, wg, wu, wd, g.reshape(1, -1), b.reshape(1, -1))


def _router_kernel(x_ref, w_ref, o_ref):
    o_ref[...] = _dot_nt(w_ref[...], x_ref[...], HI)


def router_logits(x2d, router):
    t, d = x2d.shape
    tm = min(ROW_TILE, t)
    wt = router.T
    return pl.pallas_call(
        _router_kernel,
        grid=(t // tm,),
        in_specs=[pl.BlockSpec((tm, d), lambda i: (i, 0)), _const_spec(wt.shape)],
        out_specs=pl.BlockSpec((N_EXPERTS, tm), lambda i: (0, i)),
        out_shape=jax.ShapeDtypeStruct((N_EXPERTS, t), F32),
        compiler_params=_params(("parallel",)),
        name="router_logits",
    )(x2d, wt)


def _expert_kernel(blk_e_ref, used_ref, x_ref, wg_ref, wu_ref, wd_ref, o_ref):
    i, j = pl.program_id(0), pl.program_id(1)

    @pl.when(i < used_ref[0])
    def _():
        xb = x_ref[...]
        gate = _dot(xb, wg_ref[0])
        up = _dot(xb, wu_ref[0])
        hid = (gate * _sigmoid(gate) * up).astype(BF16)
        part = _dot(hid, wd_ref[0])

        @pl.when(j == 0)
        def _():
            o_ref[...] = part

        @pl.when(j > 0)
        def _():
            o_ref[...] += part


def expert_ffn(xs, blk_e, n_used, wg, wu, wd, f_chunk=896):
    rows, d = xs.shape
    ff = wg.shape[2]
    tm = MOE_TILE
    grid_spec = pltpu.PrefetchScalarGridSpec(
        num_scalar_prefetch=2,
        grid=(rows // tm, ff // f_chunk),
        in_specs=[pl.BlockSpec((tm, d), lambda i, j, be, nu: (i, 0)),
                  pl.BlockSpec((1, d, f_chunk), lambda i, j, be, nu: (be[i], 0, j)),
                  pl.BlockSpec((1, d, f_chunk), lambda i, j, be, nu: (be[i], 0, j)),
                  pl.BlockSpec((1, f_chunk, d), lambda i, j, be, nu: (be[i], j, 0))],
        out_specs=pl.BlockSpec((tm, d), lambda i, j, be, nu: (i, 0)),
    )
    return pl.pallas_call(
        _expert_kernel,
        grid_spec=grid_spec,
        out_shape=jax.ShapeDtypeStruct((rows, d), F32),
        compiler_params=_params(("parallel", "arbitrary")),
        name="expert_ffn",
    )(blk_e, n_used, xs, wg, wu, wd)


def _residual_ln_kernel(x_ref, f_ref, g_ref, b_ref, o_ref, *, alpha):
    o_ref[...] = _layer_norm(alpha * x_ref[...] + f_ref[...], g_ref[...], b_ref[...])


def residual_ln(x2d, f2d, g, b, alpha):
    t, d = x2d.shape
    tm = min(ROW_TILE, t)
    rows = pl.BlockSpec((tm, d), lambda i: (i, 0))
    return pl.pallas_call(
        functools.partial(_residual_ln_kernel, alpha=alpha),
        grid=(t // tm,),
        in_specs=[rows, rows, _const_spec((1, d)), _const_spec((1, d))],
        out_specs=rows,
        out_shape=jax.ShapeDtypeStruct((t, d), F32),
        compiler_params=_params(("parallel",)),
        name="residual_ln",
    )(x2d, f2d, g.reshape(1, -1), b.reshape(1, -1))


def moe_ffn_ln(x2d, router, wg, wu, wd, g, b, alpha):
    t, d = x2d.shape
    tm = MOE_TILE
    logits = router_logits(x2d, router).T
    top_val, top_idx = lax.top_k(logits, TOP_K)
    gates = jax.nn.softmax(top_val, axis=-1)
    member = (top_idx[:, :, None] == jnp.arange(N_EXPERTS)[None, None, :]).any(axis=1)
    counts = member.sum(axis=0).astype(jnp.int32)
    rank = jnp.cumsum(member.astype(jnp.int32), axis=0) - member.astype(jnp.int32)
    padded = (counts + tm - 1) // tm * tm
    pad_ends = jnp.cumsum(padded)
    pad_starts = pad_ends - padded
    pos = jnp.take_along_axis(pad_starts[None, :] + rank, top_idx, axis=1)
    rows = t * TOP_K + N_EXPERTS * tm
    n_blk = rows // tm
    tok = jnp.broadcast_to(jnp.arange(t, dtype=jnp.int32)[:, None], (t, TOP_K))
    src = jnp.zeros((rows,), jnp.int32).at[pos.reshape(-1)].set(tok.reshape(-1))
    live = jnp.zeros((rows,), jnp.bool_).at[pos.reshape(-1)].set(True)
    xs = jnp.where(live[:, None], x2d.astype(BF16)[src], 0)
    blk_e = jnp.minimum(jnp.searchsorted(pad_ends, jnp.arange(n_blk, dtype=jnp.int32) * tm, side='right'),
                        N_EXPERTS - 1).astype(jnp.int32)
    n_used = (pad_ends[-1:] // tm).astype(jnp.int32)
    ys = expert_ffn(xs, blk_e, n_used, wg, wu, wd)
    f = (ys[pos.reshape(-1)].reshape(t, TOP_K, d) * gates[:, :, None]).sum(axis=1)
    return residual_ln(x2d, f, g, b, alpha)


def kernel(x, w_in, w_out, rwkv_mu, rwkv_w0, rwkv_w_up, rwkv_a0, rwkv_a_up, rwkv_g_up, rwkv_k_k, rwkv_k_a,
           rwkv_r_k, rwkv_ln_g, rwkv_ln_b, ret_gn_g, ret_gn_b, rel_bias, ln_g, ln_b, ffn_w_gate, ffn_w_up,
           ffn_w_down, moe_router, moe_w_gate, moe_w_up, moe_w_down):
    batch, seq, d = x.shape
    depth = w_in.shape[0]
    alpha = (2 * depth) ** 0.25
    h = x.reshape(batch * seq, d)
    for layer in range(depth):
        p = in_projection(h, w_in[layer].astype(BF16))
        ya = rwkv_time_mix(p, batch, seq, rwkv_mu[layer], rwkv_w0[layer], rwkv_w_up[layer], rwkv_a0[layer],
                           rwkv_a_up[layer], rwkv_g_up[layer], rwkv_k_k[layer], rwkv_k_a[layer],
                           rwkv_r_k[layer], rwkv_ln_g[layer], rwkv_ln_b[layer])
        yb = dilated_attention(p, batch, seq, rel_bias)
        yc = retention(p, batch, seq, ret_gn_g[layer], ret_gn_b[layer])
        h = out_projection_ln(ya, yb, yc, h, w_out[layer].astype(BF16), ln_g[layer, 0], ln_b[layer, 0], alpha)
        j = layer // 2
        if layer % 2 == 0:
            h = dense_ffn_ln(h, ffn_w_gate[j].astype(BF16), ffn_w_up[j].astype(BF16),
                             ffn_w_down[j].astype(BF16), ln_g[layer, 1], ln_b[layer, 1], alpha)
        else:
            h = moe_ffn_ln(h, moe_router[j], moe_w_gate[j].astype(BF16), moe_w_up[j].astype(BF16),
                           moe_w_down[j].astype(BF16), ln_g[layer, 1], ln_b[layer, 1], alpha)
    return h.reshape(batch, seq, d)
```

```python
import functools
import math

import numpy as np
import jax
import jax.numpy as jnp
from jax import lax
from jax.experimental import pallas as pl
from jax.experimental.pallas import tpu as pltpu

F32 = jnp.float32
BF16 = jnp.bfloat16
HI = lax.Precision.HIGHEST

HEAD_DIM = 64
RWKV_HEADS = 4
ATTN_HEADS = 8
RET_HEADS = 4
RWKV_DIM = RWKV_HEADS * HEAD_DIM
ATTN_DIM = ATTN_HEADS * HEAD_DIM
RET_DIM = RET_HEADS * HEAD_DIM
DECAY_LORA = 64
ICL_LORA = 64
GATE_LORA = 128
RWKV_IN = 3 * RWKV_DIM + DECAY_LORA + ICL_LORA + GATE_LORA
ATTN_IN = 3 * ATTN_DIM
RET_IN = 4 * RET_DIM
RWKV_GN_EPS = 64e-5
DECAY_SCALE = math.exp(-0.5)
DILATED_PATTERNS = ((128, 1), (512, 4), (2048, 16))
NUM_BUCKETS = 32
MAX_DISTANCE = 2048
ROPE_BASE = 10000.0
N_EXPERTS = 8
TOP_K = 2
LN_EPS = 1e-5

LANES = 128
WKV_CHUNK = 64
WKV_BLOCK = 256
WKV_GROUP = 2
ATTN_W = 128
ATTN_UNROLL = 4
RET_CHUNK = 128
RET_BLOCK = 1024
RET_UNROLL = 2
ROW_TILE = 512
MOE_TILE = 512
MASK_VALUE = -1e30
VMEM_LIMIT = 56 * 1024 * 1024


def _dot(a, b, prec=None):
    return jnp.dot(a, b, preferred_element_type=F32, precision=prec)


def _dot_nt(a, b, prec=None):
    return lax.dot_general(a, b, (((1,), (1,)), ((), ())), preferred_element_type=F32, precision=prec)


def _dot_tn(a, b, prec=None):
    return lax.dot_general(a, b, (((0,), (0,)), ((), ())), preferred_element_type=F32, precision=prec)


_DIMS = {"nn": (((1,), (0,)), ((), ())), "nt": (((1,), (1,)), ((), ())), "tn": (((0,), (0,)), ((), ()))}


def _split(x, terms):
    parts = []
    for _ in range(terms - 1):
        hi = x.astype(BF16)
        parts.append(hi)
        x = x - hi.astype(F32)
    parts.append(x.astype(BF16))
    return parts


def _mm(a, b, kind="nn", passes=3):
    dg = lambda p, q: lax.dot_general(p, q, _DIMS[kind], preferred_element_type=F32)
    if passes == 1:
        return dg(a.astype(BF16), b.astype(BF16))
    ah, al = _split(a, 2)
    bh, bl = _split(b, 2)
    return dg(ah, bh) + (dg(al, bh) + dg(ah, bl))


def _mm_ones(x, ones_bf16, ones_first=False, terms=3):
    parts = _split(x, terms)
    if ones_first:
        out = [lax.dot_general(ones_bf16, p, _DIMS["nn"], preferred_element_type=F32) for p in parts]
    else:
        out = [lax.dot_general(p, ones_bf16, _DIMS["nn"], preferred_element_type=F32) for p in parts]
    acc = out[-1]
    for o in reversed(out[:-1]):
        acc = acc + o
    return acc


def _sigmoid(x):
    return 1.0 / (1.0 + jnp.exp(-x))


def _layer_norm(z, g, b):
    mu = jnp.mean(z, axis=-1, keepdims=True)
    d = z - mu
    var = jnp.mean(d * d, axis=-1, keepdims=True)
    return d * lax.rsqrt(var + LN_EPS) * g + b


def _params(sem, vmem=VMEM_LIMIT):
    return pltpu.CompilerParams(dimension_semantics=sem, vmem_limit_bytes=vmem)


def _const_spec(shape):
    nd = len(shape)
    return pl.BlockSpec(shape, lambda *_: (0,) * nd)


def _inproj_kernel(x_ref, w_ref, o_ref, *, n_chunk):
    xb = x_ref[...].astype(BF16)
    for n0 in range(0, o_ref.shape[1], n_chunk):
        o_ref[:, n0:n0 + n_chunk] = _dot(xb, w_ref[:, n0:n0 + n_chunk])


def in_projection(x2d, w_bf16):
    t, d = x2d.shape
    n = w_bf16.shape[1]
    tm = min(ROW_TILE, t)
    return pl.pallas_call(
        functools.partial(_inproj_kernel, n_chunk=512),
        grid=(t // tm,),
        in_specs=[pl.BlockSpec((tm, d), lambda i: (i, 0)), _const_spec((d, n))],
        out_specs=pl.BlockSpec((tm, n), lambda i: (i, 0)),
        out_shape=jax.ShapeDtypeStruct((t, n), F32),
        compiler_params=_params(("parallel",)),
        name="in_projection",
    )(x2d, w_bf16)


def _rwkv_kernel(p_ref, mu_ref, w0_ref, wup_ref, a0_ref, aup_ref, gup_ref, kk_ref, ka_ref, rk_ref,
                 lng_ref, lnb_ref, ltri_ref, same_ref, hsum_ref, o_ref,
                 state_s, prev_s, kt_s, rt_s, bt_s, kn_s, v_s, btg_s, kng_s, etot_s, y_s, rp_s, y0_s, gt_s, zt_s):
    c = WKV_CHUNK
    tb = p_ref.shape[0]
    d = RWKV_DIM

    @pl.when(pl.program_id(1) == 0)
    def _():
        state_s[...] = jnp.zeros_like(state_s)
        prev_s[...] = jnp.zeros_like(prev_s)

    p = p_ref[...]
    row = lax.broadcasted_iota(jnp.int32, p.shape, 0)
    shifted = jnp.where(row == 0, prev_s[...], pltpu.roll(p, 1, axis=0))
    prev_s[...] = p[tb - 1:tb, :]
    ps = p + (shifted - p) * mu_ref[...]
    r = ps[:, 0:d]
    k = ps[:, d:2 * d]
    v = ps[:, 2 * d:3 * d]
    xw = ps[:, 3 * d:3 * d + DECAY_LORA]
    xa = ps[:, 3 * d + DECAY_LORA:3 * d + DECAY_LORA + ICL_LORA]
    xg = ps[:, 3 * d + DECAY_LORA + ICL_LORA:]

    hsum = hsum_ref[...]
    logw = -DECAY_SCALE * _sigmoid(w0_ref[...] + _mm(jnp.tanh(xw), wup_ref[...]))
    a = _sigmoid(a0_ref[...] + _mm(xa, aup_ref[...]))
    g = _dot(_sigmoid(xg).astype(BF16), gup_ref[...].astype(BF16))
    kap = k * kk_ref[...]
    kap = kap / jnp.maximum(jnp.sqrt(_mm_ones(kap * kap, hsum)), 1e-12)
    kn = k * (1.0 + (a - 1.0) * ka_ref[...])
    cum = _mm_ones(logw, ltri_ref[...], ones_first=True)
    tot = _mm_ones(logw, same_ref[...], ones_first=True)
    e_neg = jnp.exp(-cum)
    e_rem = jnp.exp(tot - cum)
    nb = -(a * kap)
    kt_s[...] = kap * jnp.exp(cum - logw)
    rt_s[...] = r * jnp.exp(cum)
    bt_s[...] = nb * e_neg
    kn_s[...] = kn * e_neg
    btg_s[...] = nb * e_rem
    kng_s[...] = kn * e_rem
    etot_s[...] = jnp.exp(tot)
    v_s[...] = v

    ri = lax.broadcasted_iota(jnp.int32, (c, c), 0)
    ci = lax.broadcasted_iota(jnp.int32, (c, c), 1)
    strict = ci < ri
    incl = ci <= ri
    eye = (ci == ri).astype(F32)

    nchunk = tb // c
    cat0 = lambda x, y: jnp.concatenate([x, y], axis=0)
    cat1 = lambda x, y: jnp.concatenate([x, y], axis=1)
    levels = int(math.log2(c)) - 1

    for j0 in range(0, nchunk, WKV_GROUP):
        probs = [(j, h) for j in range(j0, min(j0 + WKV_GROUP, nchunk)) for h in range(RWKV_HEADS)]
        tile = lambda ref, j, h: ref[j * c:(j + 1) * c, h * HEAD_DIM:(h + 1) * HEAD_DIM]
        get = lambda ref: [tile(ref, j, h) for j, h in probs]
        kt, rt, vv, btg = get(kt_s), get(rt_s), get(v_s), get(btg_s)
        amat = [_mm(cat0(k_, r_), cat0(b_, n_), "nt") for k_, r_, b_, n_ in zip(kt, rt, get(bt_s), get(kn_s))]
        a_ab = [jnp.where(strict, m[:c, :c], 0.0) for m in amat]
        a_kr = [cat0(jnp.where(strict, m[:c, c:], 0.0), jnp.where(incl, m[c:, c:], 0.0)) for m in amat]
        a_rb = [jnp.where(incl, m[c:, :c], 0.0) for m in amat]
        inv = [eye + m for m in a_ab]
        pw = [_mm(m, m, passes=1) for m in a_ab]
        for lvl in range(levels):
            if lvl < levels - 1:
                both = [_mm(cat0(x_, p_), p_, passes=1) for x_, p_ in zip(inv, pw)]
                inv = [x_ + b_[:c] for x_, b_ in zip(inv, both)]
                pw = [b_[c:] for b_ in both]
            else:
                inv = [x_ + _mm(x_, p_, passes=1) for x_, p_ in zip(inv, pw)]
        av = [_mm(m, v_) for m, v_ in zip(a_kr, vv)]
        wu = [_mm(x_, cat1(k_, a_[:c]), passes=1) for x_, k_, a_ in zip(inv, kt, av)]
        rw = [_mm(m, w_, passes=1) for m, w_ in zip(a_rb, wu)]
        gz = [_mm(b_, w_, "tn", passes=1) for b_, w_ in zip(btg, wu)]
        kv = [_mm(n_, v_, "tn") for n_, v_ in zip(get(kng_s), vv)]
        for i, (j, h) in enumerate(probs):
            rows, sl = slice(j * c, (j + 1) * c), slice(h * HEAD_DIM, (h + 1) * HEAD_DIM)
            rp_s[rows, sl] = rt[i] + rw[i][:, :HEAD_DIM]
            y0_s[rows, sl] = rw[i][:, HEAD_DIM:] + av[i][c:]
            g_diag = jnp.where(ci == ri, jnp.broadcast_to(etot_s[j * c:j * c + 1, sl], (c, c)), 0.0)
            gt_s[rows, sl] = g_diag + gz[i][:, :HEAD_DIM]
            zt_s[rows, sl] = gz[i][:, HEAD_DIM:] + kv[i]

    states = [state_s[h] for h in range(RWKV_HEADS)]
    for j in range(nchunk):
        rows = slice(j * c, (j + 1) * c)
        ry = [_mm(cat0(rp_s[rows, h * HEAD_DIM:(h + 1) * HEAD_DIM], gt_s[rows, h * HEAD_DIM:(h + 1) * HEAD_DIM]),
                  states[h]) for h in range(RWKV_HEADS)]
        for h in range(RWKV_HEADS):
            sl = slice(h * HEAD_DIM, (h + 1) * HEAD_DIM)
            y_s[rows, sl] = ry[h][:c] + y0_s[rows, sl]
            states[h] = ry[h][c:] + zt_s[rows, sl]
    for h in range(RWKV_HEADS):
        state_s[h] = states[h]

    y = y_s[...]
    mean = _mm_ones(y, hsum) * (1.0 / HEAD_DIM)
    dy = y - mean
    var = _mm_ones(dy * dy, hsum) * (1.0 / HEAD_DIM)
    yn = dy * lax.rsqrt(var + RWKV_GN_EPS) * lng_ref[...] + lnb_ref[...]
    bonus = _mm_ones(r * kn * rk_ref[...], hsum) * v
    o_ref[...] = ((yn + bonus) * g).astype(o_ref.dtype)


def _chunk_masks(tb, c):
    i = np.arange(tb)
    same = (i[:, None] // c) == (i[None, :] // c)
    ltri = same & (i[None, :] <= i[:, None])
    return jnp.asarray(ltri, BF16), jnp.asarray(same, BF16)


def _head_sum_matrix(width):
    i = np.arange(width)
    return jnp.asarray((i[:, None] // HEAD_DIM) == (i[None, :] // HEAD_DIM), BF16)


def rwkv_time_mix(p, batch, seq, mu, w0, w_up, a0, a_up, g_up, k_k, k_a, r_k, ln_g, ln_b):
    t = batch * seq
    tb = min(WKV_BLOCK, seq)
    nblk = seq // tb
    ltri, same = _chunk_masks(tb, WKV_CHUNK)
    hsum = _head_sum_matrix(RWKV_DIM)
    row = lambda a: a.reshape(1, -1)
    consts = [row(mu), row(w0), w_up, row(a0), a_up, g_up, row(k_k), row(k_a), row(r_k), row(ln_g), row(ln_b),
              ltri, same, hsum]
    buf = lambda: pltpu.VMEM((tb, RWKV_DIM), F32)
    return pl.pallas_call(
        _rwkv_kernel,
        grid=(batch, nblk),
        in_specs=[pl.BlockSpec((tb, RWKV_IN), lambda b, j: (b * nblk + j, 0))]
                 + [_const_spec(a.shape) for a in consts],
        out_specs=pl.BlockSpec((tb, RWKV_DIM), lambda b, j: (b * nblk + j, 0)),
        out_shape=jax.ShapeDtypeStruct((t, RWKV_DIM), BF16),
        scratch_shapes=[pltpu.VMEM((RWKV_HEADS, HEAD_DIM, HEAD_DIM), F32), pltpu.VMEM((1, RWKV_IN), F32)]
                       + [buf() for _ in range(13)],
        compiler_params=_params(("parallel", "arbitrary")),
        name="rwkv_time_mix",
    )(p, *consts)


def _attn_kernel(q_ref, k_ref, v_ref, bias_ref, o_ref, acc_s, m_s, l_s):
    seq = q_ref.shape[0]
    w = ATTN_W
    scale = HEAD_DIM ** -0.5

    def rows_of(start, dil):
        return pl.ds(start, w) if dil == 1 else pl.ds(start, w, stride=dil)

    lane = lax.broadcasted_iota(jnp.int32, (w, LANES), 1)
    head0 = lane < HEAD_DIM
    zero = jnp.zeros((), BF16)
    one = jnp.ones((), BF16)

    def group(pi, dil, g, firsts):
        rows_l, q_l, k_l, v_l = [], [], [], []
        for u, first in enumerate(firsts):
            b = g * len(firsts) + u
            start = (b % dil) + (b // dil) * (dil * w)
            rows = rows_of(start, dil)
            q = (q_ref[rows, :] * scale).astype(BF16)
            kk = k_ref[rows, :].astype(BF16)
            vv = v_ref[rows, :].astype(BF16)
            if not first:
                prev = rows_of(start - dil * w, dil)
                kk = jnp.concatenate([k_ref[prev, :].astype(BF16), kk], axis=0)
                vv = jnp.concatenate([v_ref[prev, :].astype(BF16), vv], axis=0)
            rows_l.append(rows)
            q_l.append(q)
            k_l.append(kk)
            v_l.append(vv)
        s = [[_dot_nt(jnp.where(head0 if h == 0 else ~head0, q, zero), kk)
              + (bias_ref[pi, h, :, w:] if first else bias_ref[pi, h])
              for h in range(2)] for q, kk, first in zip(q_l, k_l, firsts)]
        m = [[jnp.max(sh, axis=-1, keepdims=True) for sh in su] for su in s]
        pr = [[jnp.exp(sh - mh).astype(BF16) for sh, mh in zip(su, mu)] for su, mu in zip(s, m)]
        kmask = lambda vv: lax.broadcasted_iota(jnp.int32, vv.shape, 1) < HEAD_DIM
        res = [[_dot(pu[0], jnp.where(kmask(vv), vv, one)), _dot(pu[1], jnp.where(kmask(vv), one, vv))]
               for pu, vv in zip(pr, v_l)]
        for rows, ru, mu in zip(rows_l, res, m):
            acc_s[pi, rows, :] = jnp.where(head0, ru[0], ru[1])
            l_s[pi, rows, :] = jnp.where(head0, ru[1], ru[0])
            m_s[pi, rows, :] = jnp.where(head0, mu[0], mu[1])

    n_groups = (seq // w) // ATTN_UNROLL
    for pi, (window, dil) in enumerate(DILATED_PATTERNS):
        flags = [tuple((g * ATTN_UNROLL + u) < dil for u in range(ATTN_UNROLL)) for g in range(n_groups)]
        g0 = 0
        while g0 < n_groups:
            g1 = g0
            while g1 < n_groups and flags[g1] == flags[g0]:
                g1 += 1
            if g1 - g0 == 1:
                group(pi, dil, g0, flags[g0])
            else:
                def body(g, carry, pi=pi, dil=dil, firsts=flags[g0]):
                    group(pi, dil, g, firsts)
                    return carry
                lax.fori_loop(g0, g1, body, 0)
            g0 = g1

    mt = 256

    def merge_body(i, carry):
        rows = pl.ds(pl.multiple_of(i * mt, mt), mt)
        m0, m1, m2 = m_s[0, rows, :], m_s[1, rows, :], m_s[2, rows, :]
        mx = jnp.maximum(jnp.maximum(m0, m1), m2)
        w0, w1, w2 = jnp.exp(m0 - mx), jnp.exp(m1 - mx), jnp.exp(m2 - mx)
        num = w0 * acc_s[0, rows, :] + w1 * acc_s[1, rows, :] + w2 * acc_s[2, rows, :]
        swap = lambda x: pltpu.roll(x, HEAD_DIM, axis=1)
        den = w0 * swap(l_s[0, rows, :]) + w1 * swap(l_s[1, rows, :]) + w2 * swap(l_s[2, rows, :])
        o_ref[rows, :] = (num / den).astype(o_ref.dtype)
        return carry

    lax.fori_loop(0, seq // mt, merge_body, 0)


def _t5_bucket(dist):
    max_exact = NUM_BUCKETS // 2
    large = max_exact + (np.log(np.maximum(dist, max_exact) / max_exact)
                         / math.log(MAX_DISTANCE / max_exact) * (NUM_BUCKETS - max_exact)).astype(np.int32)
    return np.where(dist < max_exact, dist, np.minimum(large, NUM_BUCKETS - 1)).astype(np.int32)


def _attn_bias(rel_bias):
    w = ATTN_W
    i = np.arange(w)[:, None]
    j = np.arange(2 * w)[None, :]
    rel = i + w - j
    band = (rel >= 0) & (rel <= w)
    tabs = []
    for window, dil in DILATED_PATTERNS:
        bucket = _t5_bucket(np.clip(rel, 0, None) * dil)
        bias = jnp.transpose(rel_bias[bucket], (2, 0, 1)).astype(F32)
        tabs.append(jnp.where(band[None], bias, MASK_VALUE))
    return jnp.stack(tabs)


def dilated_attention(p, batch, seq, rel_bias):
    t = batch * seq
    bias = _attn_bias(rel_bias)
    col0 = RWKV_IN // LANES
    npair = ATTN_DIM // LANES
    spec = lambda off: pl.BlockSpec((seq, LANES), lambda b, hp: (b, col0 + off + hp))
    return pl.pallas_call(
        _attn_kernel,
        grid=(batch, npair),
        in_specs=[spec(0), spec(npair), spec(2 * npair),
                  pl.BlockSpec((3, 2, ATTN_W, 2 * ATTN_W), lambda b, hp: (0, hp, 0, 0))],
        out_specs=pl.BlockSpec((seq, LANES), lambda b, hp: (b, hp)),
        out_shape=jax.ShapeDtypeStruct((t, ATTN_DIM), BF16),
        scratch_shapes=[pltpu.VMEM((3, seq, LANES), F32) for _ in range(3)],
        compiler_params=_params(("parallel", "parallel")),
        name="dilated_attention",
    )(p, p, p, bias)


def _ret_kernel(q_ref, k_ref, v_ref, g_ref, cos_ref, sin_ref, dmat_ref, xi_ref, zeta_ref, gng_ref, gnb_ref,
                hsum_ref, o_ref, state_s):
    c = RET_CHUNK
    tb = q_ref.shape[0]

    @pl.when(pl.program_id(1) == 0)
    def _():
        state_s[...] = jnp.zeros_like(state_s)

    lane = lax.broadcasted_iota(jnp.int32, (c, RET_DIM), 1)
    first_half = (lane % HEAD_DIM) < (HEAD_DIM // 2)

    def rotate(x, cos, sin):
        swapped = jnp.where(first_half, pltpu.roll(x, RET_DIM - HEAD_DIM // 2, axis=1),
                            pltpu.roll(x, HEAD_DIM // 2, axis=1))
        return x * cos + swapped * sin

    heads = range(RET_HEADS)
    hsl = [slice(h * HEAD_DIM, (h + 1) * HEAD_DIM) for h in heads]
    chunk_decay = [(1.0 - 2.0 ** (-5.0 - h)) ** c for h in heads]

    def group_body(g, carry):
        rows_l, qb, kb, qx, kz, vb = [], [], [], [], [], []
        for u in range(RET_UNROLL):
            rows = pl.ds(pl.multiple_of((g * RET_UNROLL + u) * c, c), c)
            cos, sin = cos_ref[rows, :], sin_ref[rows, :]
            q = rotate(q_ref[rows, :], cos, sin)
            k = rotate(k_ref[rows, :], cos, sin) * (HEAD_DIM ** -0.5)
            rows_l.append(rows)
            qb.append(q.astype(BF16))
            kb.append(k.astype(BF16))
            qx.append((q * xi_ref[...]).astype(BF16))
            kz.append((k * zeta_ref[...]).astype(BF16))
            vb.append(v_ref[rows, :].astype(BF16))
        sc = [[(_dot_nt(qb[u][:, sl], kb[u][:, sl]) * dmat_ref[h]).astype(BF16) for h, sl in zip(heads, hsl)]
              for u in range(RET_UNROLL)]
        intra = [[_dot(sc[u][h], vb[u][:, hsl[h]]) for h in heads] for u in range(RET_UNROLL)]
        kv = [[_dot_tn(kz[u][:, sl], vb[u][:, sl]) for sl in hsl] for u in range(RET_UNROLL)]
        states = [state_s[h] for h in heads]
        ys = []
        for u in range(RET_UNROLL):
            ys.append(jnp.concatenate(
                [intra[u][h] + _dot(qx[u][:, hsl[h]], states[h].astype(BF16)) for h in heads], axis=1))
            states = [states[h] * chunk_decay[h] + kv[u][h] for h in heads]
        for h in heads:
            state_s[h] = states[h]
        hsum = hsum_ref[...]
        for rows, y in zip(rows_l, ys):
            mean = _mm_ones(y, hsum) * (1.0 / HEAD_DIM)
            dy = y - mean
            var = _mm_ones(dy * dy, hsum) * (1.0 / HEAD_DIM)
            yn = dy * lax.rsqrt(var + LN_EPS) * gng_ref[...] + gnb_ref[...]
            gate = g_ref[rows, :]
            o_ref[rows, :] = (gate * _sigmoid(gate) * yn).astype(o_ref.dtype)
        return carry

    lax.fori_loop(0, tb // (c * RET_UNROLL), group_body, 0)


def _ret_tables(seq):
    c = RET_CHUNK
    half = HEAD_DIM // 2
    inv = ROPE_BASE ** (-jnp.arange(half, dtype=F32) / half)
    ang = jnp.arange(seq, dtype=F32)[:, None] * inv
    cos, sin = jnp.cos(ang), jnp.sin(ang)
    cos_t = jnp.tile(jnp.concatenate([cos, cos], axis=1), (1, RET_HEADS))
    sin_t = jnp.tile(jnp.concatenate([-sin, sin], axis=1), (1, RET_HEADS))
    log_g = jnp.log1p(-jnp.exp2(-5.0 - jnp.arange(RET_HEADS, dtype=F32)))
    n = jnp.arange(c, dtype=F32)
    diff = n[:, None] - n[None, :]
    dmat = jnp.where(diff >= 0, jnp.exp(log_g[:, None, None] * jnp.maximum(diff, 0.0)), 0.0)
    zeta = jnp.exp(log_g[:, None] * (c - 1 - n))
    xi = jnp.exp(log_g[:, None] * (n + 1))
    widen = lambda tab: jnp.repeat(tab.T, HEAD_DIM, axis=1)
    return cos_t, sin_t, dmat, widen(xi), widen(zeta)


def retention(p, batch, seq, gn_g, gn_b):
    t = batch * seq
    tb = min(RET_BLOCK, seq)
    nblk = seq // tb
    cos_t, sin_t, dmat, xi, zeta = _ret_tables(seq)
    hsum = _head_sum_matrix(RET_DIM)
    col0 = (RWKV_IN + ATTN_IN) // RET_DIM
    spec = lambda off: pl.BlockSpec((tb, RET_DIM), lambda b, j: (b * nblk + j, col0 + off))
    tab = pl.BlockSpec((tb, RET_DIM), lambda b, j: (j, 0))
    consts = [dmat, xi, zeta, gn_g.reshape(1, -1), gn_b.reshape(1, -1), hsum]
    return pl.pallas_call(
        _ret_kernel,
        grid=(batch, nblk),
        in_specs=[spec(0), spec(1), spec(2), spec(3), tab, tab] + [_const_spec(a.shape) for a in consts],
        out_specs=pl.BlockSpec((tb, RET_DIM), lambda b, j: (b * nblk + j, 0)),
        out_shape=jax.ShapeDtypeStruct((t, RET_DIM), BF16),
        scratch_shapes=[pltpu.VMEM((RET_HEADS, HEAD_DIM, HEAD_DIM), F32)],
        compiler_params=_params(("parallel", "arbitrary")),
        name="retention",
    )(p, p, p, p, cos_t, sin_t, *consts)


def _outproj_kernel(ya_ref, yb_ref, yc_ref, x_ref, w_ref, g_ref, b_ref, o_ref, *, alpha):
    acc = _dot(ya_ref[...], w_ref[0:RWKV_DIM, :])
    acc += _dot(yb_ref[...], w_ref[RWKV_DIM:RWKV_DIM + ATTN_DIM, :])
    acc += _dot(yc_ref[...], w_ref[RWKV_DIM + ATTN_DIM:, :])
    o_ref[...] = _layer_norm(alpha * x_ref[...] + acc, g_ref[...], b_ref[...])


def out_projection_ln(ya, yb, yc, x2d, w_bf16, g, b, alpha):
    t, d = x2d.shape
    tm = min(ROW_TILE, t)
    rows = lambda width: pl.BlockSpec((tm, width), lambda i: (i, 0))
    return pl.pallas_call(
        functools.partial(_outproj_kernel, alpha=alpha),
        grid=(t // tm,),
        in_specs=[rows(RWKV_DIM), rows(ATTN_DIM), rows(RET_DIM), rows(d), _const_spec(w_bf16.shape),
                  _const_spec((1, d)), _const_spec((1, d))],
        out_specs=rows(d),
        out_shape=jax.ShapeDtypeStruct((t, d), F32),
        compiler_params=_params(("parallel",)),
        name="out_projection_ln",
    )(ya, yb, yc, x2d, w_bf16, g.reshape(1, -1), b.reshape(1, -1))


def _ffn_kernel(x_ref, wg_ref, wu_ref, wd_ref, g_ref, b_ref, o_ref, *, alpha, f_chunk):
    x = x_ref[...]
    xb = x.astype(BF16)
    acc = alpha * x
    for f0 in range(0, wg_ref.shape[1], f_chunk):
        gate = _dot(xb, wg_ref[:, f0:f0 + f_chunk])
        up = _dot(xb, wu_ref[:, f0:f0 + f_chunk])
        hid = (gate * _sigmoid(gate) * up).astype(BF16)
        acc += _dot(hid, wd_ref[f0:f0 + f_chunk, :])
    o_ref[...] = _layer_norm(acc, g_ref[...], b_ref[...])


def dense_ffn_ln(x2d, wg, wu, wd, g, b, alpha):
    t, d = x2d.shape
    ff = wg.shape[1]
    tm = min(ROW_TILE, t)
    f_chunk = ff // 2 if (ff // 2) % LANES == 0 else ff
    return pl.pallas_call(
        functools.partial(_ffn_kernel, alpha=alpha, f_chunk=f_chunk),
        grid=(t // tm,),
        in_specs=[pl.BlockSpec((tm, d), lambda i: (i, 0)), _const_spec(wg.shape), _const_spec(wu.shape),
                  _const_spec(wd.shape), _const_spec((1, d)), _const_spec((1, d))],
        out_specs=pl.BlockSpec((tm, d), lambda i: (i, 0)),
        out_shape=jax.ShapeDtypeStruct((t, d), F32),
        compiler_params=_params(("parallel",)),
        name="dense_ffn_ln",
    )(x2d, wg, wu, wd, g.reshape(1, -1), b.reshape(1, -1))


def _router_kernel(x_ref, w_ref, o_ref):
    o_ref[...] = _dot_nt(w_ref[...].astype(BF16), x_ref[...].astype(BF16))


def router_logits(x2d, router):
    t, d = x2d.shape
    tm = min(ROW_TILE, t)
    wt = router.T
    return pl.pallas_call(
        _router_kernel,
        grid=(t // tm,),
        in_specs=[pl.BlockSpec((tm, d), lambda i: (i, 0)), _const_spec(wt.shape)],
        out_specs=pl.BlockSpec((N_EXPERTS, tm), lambda i: (0, i)),
        out_shape=jax.ShapeDtypeStruct((N_EXPERTS, t), F32),
        compiler_params=_params(("parallel",)),
        name="router_logits",
    )(x2d, wt)


def _expert_kernel(blk_e_ref, used_ref, x_ref, wg_ref, wu_ref, wd_ref, o_ref):
    i, j = pl.program_id(0), pl.program_id(1)

    @pl.when(i < used_ref[0])
    def _():
        xb = x_ref[...]
        gate = _dot(xb, wg_ref[0])
        up = _dot(xb, wu_ref[0])
        hid = (gate * _sigmoid(gate) * up).astype(BF16)
        part = _dot(hid, wd_ref[0])

        @pl.when(j == 0)
        def _():
            o_ref[...] = part

        @pl.when(j > 0)
        def _():
            o_ref[...] += part


def expert_ffn(xs, blk_e, n_used, wg, wu, wd, f_chunk=896):
    rows, d = xs.shape
    ff = wg.shape[2]
    tm = MOE_TILE
    grid_spec = pltpu.PrefetchScalarGridSpec(
        num_scalar_prefetch=2,
        grid=(rows // tm, ff // f_chunk),
        in_specs=[pl.BlockSpec((tm, d), lambda i, j, be, nu: (i, 0)),
                  pl.BlockSpec((1, d, f_chunk), lambda i, j, be, nu: (be[i], 0, j)),
                  pl.BlockSpec((1, d, f_chunk), lambda i, j, be, nu: (be[i], 0, j)),
                  pl.BlockSpec((1, f_chunk, d), lambda i, j, be, nu: (be[i], j, 0))],
        out_specs=pl.BlockSpec((tm, d), lambda i, j, be, nu: (i, 0)),
    )
    return pl.pallas_call(
        _expert_kernel,
        grid_spec=grid_spec,
        out_shape=jax.ShapeDtypeStruct((rows, d), F32),
        compiler_params=_params(("parallel", "arbitrary")),
        name="expert_ffn",
    )(blk_e, n_used, xs, wg, wu, wd)


def _residual_ln_kernel(x_ref, f_ref, g_ref, b_ref, o_ref, *, alpha):
    o_ref[...] = _layer_norm(alpha * x_ref[...] + f_ref[...], g_ref[...], b_ref[...])


def residual_ln(x2d, f2d, g, b, alpha):
    t, d = x2d.shape
    tm = min(ROW_TILE, t)
    rows = pl.BlockSpec((tm, d), lambda i: (i, 0))
    return pl.pallas_call(
        functools.partial(_residual_ln_kernel, alpha=alpha),
        grid=(t // tm,),
        in_specs=[rows, rows, _const_spec((1, d)), _const_spec((1, d))],
        out_specs=rows,
        out_shape=jax.ShapeDtypeStruct((t, d), F32),
        compiler_params=_params(("parallel",)),
        name="residual_ln",
    )(x2d, f2d, g.reshape(1, -1), b.reshape(1, -1))


def moe_ffn_ln(x2d, router, wg, wu, wd, g, b, alpha):
    t, d = x2d.shape
    tm = MOE_TILE
    logits = router_logits(x2d, router).T
    top_val, top_idx = lax.top_k(logits, TOP_K)
    gates = jax.nn.softmax(top_val, axis=-1)
    member = (top_idx[:, :, None] == jnp.arange(N_EXPERTS)[None, None, :]).any(axis=1)
    counts = member.sum(axis=0).astype(jnp.int32)
    rank = jnp.cumsum(member.astype(jnp.int32), axis=0) - member.astype(jnp.int32)
    padded = (counts + tm - 1) // tm * tm
    pad_ends = jnp.cumsum(padded)
    pad_starts = pad_ends - padded
    pos = jnp.take_along_axis(pad_starts[None, :] + rank, top_idx, axis=1)
    rows = t * TOP_K + N_EXPERTS * tm
    n_blk = rows // tm
    order = jnp.argsort(top_idx.reshape(-1), stable=True).astype(jnp.int32)
    tok_sorted = order // TOP_K
    starts = jnp.cumsum(counts) - counts
    r = jnp.arange(rows, dtype=jnp.int32)
    row_e = jnp.minimum(jnp.searchsorted(pad_ends, r, side='right'), N_EXPERTS - 1).astype(jnp.int32)
    within = r - pad_starts[row_e]
    src = jnp.where(within < counts[row_e], tok_sorted[jnp.minimum(starts[row_e] + within, t * TOP_K - 1)], 0)
    xs = x2d.astype(BF16)[src]
    blk_e = row_e[::tm]
    n_used = (pad_ends[-1:] // tm).astype(jnp.int32)
    ys = expert_ffn(xs, blk_e, n_used, wg, wu, wd)
    f = ys[pos[:, 0]] * gates[:, 0:1] + ys[pos[:, 1]] * gates[:, 1:2]
    return residual_ln(x2d, f, g, b, alpha)


def kernel(x, w_in, w_out, rwkv_mu, rwkv_w0, rwkv_w_up, rwkv_a0, rwkv_a_up, rwkv_g_up, rwkv_k_k, rwkv_k_a,
           rwkv_r_k, rwkv_ln_g, rwkv_ln_b, ret_gn_g, ret_gn_b, rel_bias, ln_g, ln_b, ffn_w_gate, ffn_w_up,
           ffn_w_down, moe_router, moe_w_gate, moe_w_up, moe_w_down):
    batch, seq, d = x.shape
    depth = w_in.shape[0]
    alpha = (2 * depth) ** 0.25
    h = x.reshape(batch * seq, d)
    for layer in range(depth):
        p = in_projection(h, w_in[layer].astype(BF16))
        ya = rwkv_time_mix(p, batch, seq, rwkv_mu[layer], rwkv_w0[layer], rwkv_w_up[layer], rwkv_a0[layer],
                           rwkv_a_up[layer], rwkv_g_up[layer], rwkv_k_k[layer], rwkv_k_a[layer],
                           rwkv_r_k[layer], rwkv_ln_g[layer], rwkv_ln_b[layer])
        yb = dilated_attention(p, batch, seq, rel_bias)
        yc = retention(p, batch, seq, ret_gn_g[layer], ret_gn_b[layer])
        h = out_projection_ln(ya, yb, yc, h, w_out[layer].astype(BF16), ln_g[layer, 0], ln_b[layer, 0], alpha)
        j = layer // 2
        if layer % 2 == 0:
            h = dense_ffn_ln(h, ffn_w_gate[j].astype(BF16), ffn_w_up[j].astype(BF16),
                             ffn_w_down[j].astype(BF16), ln_g[layer, 1], ln_b[layer, 1], alpha)
        else:
            h = moe_ffn_ln(h, moe_router[j], moe_w_gate[j].astype(BF16), moe_w_up[j].astype(BF16),
                           moe_w_down[j].astype(BF16), ln_g[layer, 1], ln_b[layer, 1], alpha)
    return h.reshape(batch, seq, d)
```

```python
import functools
import math

import numpy as np
import jax
import jax.numpy as jnp
from jax import lax
from jax.experimental import pallas as pl
from jax.experimental.pallas import tpu as pltpu

F32 = jnp.float32
BF16 = jnp.bfloat16
HI = lax.Precision.HIGHEST

HEAD_DIM = 64
RWKV_HEADS = 4
ATTN_HEADS = 8
RET_HEADS = 4
RWKV_DIM = RWKV_HEADS * HEAD_DIM
ATTN_DIM = ATTN_HEADS * HEAD_DIM
RET_DIM = RET_HEADS * HEAD_DIM
DECAY_LORA = 64
ICL_LORA = 64
GATE_LORA = 128
RWKV_IN = 3 * RWKV_DIM + DECAY_LORA + ICL_LORA + GATE_LORA
ATTN_IN = 3 * ATTN_DIM
RET_IN = 4 * RET_DIM
RWKV_GN_EPS = 64e-5
DECAY_SCALE = math.exp(-0.5)
DILATED_PATTERNS = ((128, 1), (512, 4), (2048, 16))
NUM_BUCKETS = 32
MAX_DISTANCE = 2048
ROPE_BASE = 10000.0
N_EXPERTS = 8
TOP_K = 2
LN_EPS = 1e-5

LANES = 128
WKV_CHUNK = 64
WKV_BLOCK = 256
WKV_GROUP = 2
ATTN_W = 128
ATTN_UNROLL = 4
RET_CHUNK = 128
RET_BLOCK = 1024
RET_UNROLL = 2
ROW_TILE = 512
MOE_TILE = 512
MOE_F_BLOCK = 1792
FFN_SUBCHUNK = 256
MASK_VALUE = -1e30
VMEM_LIMIT = 56 * 1024 * 1024


def _dot(a, b, prec=None):
    return jnp.dot(a, b, preferred_element_type=F32, precision=prec)


def _dot_nt(a, b, prec=None):
    return lax.dot_general(a, b, (((1,), (1,)), ((), ())), preferred_element_type=F32, precision=prec)


def _dot_tn(a, b, prec=None):
    return lax.dot_general(a, b, (((0,), (0,)), ((), ())), preferred_element_type=F32, precision=prec)


_DIMS = {"nn": (((1,), (0,)), ((), ())), "nt": (((1,), (1,)), ((), ())), "tn": (((0,), (0,)), ((), ()))}


def _split(x, terms):
    parts = []
    for _ in range(terms - 1):
        hi = x.astype(BF16)
        parts.append(hi)
        x = x - hi.astype(F32)
    parts.append(x.astype(BF16))
    return parts


def _mm(a, b, kind="nn", passes=3):
    dg = lambda p, q: lax.dot_general(p, q, _DIMS[kind], preferred_element_type=F32)
    if passes == 1:
        return dg(a.astype(BF16), b.astype(BF16))
    ah, al = _split(a, 2)
    bh, bl = _split(b, 2)
    return dg(ah, bh) + (dg(al, bh) + dg(ah, bl))


def _mm_ones(x, ones_bf16, ones_first=False, terms=3):
    parts = _split(x, terms)
    if ones_first:
        out = [lax.dot_general(ones_bf16, p, _DIMS["nn"], preferred_element_type=F32) for p in parts]
    else:
        out = [lax.dot_general(p, ones_bf16, _DIMS["nn"], preferred_element_type=F32) for p in parts]
    acc = out[-1]
    for o in reversed(out[:-1]):
        acc = acc + o
    return acc


def _sigmoid(x):
    return 1.0 / (1.0 + jnp.exp(-x))


def _layer_norm(z, g, b):
    mu = jnp.mean(z, axis=-1, keepdims=True)
    d = z - mu
    var = jnp.mean(d * d, axis=-1, keepdims=True)
    return d * lax.rsqrt(var + LN_EPS) * g + b


def _params(sem, vmem=VMEM_LIMIT):
    return pltpu.CompilerParams(dimension_semantics=sem, vmem_limit_bytes=vmem)


def _const_spec(shape):
    nd = len(shape)
    return pl.BlockSpec(shape, lambda *_: (0,) * nd)


def _layer_spec(stacked, layer):
    nd = stacked.ndim - 1
    return pl.BlockSpec((None,) + stacked.shape[1:], lambda *_: (layer,) + (0,) * nd)


def _inproj_kernel(x_ref, w_ref, o_ref, *, n_chunk):
    xb = x_ref[...].astype(BF16)
    for n0 in range(0, o_ref.shape[1], n_chunk):
        o_ref[:, n0:n0 + n_chunk] = _dot(xb, w_ref[:, n0:n0 + n_chunk])


def in_projection(x2d, w_bf16, layer):
    t, d = x2d.shape
    n = w_bf16.shape[2]
    tm = min(ROW_TILE, t)
    return pl.pallas_call(
        functools.partial(_inproj_kernel, n_chunk=512),
        grid=(t // tm,),
        in_specs=[pl.BlockSpec((tm, d), lambda i: (i, 0)), _layer_spec(w_bf16, layer)],
        out_specs=pl.BlockSpec((tm, n), lambda i: (i, 0)),
        out_shape=jax.ShapeDtypeStruct((t, n), F32),
        compiler_params=_params(("parallel",)),
        name="in_projection",
    )(x2d, w_bf16)


def _rwkv_kernel(p_ref, mu_ref, w0_ref, wup_ref, a0_ref, aup_ref, gup_ref, kk_ref, ka_ref, rk_ref,
                 lng_ref, lnb_ref, ltri_ref, same_ref, hsum_ref, o_ref,
                 state_s, prev_s, kt_s, rt_s, bt_s, kn_s, v_s, btg_s, kng_s, etot_s, y_s, rp_s, y0_s, gt_s, zt_s):
    c = WKV_CHUNK
    tb = p_ref.shape[0]
    d = RWKV_DIM

    @pl.when(pl.program_id(1) == 0)
    def _():
        state_s[...] = jnp.zeros_like(state_s)
        prev_s[...] = jnp.zeros_like(prev_s)

    p = p_ref[...]
    row = lax.broadcasted_iota(jnp.int32, p.shape, 0)
    shifted = jnp.where(row == 0, prev_s[...], pltpu.roll(p, 1, axis=0))
    prev_s[...] = p[tb - 1:tb, :]
    ps = p + (shifted - p) * mu_ref[...]
    r = ps[:, 0:d]
    k = ps[:, d:2 * d]
    v = ps[:, 2 * d:3 * d]
    xw = ps[:, 3 * d:3 * d + DECAY_LORA]
    xa = ps[:, 3 * d + DECAY_LORA:3 * d + DECAY_LORA + ICL_LORA]
    xg = ps[:, 3 * d + DECAY_LORA + ICL_LORA:]

    hsum = hsum_ref[...]
    logw = -DECAY_SCALE * _sigmoid(w0_ref[...] + _mm(jnp.tanh(xw), wup_ref[...]))
    a = _sigmoid(a0_ref[...] + _mm(xa, aup_ref[...]))
    g = _dot(_sigmoid(xg).astype(BF16), gup_ref[...].astype(BF16))
    kap = k * kk_ref[...]
    kap = kap / jnp.maximum(jnp.sqrt(_mm_ones(kap * kap, hsum)), 1e-12)
    kn = k * (1.0 + (a - 1.0) * ka_ref[...])
    cum = _mm_ones(logw, ltri_ref[...], ones_first=True)
    tot = _mm_ones(logw, same_ref[...], ones_first=True)
    e_neg = jnp.exp(-cum)
    e_rem = jnp.exp(tot - cum)
    nb = -(a * kap)
    kt_s[...] = kap * jnp.exp(cum - logw)
    rt_s[...] = r * jnp.exp(cum)
    bt_s[...] = nb * e_neg
    kn_s[...] = kn * e_neg
    btg_s[...] = nb * e_rem
    kng_s[...] = kn * e_rem
    etot_s[...] = jnp.exp(tot)
    v_s[...] = v

    ri = lax.broadcasted_iota(jnp.int32, (c, c), 0)
    ci = lax.broadcasted_iota(jnp.int32, (c, c), 1)
    strict = ci < ri
    incl = ci <= ri
    eye = (ci == ri).astype(F32)

    nchunk = tb // c
    cat0 = lambda x, y: jnp.concatenate([x, y], axis=0)
    cat1 = lambda x, y: jnp.concatenate([x, y], axis=1)
    levels = int(math.log2(c)) - 1

    for j0 in range(0, nchunk, WKV_GROUP):
        probs = [(j, h) for j in range(j0, min(j0 + WKV_GROUP, nchunk)) for h in range(RWKV_HEADS)]
        tile = lambda ref, j, h: ref[j * c:(j + 1) * c, h * HEAD_DIM:(h + 1) * HEAD_DIM]
        get = lambda ref: [tile(ref, j, h) for j, h in probs]
        kt, rt, vv, btg = get(kt_s), get(rt_s), get(v_s), get(btg_s)
        amat = [_mm(cat0(k_, r_), cat0(b_, n_), "nt") for k_, r_, b_, n_ in zip(kt, rt, get(bt_s), get(kn_s))]
        a_ab = [jnp.where(strict, m[:c, :c], 0.0) for m in amat]
        a_kr = [cat0(jnp.where(strict, m[:c, c:], 0.0), jnp.where(incl, m[c:, c:], 0.0)) for m in amat]
        a_rb = [jnp.where(incl, m[c:, :c], 0.0) for m in amat]
        inv = [eye + m for m in a_ab]
        pw = [_mm(m, m, passes=1) for m in a_ab]
        for lvl in range(levels):
            if lvl < levels - 1:
                both = [_mm(cat0(x_, p_), p_, passes=1) for x_, p_ in zip(inv, pw)]
                inv = [x_ + b_[:c] for x_, b_ in zip(inv, both)]
                pw = [b_[c:] for b_ in both]
            else:
                inv = [x_ + _mm(x_, p_, passes=1) for x_, p_ in zip(inv, pw)]
        av = [_mm(m, v_) for m, v_ in zip(a_kr, vv)]
        wu = [_mm(x_, cat1(k_, a_[:c]), passes=1) for x_, k_, a_ in zip(inv, kt, av)]
        rw = [_mm(m, w_, passes=1) for m, w_ in zip(a_rb, wu)]
        gz = [_mm(b_, w_, "tn", passes=1) for b_, w_ in zip(btg, wu)]
        kv = [_mm(n_, v_, "tn") for n_, v_ in zip(get(kng_s), vv)]
        for i, (j, h) in enumerate(probs):
            rows, sl = slice(j * c, (j + 1) * c), slice(h * HEAD_DIM, (h + 1) * HEAD_DIM)
            rp_s[rows, sl] = rt[i] + rw[i][:, :HEAD_DIM]
            y0_s[rows, sl] = rw[i][:, HEAD_DIM:] + av[i][c:]
            g_diag = jnp.where(ci == ri, jnp.broadcast_to(etot_s[j * c:j * c + 1, sl], (c, c)), 0.0)
            gt_s[rows, sl] = g_diag + gz[i][:, :HEAD_DIM]
            zt_s[rows, sl] = gz[i][:, HEAD_DIM:] + kv[i]

    states = [state_s[h] for h in range(RWKV_HEADS)]
    for j in range(nchunk):
        rows = slice(j * c, (j + 1) * c)
        ry = [_mm(cat0(rp_s[rows, h * HEAD_DIM:(h + 1) * HEAD_DIM], gt_s[rows, h * HEAD_DIM:(h + 1) * HEAD_DIM]),
                  states[h]) for h in range(RWKV_HEADS)]
        for h in range(RWKV_HEADS):
            sl = slice(h * HEAD_DIM, (h + 1) * HEAD_DIM)
            y_s[rows, sl] = ry[h][:c] + y0_s[rows, sl]
            states[h] = ry[h][c:] + zt_s[rows, sl]
    for h in range(RWKV_HEADS):
        state_s[h] = states[h]

    y = y_s[...]
    mean = _mm_ones(y, hsum) * (1.0 / HEAD_DIM)
    dy = y - mean
    var = _mm_ones(dy * dy, hsum) * (1.0 / HEAD_DIM)
    yn = dy * lax.rsqrt(var + RWKV_GN_EPS) * lng_ref[...] + lnb_ref[...]
    bonus = _mm_ones(r * kn * rk_ref[...], hsum) * v
    o_ref[...] = ((yn + bonus) * g).astype(o_ref.dtype)


def _chunk_masks(tb, c):
    i = np.arange(tb)
    same = (i[:, None] // c) == (i[None, :] // c)
    ltri = same & (i[None, :] <= i[:, None])
    return jnp.asarray(ltri, BF16), jnp.asarray(same, BF16)


def _head_sum_matrix(width):
    i = np.arange(width)
    return jnp.asarray((i[:, None] // HEAD_DIM) == (i[None, :] // HEAD_DIM), BF16)


def rwkv_time_mix(p, batch, seq, mu, w0, w_up, a0, a_up, g_up, k_k, k_a, r_k, ln_g, ln_b):
    t = batch * seq
    tb = min(WKV_BLOCK, seq)
    nblk = seq // tb
    ltri, same = _chunk_masks(tb, WKV_CHUNK)
    hsum = _head_sum_matrix(RWKV_DIM)
    row = lambda a: a.reshape(1, -1)
    consts = [row(mu), row(w0), w_up, row(a0), a_up, g_up, row(k_k), row(k_a), row(r_k), row(ln_g), row(ln_b),
              ltri, same, hsum]
    buf = lambda: pltpu.VMEM((tb, RWKV_DIM), F32)
    return pl.pallas_call(
        _rwkv_kernel,
        grid=(batch, nblk),
        in_specs=[pl.BlockSpec((tb, RWKV_IN), lambda b, j: (b * nblk + j, 0))]
                 + [_const_spec(a.shape) for a in consts],
        out_specs=pl.BlockSpec((tb, RWKV_DIM), lambda b, j: (b * nblk + j, 0)),
        out_shape=jax.ShapeDtypeStruct((t, RWKV_DIM), BF16),
        scratch_shapes=[pltpu.VMEM((RWKV_HEADS, HEAD_DIM, HEAD_DIM), F32), pltpu.VMEM((1, RWKV_IN), F32)]
                       + [buf() for _ in range(13)],
        compiler_params=_params(("parallel", "arbitrary")),
        name="rwkv_time_mix",
    )(p, *consts)


def _attn_kernel(q_ref, k_ref, v_ref, bias_ref, o_ref, acc_s, m_s, l_s):
    seq = q_ref.shape[0]
    w = ATTN_W
    scale = HEAD_DIM ** -0.5

    def rows_of(start, dil):
        return pl.ds(start, w) if dil == 1 else pl.ds(start, w, stride=dil)

    lane = lax.broadcasted_iota(jnp.int32, (w, LANES), 1)
    head0 = lane < HEAD_DIM
    zero = jnp.zeros((), BF16)
    one = jnp.ones((), BF16)

    def group(pi, dil, g, firsts):
        rows_l, q_l, k_l, v_l = [], [], [], []
        for u, first in enumerate(firsts):
            b = g * len(firsts) + u
            start = (b % dil) + (b // dil) * (dil * w)
            rows = rows_of(start, dil)
            q = (q_ref[rows, :] * scale).astype(BF16)
            kk = k_ref[rows, :].astype(BF16)
            vv = v_ref[rows, :].astype(BF16)
            if not first:
                prev = rows_of(start - dil * w, dil)
                kk = jnp.concatenate([k_ref[prev, :].astype(BF16), kk], axis=0)
                vv = jnp.concatenate([v_ref[prev, :].astype(BF16), vv], axis=0)
            rows_l.append(rows)
            q_l.append(q)
            k_l.append(kk)
            v_l.append(vv)
        s = [[_dot_nt(jnp.where(head0 if h == 0 else ~head0, q, zero), kk)
              + (bias_ref[pi, h, :, w:] if first else bias_ref[pi, h])
              for h in range(2)] for q, kk, first in zip(q_l, k_l, firsts)]
        m = [[jnp.max(sh, axis=-1, keepdims=True) for sh in su] for su in s]
        pr = [[jnp.exp(sh - mh).astype(BF16) for sh, mh in zip(su, mu)] for su, mu in zip(s, m)]
        kmask = lambda vv: lax.broadcasted_iota(jnp.int32, vv.shape, 1) < HEAD_DIM
        res = [[_dot(pu[0], jnp.where(kmask(vv), vv, one)), _dot(pu[1], jnp.where(kmask(vv), one, vv))]
               for pu, vv in zip(pr, v_l)]
        for rows, ru, mu in zip(rows_l, res, m):
            acc_s[pi, rows, :] = jnp.where(head0, ru[0], ru[1])
            l_s[pi, rows, :] = jnp.where(head0, ru[1], ru[0])
            m_s[pi, rows, :] = jnp.where(head0, mu[0], mu[1])

    n_groups = (seq // w) // ATTN_UNROLL
    for pi, (window, dil) in enumerate(DILATED_PATTERNS):
        flags = [tuple((g * ATTN_UNROLL + u) < dil for u in range(ATTN_UNROLL)) for g in range(n_groups)]
        g0 = 0
        while g0 < n_groups:
            g1 = g0
            while g1 < n_groups and flags[g1] == flags[g0]:
                g1 += 1
            if g1 - g0 == 1:
                group(pi, dil, g0, flags[g0])
            else:
                def body(g, carry, pi=pi, dil=dil, firsts=flags[g0]):
                    group(pi, dil, g, firsts)
                    return carry
                lax.fori_loop(g0, g1, body, 0)
            g0 = g1

    mt = 256

    def merge_body(i, carry):
        rows = pl.ds(pl.multiple_of(i * mt, mt), mt)
        m0, m1, m2 = m_s[0, rows, :], m_s[1, rows, :], m_s[2, rows, :]
        mx = jnp.maximum(jnp.maximum(m0, m1), m2)
        w0, w1, w2 = jnp.exp(m0 - mx), jnp.exp(m1 - mx), jnp.exp(m2 - mx)
        num = w0 * acc_s[0, rows, :] + w1 * acc_s[1, rows, :] + w2 * acc_s[2, rows, :]
        swap = lambda x: pltpu.roll(x, HEAD_DIM, axis=1)
        den = w0 * swap(l_s[0, rows, :]) + w1 * swap(l_s[1, rows, :]) + w2 * swap(l_s[2, rows, :])
        o_ref[rows, :] = (num / den).astype(o_ref.dtype)
        return carry

    lax.fori_loop(0, seq // mt, merge_body, 0)


def _t5_bucket(dist):
    max_exact = NUM_BUCKETS // 2
    large = max_exact + (np.log(np.maximum(dist, max_exact) / max_exact)
                         / math.log(MAX_DISTANCE / max_exact) * (NUM_BUCKETS - max_exact)).astype(np.int32)
    return np.where(dist < max_exact, dist, np.minimum(large, NUM_BUCKETS - 1)).astype(np.int32)


def _attn_bias(rel_bias):
    w = ATTN_W
    i = np.arange(w)[:, None]
    j = np.arange(2 * w)[None, :]
    rel = i + w - j
    band = (rel >= 0) & (rel <= w)
    tabs = []
    for window, dil in DILATED_PATTERNS:
        bucket = _t5_bucket(np.clip(rel, 0, None) * dil)
        bias = jnp.transpose(rel_bias[bucket], (2, 0, 1)).astype(F32)
        tabs.append(jnp.where(band[None], bias, MASK_VALUE))
    return jnp.stack(tabs)


def dilated_attention(p, batch, seq, rel_bias):
    t = batch * seq
    bias = _attn_bias(rel_bias)
    col0 = RWKV_IN // LANES
    npair = ATTN_DIM // LANES
    spec = lambda off: pl.BlockSpec((seq, LANES), lambda b, hp: (b, col0 + off + hp))
    return pl.pallas_call(
        _attn_kernel,
        grid=(batch, npair),
        in_specs=[spec(0), spec(npair), spec(2 * npair),
                  pl.BlockSpec((3, 2, ATTN_W, 2 * ATTN_W), lambda b, hp: (0, hp, 0, 0))],
        out_specs=pl.BlockSpec((seq, LANES), lambda b, hp: (b, hp)),
        out_shape=jax.ShapeDtypeStruct((t, ATTN_DIM), BF16),
        scratch_shapes=[pltpu.VMEM((3, seq, LANES), F32) for _ in range(3)],
        compiler_params=_params(("parallel", "parallel")),
        name="dilated_attention",
    )(p, p, p, bias)


def _ret_kernel(q_ref, k_ref, v_ref, g_ref, cos_ref, sin_ref, dmat_ref, xi_ref, zeta_ref, gng_ref, gnb_ref,
                hsum_ref, o_ref, state_s):
    c = RET_CHUNK
    tb = q_ref.shape[0]

    @pl.when(pl.program_id(1) == 0)
    def _():
        state_s[...] = jnp.zeros_like(state_s)

    lane = lax.broadcasted_iota(jnp.int32, (c, RET_DIM), 1)
    first_half = (lane % HEAD_DIM) < (HEAD_DIM // 2)

    def rotate(x, cos, sin):
        swapped = jnp.where(first_half, pltpu.roll(x, RET_DIM - HEAD_DIM // 2, axis=1),
                            pltpu.roll(x, HEAD_DIM // 2, axis=1))
        return x * cos + swapped * sin

    heads = range(RET_HEADS)
    hsl = [slice(h * HEAD_DIM, (h + 1) * HEAD_DIM) for h in heads]
    chunk_decay = [(1.0 - 2.0 ** (-5.0 - h)) ** c for h in heads]

    def group_body(g, carry):
        rows_l, qb, kb, qx, kz, vb = [], [], [], [], [], []
        for u in range(RET_UNROLL):
            rows = pl.ds(pl.multiple_of((g * RET_UNROLL + u) * c, c), c)
            cos, sin = cos_ref[rows, :], sin_ref[rows, :]
            q = rotate(q_ref[rows, :], cos, sin)
            k = rotate(k_ref[rows, :], cos, sin) * (HEAD_DIM ** -0.5)
            rows_l.append(rows)
            qb.append(q.astype(BF16))
            kb.append(k.astype(BF16))
            qx.append((q * xi_ref[...]).astype(BF16))
            kz.append((k * zeta_ref[...]).astype(BF16))
            vb.append(v_ref[rows, :].astype(BF16))
        sc = [[(_dot_nt(qb[u][:, sl], kb[u][:, sl]) * dmat_ref[h]).astype(BF16) for h, sl in zip(heads, hsl)]
              for u in range(RET_UNROLL)]
        intra = [[_dot(sc[u][h], vb[u][:, hsl[h]]) for h in heads] for u in range(RET_UNROLL)]
        kv = [[_dot_tn(kz[u][:, sl], vb[u][:, sl]) for sl in hsl] for u in range(RET_UNROLL)]
        states = [state_s[h] for h in heads]
        ys = []
        for u in range(RET_UNROLL):
            ys.append(jnp.concatenate(
                [intra[u][h] + _dot(qx[u][:, hsl[h]], states[h].astype(BF16)) for h in heads], axis=1))
            states = [states[h] * chunk_decay[h] + kv[u][h] for h in heads]
        for h in heads:
            state_s[h] = states[h]
        hsum = hsum_ref[...]
        for rows, y in zip(rows_l, ys):
            mean = _mm_ones(y, hsum) * (1.0 / HEAD_DIM)
            dy = y - mean
            var = _mm_ones(dy * dy, hsum) * (1.0 / HEAD_DIM)
            yn = dy * lax.rsqrt(var + LN_EPS) * gng_ref[...] + gnb_ref[...]
            gate = g_ref[rows, :]
            o_ref[rows, :] = (gate * _sigmoid(gate) * yn).astype(o_ref.dtype)
        return carry

    lax.fori_loop(0, tb // (c * RET_UNROLL), group_body, 0)


def _ret_tables(seq):
    c = RET_CHUNK
    half = HEAD_DIM // 2
    inv = ROPE_BASE ** (-jnp.arange(half, dtype=F32) / half)
    ang = jnp.arange(seq, dtype=F32)[:, None] * inv
    cos, sin = jnp.cos(ang), jnp.sin(ang)
    cos_t = jnp.tile(jnp.concatenate([cos, cos], axis=1), (1, RET_HEADS))
    sin_t = jnp.tile(jnp.concatenate([-sin, sin], axis=1), (1, RET_HEADS))
    log_g = jnp.log1p(-jnp.exp2(-5.0 - jnp.arange(RET_HEADS, dtype=F32)))
    n = jnp.arange(c, dtype=F32)
    diff = n[:, None] - n[None, :]
    dmat = jnp.where(diff >= 0, jnp.exp(log_g[:, None, None] * jnp.maximum(diff, 0.0)), 0.0)
    zeta = jnp.exp(log_g[:, None] * (c - 1 - n))
    xi = jnp.exp(log_g[:, None] * (n + 1))
    widen = lambda tab: jnp.repeat(tab.T, HEAD_DIM, axis=1)
    return cos_t, sin_t, dmat, widen(xi), widen(zeta)


def retention(p, batch, seq, gn_g, gn_b):
    t = batch * seq
    tb = min(RET_BLOCK, seq)
    nblk = seq // tb
    cos_t, sin_t, dmat, xi, zeta = _ret_tables(seq)
    hsum = _head_sum_matrix(RET_DIM)
    col0 = (RWKV_IN + ATTN_IN) // RET_DIM
    spec = lambda off: pl.BlockSpec((tb, RET_DIM), lambda b, j: (b * nblk + j, col0 + off))
    tab = pl.BlockSpec((tb, RET_DIM), lambda b, j: (j, 0))
    consts = [dmat, xi, zeta, gn_g.reshape(1, -1), gn_b.reshape(1, -1), hsum]
    return pl.pallas_call(
        _ret_kernel,
        grid=(batch, nblk),
        in_specs=[spec(0), spec(1), spec(2), spec(3), tab, tab] + [_const_spec(a.shape) for a in consts],
        out_specs=pl.BlockSpec((tb, RET_DIM), lambda b, j: (b * nblk + j, 0)),
        out_shape=jax.ShapeDtypeStruct((t, RET_DIM), BF16),
        scratch_shapes=[pltpu.VMEM((RET_HEADS, HEAD_DIM, HEAD_DIM), F32)],
        compiler_params=_params(("parallel", "arbitrary")),
        name="retention",
    )(p, p, p, p, cos_t, sin_t, *consts)


def _outproj_kernel(ya_ref, yb_ref, yc_ref, x_ref, w_ref, g_ref, b_ref, o_ref, *, alpha):
    acc = _dot(ya_ref[...], w_ref[0:RWKV_DIM, :])
    acc += _dot(yb_ref[...], w_ref[RWKV_DIM:RWKV_DIM + ATTN_DIM, :])
    acc += _dot(yc_ref[...], w_ref[RWKV_DIM + ATTN_DIM:, :])
    o_ref[...] = _layer_norm(alpha * x_ref[...] + acc, g_ref[...], b_ref[...])


def out_projection_ln(ya, yb, yc, x2d, w_bf16, layer, g, b, alpha):
    t, d = x2d.shape
    tm = min(ROW_TILE, t)
    rows = lambda width: pl.BlockSpec((tm, width), lambda i: (i, 0))
    return pl.pallas_call(
        functools.partial(_outproj_kernel, alpha=alpha),
        grid=(t // tm,),
        in_specs=[rows(RWKV_DIM), rows(ATTN_DIM), rows(RET_DIM), rows(d), _layer_spec(w_bf16, layer),
                  _const_spec((1, d)), _const_spec((1, d))],
        out_specs=rows(d),
        out_shape=jax.ShapeDtypeStruct((t, d), F32),
        compiler_params=_params(("parallel",)),
        name="out_projection_ln",
    )(ya, yb, yc, x2d, w_bf16, g.reshape(1, -1), b.reshape(1, -1))


def _swiglu(xb, wg, wu, wd, acc, f_chunk):
    ff = wg.shape[-1]
    starts = list(range(0, ff, f_chunk))
    gate_up = lambda f0: (_dot(xb, wg[:, f0:f0 + f_chunk]), _dot(xb, wu[:, f0:f0 + f_chunk]))
    nxt = gate_up(starts[0])
    for n, f0 in enumerate(starts):
        gate, up = nxt
        if n + 1 < len(starts):
            nxt = gate_up(starts[n + 1])
        hid = (gate * _sigmoid(gate) * up).astype(BF16)
        part = _dot(hid, wd[f0:f0 + f_chunk, :])
        acc = part if acc is None else acc + part
    return acc


def _ffn_kernel(x_ref, wg_ref, wu_ref, wd_ref, g_ref, b_ref, o_ref, *, alpha, f_chunk):
    x = x_ref[...]
    acc = _swiglu(x.astype(BF16), wg_ref, wu_ref, wd_ref, alpha * x, f_chunk)
    o_ref[...] = _layer_norm(acc, g_ref[...], b_ref[...])


def dense_ffn_ln(x2d, wg, wu, wd, layer, g, b, alpha):
    t, d = x2d.shape
    tm = min(ROW_TILE, t)
    return pl.pallas_call(
        functools.partial(_ffn_kernel, alpha=alpha, f_chunk=FFN_SUBCHUNK),
        grid=(t // tm,),
        in_specs=[pl.BlockSpec((tm, d), lambda i: (i, 0)), _layer_spec(wg, layer), _layer_spec(wu, layer),
                  _layer_spec(wd, layer), _const_spec((1, d)), _const_spec((1, d))],
        out_specs=pl.BlockSpec((tm, d), lambda i: (i, 0)),
        out_shape=jax.ShapeDtypeStruct((t, d), F32),
        compiler_params=_params(("parallel",)),
        name="dense_ffn_ln",
    )(x2d, wg, wu, wd, g.reshape(1, -1), b.reshape(1, -1))


def _router_kernel(x_ref, w_ref, o_ref):
    o_ref[...] = _dot_nt(w_ref[...].astype(BF16), x_ref[...].astype(BF16))


def router_logits(x2d, router):
    t, d = x2d.shape
    tm = min(ROW_TILE, t)
    wt = router.T
    return pl.pallas_call(
        _router_kernel,
        grid=(t // tm,),
        in_specs=[pl.BlockSpec((tm, d), lambda i: (i, 0)), _const_spec(wt.shape)],
        out_specs=pl.BlockSpec((N_EXPERTS, tm), lambda i: (0, i)),
        out_shape=jax.ShapeDtypeStruct((N_EXPERTS, t), F32),
        compiler_params=_params(("parallel",)),
        name="router_logits",
    )(x2d, wt)


def _expert_kernel(blk_e_ref, used_ref, x_ref, wg_ref, wu_ref, wd_ref, o_ref):
    i, j = pl.program_id(0), pl.program_id(1)

    @pl.when(i < used_ref[0])
    def _():
        part = _swiglu(x_ref[...], wg_ref.at[0], wu_ref.at[0], wd_ref.at[0], None, FFN_SUBCHUNK)

        @pl.when(j == 0)
        def _():
            o_ref[...] = part

        @pl.when(j > 0)
        def _():
            o_ref[...] += part


def expert_ffn(xs, blk_e, n_used, wg, wu, wd, layer, f_chunk=MOE_F_BLOCK):
    rows, d = xs.shape
    ff = wg.shape[3]
    tm = MOE_TILE
    grid_spec = pltpu.PrefetchScalarGridSpec(
        num_scalar_prefetch=2,
        grid=(rows // tm, ff // f_chunk),
        in_specs=[pl.BlockSpec((tm, d), lambda i, j, be, nu: (i, 0)),
                  pl.BlockSpec((None, 1, d, f_chunk), lambda i, j, be, nu: (layer, be[i], 0, j)),
                  pl.BlockSpec((None, 1, d, f_chunk), lambda i, j, be, nu: (layer, be[i], 0, j)),
                  pl.BlockSpec((None, 1, f_chunk, d), lambda i, j, be, nu: (layer, be[i], j, 0))],
        out_specs=pl.BlockSpec((tm, d), lambda i, j, be, nu: (i, 0)),
    )
    return pl.pallas_call(
        _expert_kernel,
        grid_spec=grid_spec,
        out_shape=jax.ShapeDtypeStruct((rows, d), F32),
        compiler_params=_params(("parallel", "arbitrary")),
        name="expert_ffn",
    )(blk_e, n_used, xs, wg, wu, wd)


def _combine_ln_kernel(x_ref, y0_ref, y1_ref, gate_ref, g_ref, b_ref, o_ref, *, alpha):
    gates = gate_ref[...]
    f = y0_ref[...] * gates[:, 0:1] + y1_ref[...] * gates[:, 1:2]
    o_ref[...] = _layer_norm(alpha * x_ref[...] + f, g_ref[...], b_ref[...])


def combine_ln(x2d, y0, y1, gates, g, b, alpha):
    t, d = x2d.shape
    tm = min(ROW_TILE, t)
    rows = pl.BlockSpec((tm, d), lambda i: (i, 0))
    return pl.pallas_call(
        functools.partial(_combine_ln_kernel, alpha=alpha),
        grid=(t // tm,),
        in_specs=[rows, rows, rows, pl.BlockSpec((tm, TOP_K), lambda i: (i, 0)),
                  _const_spec((1, d)), _const_spec((1, d))],
        out_specs=rows,
        out_shape=jax.ShapeDtypeStruct((t, d), F32),
        compiler_params=_params(("parallel",)),
        name="combine_ln",
    )(x2d, y0, y1, gates, g.reshape(1, -1), b.reshape(1, -1))


def moe_ffn_ln(x2d, router, wg, wu, wd, layer, g, b, alpha):
    t, d = x2d.shape
    tm = MOE_TILE
    logits = router_logits(x2d, router).T
    top_val, top_idx = lax.top_k(logits, TOP_K)
    gates = jax.nn.softmax(top_val, axis=-1)
    member = (top_idx[:, :, None] == jnp.arange(N_EXPERTS)[None, None, :]).any(axis=1)
    counts = member.sum(axis=0).astype(jnp.int32)
    rank = jnp.cumsum(member.astype(jnp.int32), axis=0) - member.astype(jnp.int32)
    padded = (counts + tm - 1) // tm * tm
    pad_ends = jnp.cumsum(padded)
    pad_starts = pad_ends - padded
    pos = jnp.take_along_axis(pad_starts[None, :] + rank, top_idx, axis=1)
    rows = t * TOP_K + N_EXPERTS * tm
    n_blk = rows // tm
    order = jnp.argsort(top_idx.reshape(-1), stable=True).astype(jnp.int32)
    tok_sorted = order // TOP_K
    starts = jnp.cumsum(counts) - counts
    r = jnp.arange(rows, dtype=jnp.int32)
    row_e = jnp.minimum(jnp.searchsorted(pad_ends, r, side='right'), N_EXPERTS - 1).astype(jnp.int32)
    within = r - pad_starts[row_e]
    src = jnp.where(within < counts[row_e], tok_sorted[jnp.minimum(starts[row_e] + within, t * TOP_K - 1)], 0)
    xs = x2d.astype(BF16)[src]
    blk_e = row_e[::tm]
    n_used = (pad_ends[-1:] // tm).astype(jnp.int32)
    ys = expert_ffn(xs, blk_e, n_used, wg, wu, wd, layer)
    return combine_ln(x2d, ys[pos[:, 0]], ys[pos[:, 1]], gates, g, b, alpha)


def kernel(x, w_in, w_out, rwkv_mu, rwkv_w0, rwkv_w_up, rwkv_a0, rwkv_a_up, rwkv_g_up, rwkv_k_k, rwkv_k_a,
           rwkv_r_k, rwkv_ln_g, rwkv_ln_b, ret_gn_g, ret_gn_b, rel_bias, ln_g, ln_b, ffn_w_gate, ffn_w_up,
           ffn_w_down, moe_router, moe_w_gate, moe_w_up, moe_w_down):
    batch, seq, d = x.shape
    depth = w_in.shape[0]
    alpha = (2 * depth) ** 0.25
    h = x.reshape(batch * seq, d)
    w_in, w_out = w_in.astype(BF16), w_out.astype(BF16)
    ffn_w_gate, ffn_w_up, ffn_w_down = ffn_w_gate.astype(BF16), ffn_w_up.astype(BF16), ffn_w_down.astype(BF16)
    moe_w_gate, moe_w_up, moe_w_down = moe_w_gate.astype(BF16), moe_w_up.astype(BF16), moe_w_down.astype(BF16)
    for layer in range(depth):
        p = in_projection(h, w_in, layer)
        ya = rwkv_time_mix(p, batch, seq, rwkv_mu[layer], rwkv_w0[layer], rwkv_w_up[layer], rwkv_a0[layer],
                           rwkv_a_up[layer], rwkv_g_up[layer], rwkv_k_k[layer], rwkv_k_a[layer],
                           rwkv_r_k[layer], rwkv_ln_g[layer], rwkv_ln_b[layer])
        yb = dilated_attention(p, batch, seq, rel_bias)
        yc = retention(p, batch, seq, ret_gn_g[layer], ret_gn_b[layer])
        h = out_projection_ln(ya, yb, yc, h, w_out, layer, ln_g[layer, 0], ln_b[layer, 0], alpha)
        j = layer // 2
        if layer % 2 == 0:
            h = dense_ffn_ln(h, ffn_w_gate, ffn_w_up, ffn_w_down, j, ln_g[layer, 1], ln_b[layer, 1], alpha)
        else:
            h = moe_ffn_ln(h, moe_router[j], moe_w_gate, moe_w_up, moe_w_down, j,
                           ln_g[layer, 1], ln_b[layer, 1], alpha)
    return h.reshape(batch, seq, d)
```

```python
import functools
import math

import numpy as np
import jax
import jax.numpy as jnp
from jax import lax
from jax.experimental import pallas as pl
from jax.experimental.pallas import tpu as pltpu

F32 = jnp.float32
BF16 = jnp.bfloat16
HI = lax.Precision.HIGHEST

HEAD_DIM = 64
RWKV_HEADS = 4
ATTN_HEADS = 8
RET_HEADS = 4
RWKV_DIM = RWKV_HEADS * HEAD_DIM
ATTN_DIM = ATTN_HEADS * HEAD_DIM
RET_DIM = RET_HEADS * HEAD_DIM
DECAY_LORA = 64
ICL_LORA = 64
GATE_LORA = 128
RWKV_IN = 3 * RWKV_DIM + DECAY_LORA + ICL_LORA + GATE_LORA
ATTN_IN = 3 * ATTN_DIM
RET_IN = 4 * RET_DIM
RWKV_GN_EPS = 64e-5
DECAY_SCALE = math.exp(-0.5)
DILATED_PATTERNS = ((128, 1), (512, 4), (2048, 16))
NUM_BUCKETS = 32
MAX_DISTANCE = 2048
ROPE_BASE = 10000.0
N_EXPERTS = 8
TOP_K = 2
LN_EPS = 1e-5

LANES = 128
WKV_CHUNK = 64
WKV_BLOCK = 256
ATTN_W = 128
ATTN_UNROLL = 4
RET_CHUNK = 128
RET_BLOCK = 1024
RET_UNROLL = 2
ROW_TILE = 512
MOE_TILE = 512
MOE_F_BLOCK = 1792
FFN_SUBCHUNK = 256
MASK_VALUE = -1e30
VMEM_LIMIT = 56 * 1024 * 1024


def _dot(a, b, prec=None):
    return jnp.dot(a, b, preferred_element_type=F32, precision=prec)


def _dot_nt(a, b, prec=None):
    return lax.dot_general(a, b, (((1,), (1,)), ((), ())), preferred_element_type=F32, precision=prec)


def _dot_tn(a, b, prec=None):
    return lax.dot_general(a, b, (((0,), (0,)), ((), ())), preferred_element_type=F32, precision=prec)


_DIMS = {"nn": (((1,), (0,)), ((), ())), "nt": (((1,), (1,)), ((), ())), "tn": (((0,), (0,)), ((), ()))}


def _split(x, terms):
    parts = []
    for _ in range(terms - 1):
        hi = x.astype(BF16)
        parts.append(hi)
        x = x - hi.astype(F32)
    parts.append(x.astype(BF16))
    return parts


def _mm(a, b, kind="nn", passes=3):
    dg = lambda p, q: lax.dot_general(p, q, _DIMS[kind], preferred_element_type=F32)
    if passes == 1:
        return dg(a.astype(BF16), b.astype(BF16))
    ah, al = _split(a, 2)
    bh, bl = _split(b, 2)
    return dg(ah, bh) + (dg(al, bh) + dg(ah, bl))


def _mm_ones(x, ones_bf16, ones_first=False, terms=3):
    parts = _split(x, terms)
    m, n = x.shape
    if ones_first:
        full = _dot(ones_bf16, jnp.concatenate(parts, axis=1))
        out = [full[:, i * n:(i + 1) * n] for i in range(terms)]
    else:
        full = _dot(jnp.concatenate(parts, axis=0), ones_bf16)
        out = [full[i * m:(i + 1) * m] for i in range(terms)]
    acc = out[-1]
    for o in reversed(out[:-1]):
        acc = acc + o
    return acc


def _sigmoid(x):
    return 1.0 / (1.0 + jnp.exp(-x))


def _layer_norm(z, g, b):
    mu = jnp.mean(z, axis=-1, keepdims=True)
    d = z - mu
    var = jnp.mean(d * d, axis=-1, keepdims=True)
    return d * lax.rsqrt(var + LN_EPS) * g + b


def _params(sem, vmem=VMEM_LIMIT):
    return pltpu.CompilerParams(dimension_semantics=sem, vmem_limit_bytes=vmem)


def _const_spec(shape):
    nd = len(shape)
    return pl.BlockSpec(shape, lambda *_: (0,) * nd)


def _layer_spec(stacked, layer):
    nd = stacked.ndim - 1
    return pl.BlockSpec((None,) + stacked.shape[1:], lambda *_: (layer,) + (0,) * nd)


def _inproj_kernel(x_ref, w_ref, o_ref, *, n_chunk):
    xb = x_ref[...].astype(BF16)
    for n0 in range(0, o_ref.shape[1], n_chunk):
        o_ref[:, n0:n0 + n_chunk] = _dot(xb, w_ref[:, n0:n0 + n_chunk])


def in_projection(x2d, w_bf16, layer):
    t, d = x2d.shape
    n = w_bf16.shape[2]
    tm = min(ROW_TILE, t)
    return pl.pallas_call(
        functools.partial(_inproj_kernel, n_chunk=512),
        grid=(t // tm,),
        in_specs=[pl.BlockSpec((tm, d), lambda i: (i, 0)), _layer_spec(w_bf16, layer)],
        out_specs=pl.BlockSpec((tm, n), lambda i: (i, 0)),
        out_shape=jax.ShapeDtypeStruct((t, n), F32),
        compiler_params=_params(("parallel",)),
        name="in_projection",
    )(x2d, w_bf16)


def _rwkv_kernel(p_ref, mu_ref, w0_ref, wup_ref, a0_ref, aup_ref, gup_ref, kk_ref, ka_ref, rk_ref,
                 lng_ref, lnb_ref, ltri_ref, same_ref, hsum_ref, o_ref,
                 state_s, prev_s, kt_s, rt_s, bt_s, kn_s, v_s, btg_s, kng_s, etot_s, y_s, rp_s, y0_s, gt_s, zt_s):
    c = WKV_CHUNK
    tb = p_ref.shape[0]
    d = RWKV_DIM
    assert c == HEAD_DIM

    @pl.when(pl.program_id(1) == 0)
    def _():
        state_s[...] = jnp.zeros_like(state_s)
        prev_s[...] = jnp.zeros_like(prev_s)

    p = p_ref[...]
    row = lax.broadcasted_iota(jnp.int32, p.shape, 0)
    shifted = jnp.where(row == 0, prev_s[...], pltpu.roll(p, 1, axis=0))
    prev_s[...] = p[tb - 1:tb, :]
    ps = p + (shifted - p) * mu_ref[...]
    r = ps[:, 0:d]
    k = ps[:, d:2 * d]
    v = ps[:, 2 * d:3 * d]
    xw = ps[:, 3 * d:3 * d + DECAY_LORA]
    xa = ps[:, 3 * d + DECAY_LORA:3 * d + DECAY_LORA + ICL_LORA]
    xg = ps[:, 3 * d + DECAY_LORA + ICL_LORA:]

    hsum = hsum_ref[...]
    logw = -DECAY_SCALE * _sigmoid(w0_ref[...] + _mm(jnp.tanh(xw), wup_ref[...]))
    a = _sigmoid(a0_ref[...] + _mm(xa, aup_ref[...]))
    g = _dot(_sigmoid(xg).astype(BF16), gup_ref[...].astype(BF16))
    kap = k * kk_ref[...]
    kap = kap / jnp.maximum(jnp.sqrt(_mm_ones(kap * kap, hsum)), 1e-12)
    kn = k * (1.0 + (a - 1.0) * ka_ref[...])
    cum = _mm_ones(logw, ltri_ref[...], ones_first=True)
    tot = _mm_ones(logw, same_ref[...], ones_first=True)
    e_neg = jnp.exp(-cum)
    e_rem = jnp.exp(tot - cum)
    nb = -(a * kap)
    kt_s[...] = kap * jnp.exp(cum - logw)
    rt_s[...] = r * jnp.exp(cum)
    bt_s[...] = nb * e_neg
    kn_s[...] = kn * e_neg
    btg_s[...] = nb * e_rem
    kng_s[...] = kn * e_rem
    etot_s[...] = jnp.exp(tot)
    v_s[...] = v

    nh = RWKV_HEADS
    ri = lax.broadcasted_iota(jnp.int32, (c, d), 0)
    ci = lax.broadcasted_iota(jnp.int32, (c, d), 1) % HEAD_DIM
    strict = ci < ri
    incl = ci <= ri
    diag = ci == ri
    eye = diag.astype(F32)
    bi = lax.broadcasted_iota(jnp.int32, (d, d), 0) // HEAD_DIM
    bj = lax.broadcasted_iota(jnp.int32, (d, d), 1) // HEAD_DIM
    blocks = bi == bj
    zero16 = jnp.zeros((), BF16)

    def expand(x16):
        return jnp.where(blocks, jnp.concatenate([x16] * nh, axis=0), zero16)

    def bdmm(a, y, kind="nn", passes=3):
        dg = lambda p_, q_: lax.dot_general(p_, q_, _DIMS[kind], preferred_element_type=F32)
        if passes == 1:
            return dg(a.astype(BF16), expand(y.astype(BF16)))
        ah, al = _split(a, 2)
        yh, yl = _split(y, 2)
        m = a.shape[0]
        both = dg(jnp.concatenate([ah, al], axis=0), expand(yh))
        return both[:m] + (both[m:] + dg(ah, expand(yl)))

    def block_diagonal_of(full):
        outs = []
        for n0 in range(0, full.shape[1], d):
            m = jnp.where(blocks, full[:, n0:n0 + d], 0.0)
            acc = m[0:c]
            for h in range(1, nh):
                acc = acc + m[h * c:(h + 1) * c]
            outs.append(acc)
        return outs

    nchunk = tb // c
    chunks = range(nchunk)
    cat0 = lambda x, y: jnp.concatenate([x, y], axis=0)
    cat1 = lambda x, y: jnp.concatenate([x, y], axis=1)
    levels = int(math.log2(c)) - 1
    get = lambda ref: [ref[j * c:(j + 1) * c, :] for j in chunks]

    kt, rt, vv, btg = get(kt_s), get(rt_s), get(v_s), get(btg_s)
    lhs = [cat0(k_, r_) for k_, r_ in zip(kt, rt)]
    a_b = [bdmm(l_, b_, "nt") for l_, b_ in zip(lhs, get(bt_s))]
    a_k = [bdmm(l_, n_, "nt") for l_, n_ in zip(lhs, get(kn_s))]
    a_ab = [jnp.where(strict, m[:c], 0.0) for m in a_b]
    a_rb = [jnp.where(incl, m[c:], 0.0) for m in a_b]
    a_kr = [cat0(jnp.where(strict, m[:c], 0.0), jnp.where(incl, m[c:], 0.0)) for m in a_k]
    inv = [eye + m for m in a_ab]
    pw = [bdmm(m, m, passes=1) for m in a_ab]
    for lvl in range(levels):
        if lvl < levels - 1:
            both = [bdmm(cat0(x_, p_), p_, passes=1) for x_, p_ in zip(inv, pw)]
            inv = [x_ + b_[:c] for x_, b_ in zip(inv, both)]
            pw = [b_[c:] for b_ in both]
        else:
            inv = [x_ + bdmm(x_, p_, passes=1) for x_, p_ in zip(inv, pw)]
    av = [bdmm(m, v_) for m, v_ in zip(a_kr, vv)]
    wmat = [bdmm(x_, k_, passes=1) for x_, k_ in zip(inv, kt)]
    umat = [bdmm(x_, a_[:c], passes=1) for x_, a_ in zip(inv, av)]
    rw = [bdmm(m, w_, passes=1) for m, w_ in zip(a_rb, wmat)]
    ru = [bdmm(m, u_, passes=1) for m, u_ in zip(a_rb, umat)]
    gz = [block_diagonal_of(_mm(b_, cat1(w_, u_), "tn", passes=1))
          for b_, w_, u_ in zip(btg, wmat, umat)]
    kv = [block_diagonal_of(_mm(n_, v_, "tn"))[0] for n_, v_ in zip(get(kng_s), vv)]
    for j in chunks:
        rows = slice(j * c, (j + 1) * c)
        rp_s[rows, :] = rt[j] + rw[j]
        y0_s[rows, :] = ru[j] + av[j][c:]
        g_diag = jnp.where(diag, jnp.broadcast_to(etot_s[j * c:j * c + 1, :], (c, d)), 0.0)
        gt_s[rows, :] = g_diag + gz[j][0]
        zt_s[rows, :] = gz[j][1] + kv[j]

    state = state_s[...]
    for j in chunks:
        rows = slice(j * c, (j + 1) * c)
        ry = bdmm(cat0(rp_s[rows, :], gt_s[rows, :]), state)
        y_s[rows, :] = ry[:c] + y0_s[rows, :]
        state = ry[c:] + zt_s[rows, :]
    state_s[...] = state

    y = y_s[...]
    mean = _mm_ones(y, hsum) * (1.0 / HEAD_DIM)
    dy = y - mean
    var = _mm_ones(dy * dy, hsum) * (1.0 / HEAD_DIM)
    yn = dy * lax.rsqrt(var + RWKV_GN_EPS) * lng_ref[...] + lnb_ref[...]
    bonus = _mm_ones(r * kn * rk_ref[...], hsum) * v
    o_ref[...] = ((yn + bonus) * g).astype(o_ref.dtype)


def _chunk_masks(tb, c):
    i = np.arange(tb)
    same = (i[:, None] // c) == (i[None, :] // c)
    ltri = same & (i[None, :] <= i[:, None])
    return jnp.asarray(ltri, BF16), jnp.asarray(same, BF16)


def _head_sum_matrix(width):
    i = np.arange(width)
    return jnp.asarray((i[:, None] // HEAD_DIM) == (i[None, :] // HEAD_DIM), BF16)


def rwkv_time_mix(p, batch, seq, mu, w0, w_up, a0, a_up, g_up, k_k, k_a, r_k, ln_g, ln_b):
    t = batch * seq
    tb = min(WKV_BLOCK, seq)
    nblk = seq // tb
    ltri, same = _chunk_masks(tb, WKV_CHUNK)
    hsum = _head_sum_matrix(RWKV_DIM)
    row = lambda a: a.reshape(1, -1)
    consts = [row(mu), row(w0), w_up, row(a0), a_up, g_up, row(k_k), row(k_a), row(r_k), row(ln_g), row(ln_b),
              ltri, same, hsum]
    buf = lambda: pltpu.VMEM((tb, RWKV_DIM), F32)
    return pl.pallas_call(
        _rwkv_kernel,
        grid=(batch, nblk),
        in_specs=[pl.BlockSpec((tb, RWKV_IN), lambda b, j: (b * nblk + j, 0))]
                 + [_const_spec(a.shape) for a in consts],
        out_specs=pl.BlockSpec((tb, RWKV_DIM), lambda b, j: (b * nblk + j, 0)),
        out_shape=jax.ShapeDtypeStruct((t, RWKV_DIM), BF16),
        scratch_shapes=[pltpu.VMEM((HEAD_DIM, RWKV_DIM), F32), pltpu.VMEM((1, RWKV_IN), F32)]
                       + [buf() for _ in range(13)],
        compiler_params=_params(("parallel", "arbitrary")),
        name="rwkv_time_mix",
    )(p, *consts)


def _attn_kernel(q_ref, k_ref, v_ref, bias_ref, o_ref, acc_s, m_s, l_s):
    seq = q_ref.shape[0]
    w = ATTN_W
    scale = HEAD_DIM ** -0.5

    def rows_of(start, dil):
        return pl.ds(start, w) if dil == 1 else pl.ds(start, w, stride=dil)

    lane = lax.broadcasted_iota(jnp.int32, (w, LANES), 1)
    head0 = lane < HEAD_DIM
    zero = jnp.zeros((), BF16)
    one = jnp.ones((), BF16)

    def group(pi, dil, g, firsts):
        rows_l, q_l, k_l, v_l = [], [], [], []
        for u, first in enumerate(firsts):
            b = g * len(firsts) + u
            start = (b % dil) + (b // dil) * (dil * w)
            rows = rows_of(start, dil)
            q = (q_ref[rows, :] * scale).astype(BF16)
            kk = k_ref[rows, :].astype(BF16)
            vv = v_ref[rows, :].astype(BF16)
            if not first:
                prev = rows_of(start - dil * w, dil)
                kk = jnp.concatenate([k_ref[prev, :].astype(BF16), kk], axis=0)
                vv = jnp.concatenate([v_ref[prev, :].astype(BF16), vv], axis=0)
            rows_l.append(rows)
            q_l.append(q)
            k_l.append(kk)
            v_l.append(vv)
        s = [[_dot_nt(jnp.where(head0 if h == 0 else ~head0, q, zero), kk)
              + (bias_ref[pi, h, :, w:] if first else bias_ref[pi, h])
              for h in range(2)] for q, kk, first in zip(q_l, k_l, firsts)]
        m = [[jnp.max(sh, axis=-1, keepdims=True) for sh in su] for su in s]
        pr = [[jnp.exp(sh - mh).astype(BF16) for sh, mh in zip(su, mu)] for su, mu in zip(s, m)]
        kmask = lambda vv: lax.broadcasted_iota(jnp.int32, vv.shape, 1) < HEAD_DIM
        res = [[_dot(pu[0], jnp.where(kmask(vv), vv, one)), _dot(pu[1], jnp.where(kmask(vv), one, vv))]
               for pu, vv in zip(pr, v_l)]
        for rows, ru, mu in zip(rows_l, res, m):
            acc_s[pi, rows, :] = jnp.where(head0, ru[0], ru[1])
            l_s[pi, rows, :] = jnp.where(head0, ru[1], ru[0])
            m_s[pi, rows, :] = jnp.where(head0, mu[0], mu[1])

    n_groups = (seq // w) // ATTN_UNROLL
    for pi, (window, dil) in enumerate(DILATED_PATTERNS):
        flags = [tuple((g * ATTN_UNROLL + u) < dil for u in range(ATTN_UNROLL)) for g in range(n_groups)]
        g0 = 0
        while g0 < n_groups:
            g1 = g0
            while g1 < n_groups and flags[g1] == flags[g0]:
                g1 += 1
            if g1 - g0 == 1:
                group(pi, dil, g0, flags[g0])
            else:
                def body(g, carry, pi=pi, dil=dil, firsts=flags[g0]):
                    group(pi, dil, g, firsts)
                    return carry
                lax.fori_loop(g0, g1, body, 0)
            g0 = g1

    mt = 256

    def merge_body(i, carry):
        rows = pl.ds(pl.multiple_of(i * mt, mt), mt)
        m0, m1, m2 = m_s[0, rows, :], m_s[1, rows, :], m_s[2, rows, :]
        mx = jnp.maximum(jnp.maximum(m0, m1), m2)
        w0, w1, w2 = jnp.exp(m0 - mx), jnp.exp(m1 - mx), jnp.exp(m2 - mx)
        num = w0 * acc_s[0, rows, :] + w1 * acc_s[1, rows, :] + w2 * acc_s[2, rows, :]
        swap = lambda x: pltpu.roll(x, HEAD_DIM, axis=1)
        den = w0 * swap(l_s[0, rows, :]) + w1 * swap(l_s[1, rows, :]) + w2 * swap(l_s[2, rows, :])
        o_ref[rows, :] = (num / den).astype(o_ref.dtype)
        return carry

    lax.fori_loop(0, seq // mt, merge_body, 0)


def _t5_bucket(dist):
    max_exact = NUM_BUCKETS // 2
    large = max_exact + (np.log(np.maximum(dist, max_exact) / max_exact)
                         / math.log(MAX_DISTANCE / max_exact) * (NUM_BUCKETS - max_exact)).astype(np.int32)
    return np.where(dist < max_exact, dist, np.minimum(large, NUM_BUCKETS - 1)).astype(np.int32)


def _attn_bias(rel_bias):
    w = ATTN_W
    i = np.arange(w)[:, None]
    j = np.arange(2 * w)[None, :]
    rel = i + w - j
    band = (rel >= 0) & (rel <= w)
    tabs = []
    for window, dil in DILATED_PATTERNS:
        bucket = _t5_bucket(np.clip(rel, 0, None) * dil)
        onehot = jnp.asarray(bucket[..., None] == np.arange(NUM_BUCKETS), F32)
        bias = jnp.einsum('ijb,bh->hij', onehot, rel_bias.astype(F32), precision=HI)
        tabs.append(jnp.where(band[None], bias, MASK_VALUE))
    return jnp.stack(tabs)


def dilated_attention(p, batch, seq, rel_bias):
    t = batch * seq
    bias = _attn_bias(rel_bias)
    col0 = RWKV_IN // LANES
    npair = ATTN_DIM // LANES
    spec = lambda off: pl.BlockSpec((seq, LANES), lambda b, hp: (b, col0 + off + hp))
    return pl.pallas_call(
        _attn_kernel,
        grid=(batch, npair),
        in_specs=[spec(0), spec(npair), spec(2 * npair),
                  pl.BlockSpec((3, 2, ATTN_W, 2 * ATTN_W), lambda b, hp: (0, hp, 0, 0))],
        out_specs=pl.BlockSpec((seq, LANES), lambda b, hp: (b, hp)),
        out_shape=jax.ShapeDtypeStruct((t, ATTN_DIM), BF16),
        scratch_shapes=[pltpu.VMEM((3, seq, LANES), F32) for _ in range(3)],
        compiler_params=_params(("parallel", "parallel")),
        name="dilated_attention",
    )(p, p, p, bias)


def _ret_kernel(q_ref, k_ref, v_ref, g_ref, cos_ref, sin_ref, dmat_ref, xi_ref, zeta_ref, gng_ref, gnb_ref,
                hsum_ref, o_ref, state_s):
    c = RET_CHUNK
    tb = q_ref.shape[0]

    @pl.when(pl.program_id(1) == 0)
    def _():
        state_s[...] = jnp.zeros_like(state_s)

    lane = lax.broadcasted_iota(jnp.int32, (c, RET_DIM), 1)
    first_half = (lane % HEAD_DIM) < (HEAD_DIM // 2)

    def rotate(x, cos, sin):
        swapped = jnp.where(first_half, pltpu.roll(x, RET_DIM - HEAD_DIM // 2, axis=1),
                            pltpu.roll(x, HEAD_DIM // 2, axis=1))
        return x * cos + swapped * sin

    heads = range(RET_HEADS)
    hsl = [slice(h * HEAD_DIM, (h + 1) * HEAD_DIM) for h in heads]
    chunk_decay = [(1.0 - 2.0 ** (-5.0 - h)) ** c for h in heads]

    def group_body(g, carry):
        rows_l, qb, kb, qx, kz, vb = [], [], [], [], [], []
        for u in range(RET_UNROLL):
            rows = pl.ds(pl.multiple_of((g * RET_UNROLL + u) * c, c), c)
            cos, sin = cos_ref[rows, :], sin_ref[rows, :]
            q = rotate(q_ref[rows, :], cos, sin)
            k = rotate(k_ref[rows, :], cos, sin) * (HEAD_DIM ** -0.5)
            rows_l.append(rows)
            qb.append(q.astype(BF16))
            kb.append(k.astype(BF16))
            qx.append((q * xi_ref[...]).astype(BF16))
            kz.append((k * zeta_ref[...]).astype(BF16))
            vb.append(v_ref[rows, :].astype(BF16))
        sc = [[(_dot_nt(qb[u][:, sl], kb[u][:, sl]) * dmat_ref[h]).astype(BF16) for h, sl in zip(heads, hsl)]
              for u in range(RET_UNROLL)]
        intra = [[_dot(sc[u][h], vb[u][:, hsl[h]]) for h in heads] for u in range(RET_UNROLL)]
        kv = [[_dot_tn(kz[u][:, sl], vb[u][:, sl]) for sl in hsl] for u in range(RET_UNROLL)]
        states = [state_s[h] for h in heads]
        ys = []
        for u in range(RET_UNROLL):
            ys.append(jnp.concatenate(
                [intra[u][h] + _dot(qx[u][:, hsl[h]], states[h].astype(BF16)) for h in heads], axis=1))
            states = [states[h] * chunk_decay[h] + kv[u][h] for h in heads]
        for h in heads:
            state_s[h] = states[h]
        hsum = hsum_ref[...]
        for rows, y in zip(rows_l, ys):
            mean = _mm_ones(y, hsum) * (1.0 / HEAD_DIM)
            dy = y - mean
            var = _mm_ones(dy * dy, hsum) * (1.0 / HEAD_DIM)
            yn = dy * lax.rsqrt(var + LN_EPS) * gng_ref[...] + gnb_ref[...]
            gate = g_ref[rows, :]
            o_ref[rows, :] = (gate * _sigmoid(gate) * yn).astype(o_ref.dtype)
        return carry

    lax.fori_loop(0, tb // (c * RET_UNROLL), group_body, 0)


def _ret_tables(seq):
    c = RET_CHUNK
    half = HEAD_DIM // 2
    inv = ROPE_BASE ** (-jnp.arange(half, dtype=F32) / half)
    ang = jnp.arange(seq, dtype=F32)[:, None] * inv
    cos, sin = jnp.cos(ang), jnp.sin(ang)
    cos_t = jnp.tile(jnp.concatenate([cos, cos], axis=1), (1, RET_HEADS))
    sin_t = jnp.tile(jnp.concatenate([-sin, sin], axis=1), (1, RET_HEADS))
    log_g = jnp.log1p(-jnp.exp2(-5.0 - jnp.arange(RET_HEADS, dtype=F32)))
    n = jnp.arange(c, dtype=F32)
    diff = n[:, None] - n[None, :]
    dmat = jnp.where(diff >= 0, jnp.exp(log_g[:, None, None] * jnp.maximum(diff, 0.0)), 0.0)
    zeta = jnp.exp(log_g[:, None] * (c - 1 - n))
    xi = jnp.exp(log_g[:, None] * (n + 1))
    widen = lambda tab: jnp.repeat(tab.T, HEAD_DIM, axis=1)
    return cos_t, sin_t, dmat, widen(xi), widen(zeta)


def retention(p, batch, seq, gn_g, gn_b):
    t = batch * seq
    tb = min(RET_BLOCK, seq)
    nblk = seq // tb
    cos_t, sin_t, dmat, xi, zeta = _ret_tables(seq)
    hsum = _head_sum_matrix(RET_DIM)
    col0 = (RWKV_IN + ATTN_IN) // RET_DIM
    spec = lambda off: pl.BlockSpec((tb, RET_DIM), lambda b, j: (b * nblk + j, col0 + off))
    tab = pl.BlockSpec((tb, RET_DIM), lambda b, j: (j, 0))
    consts = [dmat, xi, zeta, gn_g.reshape(1, -1), gn_b.reshape(1, -1), hsum]
    return pl.pallas_call(
        _ret_kernel,
        grid=(batch, nblk),
        in_specs=[spec(0), spec(1), spec(2), spec(3), tab, tab] + [_const_spec(a.shape) for a in consts],
        out_specs=pl.BlockSpec((tb, RET_DIM), lambda b, j: (b * nblk + j, 0)),
        out_shape=jax.ShapeDtypeStruct((t, RET_DIM), BF16),
        scratch_shapes=[pltpu.VMEM((RET_HEADS, HEAD_DIM, HEAD_DIM), F32)],
        compiler_params=_params(("parallel", "arbitrary")),
        name="retention",
    )(p, p, p, p, cos_t, sin_t, *consts)


def _outproj_kernel(ya_ref, yb_ref, yc_ref, x_ref, w_ref, g_ref, b_ref, o_ref, *, alpha):
    acc = _dot(ya_ref[...], w_ref[0:RWKV_DIM, :])
    acc += _dot(yb_ref[...], w_ref[RWKV_DIM:RWKV_DIM + ATTN_DIM, :])
    acc += _dot(yc_ref[...], w_ref[RWKV_DIM + ATTN_DIM:, :])
    o_ref[...] = _layer_norm(alpha * x_ref[...] + acc, g_ref[...], b_ref[...])


def out_projection_ln(ya, yb, yc, x2d, w_bf16, layer, g, b, alpha):
    t, d = x2d.shape
    tm = min(ROW_TILE, t)
    rows = lambda width: pl.BlockSpec((tm, width), lambda i: (i, 0))
    return pl.pallas_call(
        functools.partial(_outproj_kernel, alpha=alpha),
        grid=(t // tm,),
        in_specs=[rows(RWKV_DIM), rows(ATTN_DIM), rows(RET_DIM), rows(d), _layer_spec(w_bf16, layer),
                  _const_spec((1, d)), _const_spec((1, d))],
        out_specs=rows(d),
        out_shape=jax.ShapeDtypeStruct((t, d), F32),
        compiler_params=_params(("parallel",)),
        name="out_projection_ln",
    )(ya, yb, yc, x2d, w_bf16, g.reshape(1, -1), b.reshape(1, -1))


def _swiglu(xb, wg, wu, wd, acc, f_chunk):
    ff = wg.shape[-1]
    starts = list(range(0, ff, f_chunk))
    gate_up = lambda f0: (_dot(xb, wg[:, f0:f0 + f_chunk]), _dot(xb, wu[:, f0:f0 + f_chunk]))
    nxt = gate_up(starts[0])
    for n, f0 in enumerate(starts):
        gate, up = nxt
        if n + 1 < len(starts):
            nxt = gate_up(starts[n + 1])
        hid = (gate * _sigmoid(gate) * up).astype(BF16)
        part = _dot(hid, wd[f0:f0 + f_chunk, :])
        acc = part if acc is None else acc + part
    return acc


def _ffn_kernel(x_ref, wg_ref, wu_ref, wd_ref, g_ref, b_ref, o_ref, *, alpha, f_chunk):
    x = x_ref[...]
    acc = _swiglu(x.astype(BF16), wg_ref, wu_ref, wd_ref, alpha * x, f_chunk)
    o_ref[...] = _layer_norm(acc, g_ref[...], b_ref[...])


def dense_ffn_ln(x2d, wg, wu, wd, layer, g, b, alpha):
    t, d = x2d.shape
    tm = min(ROW_TILE, t)
    return pl.pallas_call(
        functools.partial(_ffn_kernel, alpha=alpha, f_chunk=FFN_SUBCHUNK),
        grid=(t // tm,),
        in_specs=[pl.BlockSpec((tm, d), lambda i: (i, 0)), _layer_spec(wg, layer), _layer_spec(wu, layer),
                  _layer_spec(wd, layer), _const_spec((1, d)), _const_spec((1, d))],
        out_specs=pl.BlockSpec((tm, d), lambda i: (i, 0)),
        out_shape=jax.ShapeDtypeStruct((t, d), F32),
        compiler_params=_params(("parallel",)),
        name="dense_ffn_ln",
    )(x2d, wg, wu, wd, g.reshape(1, -1), b.reshape(1, -1))


def _router_kernel(x_ref, w_ref, o_ref):
    o_ref[...] = _dot_nt(w_ref[...].astype(BF16), x_ref[...].astype(BF16))


def router_logits(x2d, router):
    t, d = x2d.shape
    tm = min(ROW_TILE, t)
    wt = router.T
    return pl.pallas_call(
        _router_kernel,
        grid=(t // tm,),
        in_specs=[pl.BlockSpec((tm, d), lambda i: (i, 0)), _const_spec(wt.shape)],
        out_specs=pl.BlockSpec((N_EXPERTS, tm), lambda i: (0, i)),
        out_shape=jax.ShapeDtypeStruct((N_EXPERTS, t), F32),
        compiler_params=_params(("parallel",)),
        name="router_logits",
    )(x2d, wt)


def _expert_kernel(blk_e_ref, used_ref, x_ref, wg_ref, wu_ref, wd_ref, o_ref):
    i, j = pl.program_id(0), pl.program_id(1)

    @pl.when(i < used_ref[0])
    def _():
        part = _swiglu(x_ref[...], wg_ref.at[0], wu_ref.at[0], wd_ref.at[0], None, FFN_SUBCHUNK)

        @pl.when(j == 0)
        def _():
            o_ref[...] = part

        @pl.when(j > 0)
        def _():
            o_ref[...] += part


def expert_ffn(xs, blk_e, n_used, wg, wu, wd, layer, f_chunk=MOE_F_BLOCK):
    rows, d = xs.shape
    ff = wg.shape[3]
    tm = MOE_TILE
    grid_spec = pltpu.PrefetchScalarGridSpec(
        num_scalar_prefetch=2,
        grid=(rows // tm, ff // f_chunk),
        in_specs=[pl.BlockSpec((tm, d), lambda i, j, be, nu: (i, 0)),
                  pl.BlockSpec((None, 1, d, f_chunk), lambda i, j, be, nu: (layer, be[i], 0, j)),
                  pl.BlockSpec((None, 1, d, f_chunk), lambda i, j, be, nu: (layer, be[i], 0, j)),
                  pl.BlockSpec((None, 1, f_chunk, d), lambda i, j, be, nu: (layer, be[i], j, 0))],
        out_specs=pl.BlockSpec((tm, d), lambda i, j, be, nu: (i, 0)),
    )
    return pl.pallas_call(
        _expert_kernel,
        grid_spec=grid_spec,
        out_shape=jax.ShapeDtypeStruct((rows, d), F32),
        compiler_params=_params(("parallel", "arbitrary")),
        name="expert_ffn",
    )(blk_e, n_used, xs, wg, wu, wd)


def _combine_ln_kernel(x_ref, y0_ref, y1_ref, gate_ref, g_ref, b_ref, o_ref, *, alpha):
    gates = gate_ref[...]
    f = y0_ref[...] * gates[:, 0:1] + y1_ref[...] * gates[:, 1:2]
    o_ref[...] = _layer_norm(alpha * x_ref[...] + f, g_ref[...], b_ref[...])


def combine_ln(x2d, y0, y1, gates, g, b, alpha):
    t, d = x2d.shape
    tm = min(ROW_TILE, t)
    rows = pl.BlockSpec((tm, d), lambda i: (i, 0))
    return pl.pallas_call(
        functools.partial(_combine_ln_kernel, alpha=alpha),
        grid=(t // tm,),
        in_specs=[rows, rows, rows, pl.BlockSpec((tm, TOP_K), lambda i: (i, 0)),
                  _const_spec((1, d)), _const_spec((1, d))],
        out_specs=rows,
        out_shape=jax.ShapeDtypeStruct((t, d), F32),
        compiler_params=_params(("parallel",)),
        name="combine_ln",
    )(x2d, y0, y1, gates, g.reshape(1, -1), b.reshape(1, -1))


def moe_ffn_ln(x2d, router, wg, wu, wd, layer, g, b, alpha):
    t, d = x2d.shape
    tm = MOE_TILE
    logits = router_logits(x2d, router).T
    top_val, top_idx = lax.top_k(logits, TOP_K)
    gates = jax.nn.softmax(top_val, axis=-1)
    member = (top_idx[:, :, None] == jnp.arange(N_EXPERTS)[None, None, :]).any(axis=1)
    counts = member.sum(axis=0).astype(jnp.int32)
    rank = jnp.cumsum(member.astype(jnp.int32), axis=0) - member.astype(jnp.int32)
    padded = (counts + tm - 1) // tm * tm
    pad_ends = jnp.cumsum(padded)
    pad_starts = pad_ends - padded
    pos = jnp.take_along_axis(pad_starts[None, :] + rank, top_idx, axis=1)
    rows = t * TOP_K + N_EXPERTS * tm
    n_blk = rows // tm
    order = jnp.argsort(top_idx.reshape(-1), stable=True).astype(jnp.int32)
    tok_sorted = order // TOP_K
    starts = jnp.cumsum(counts) - counts
    r = jnp.arange(rows, dtype=jnp.int32)
    row_e = jnp.minimum(jnp.searchsorted(pad_ends, r, side='right'), N_EXPERTS - 1).astype(jnp.int32)
    within = r - pad_starts[row_e]
    src = jnp.where(within < counts[row_e], tok_sorted[jnp.minimum(starts[row_e] + within, t * TOP_K - 1)], 0)
    xs = x2d.astype(BF16)[src]
    blk_e = row_e[::tm]
    n_used = (pad_ends[-1:] // tm).astype(jnp.int32)
    ys = expert_ffn(xs, blk_e, n_used, wg, wu, wd, layer)
    return combine_ln(x2d, ys[pos[:, 0]], ys[pos[:, 1]], gates, g, b, alpha)


def kernel(x, w_in, w_out, rwkv_mu, rwkv_w0, rwkv_w_up, rwkv_a0, rwkv_a_up, rwkv_g_up, rwkv_k_k, rwkv_k_a,
           rwkv_r_k, rwkv_ln_g, rwkv_ln_b, ret_gn_g, ret_gn_b, rel_bias, ln_g, ln_b, ffn_w_gate, ffn_w_up,
           ffn_w_down, moe_router, moe_w_gate, moe_w_up, moe_w_down):
    batch, seq, d = x.shape
    depth = w_in.shape[0]
    alpha = (2 * depth) ** 0.25
    h = x.reshape(batch * seq, d)
    w_in, w_out = w_in.astype(BF16), w_out.astype(BF16)
    ffn_w_gate, ffn_w_up, ffn_w_down = ffn_w_gate.astype(BF16), ffn_w_up.astype(BF16), ffn_w_down.astype(BF16)
    moe_w_gate, moe_w_up, moe_w_down = moe_w_gate.astype(BF16), moe_w_up.astype(BF16), moe_w_down.astype(BF16)
    for layer in range(depth):
        p = in_projection(h, w_in, layer)
        ya = rwkv_time_mix(p, batch, seq, rwkv_mu[layer], rwkv_w0[layer], rwkv_w_up[layer], rwkv_a0[layer],
                           rwkv_a_up[layer], rwkv_g_up[layer], rwkv_k_k[layer], rwkv_k_a[layer],
                           rwkv_r_k[layer], rwkv_ln_g[layer], rwkv_ln_b[layer])
        yb = dilated_attention(p, batch, seq, rel_bias)
        yc = retention(p, batch, seq, ret_gn_g[layer], ret_gn_b[layer])
        h = out_projection_ln(ya, yb, yc, h, w_out, layer, ln_g[layer, 0], ln_b[layer, 0], alpha)
        j = layer // 2
        if layer % 2 == 0:
            h = dense_ffn_ln(h, ffn_w_gate, ffn_w_up, ffn_w_down, j, ln_g[layer, 1], ln_b[layer, 1], alpha)
        else:
            h = moe_ffn_ln(h, moe_router[j], moe_w_gate, moe_w_up, moe_w_down, j,
                           ln_g[layer, 1], ln_b[layer, 1], alpha)
    return h.reshape(batch, seq, d)
```

```python
import functools
import math

import numpy as np
import jax
import jax.numpy as jnp
from jax import lax
from jax.experimental import pallas as pl
from jax.experimental.pallas import tpu as pltpu

F32 = jnp.float32
BF16 = jnp.bfloat16
HI = lax.Precision.HIGHEST

HEAD_DIM = 64
RWKV_HEADS = 4
ATTN_HEADS = 8
RET_HEADS = 4
RWKV_DIM = RWKV_HEADS * HEAD_DIM
ATTN_DIM = ATTN_HEADS * HEAD_DIM
RET_DIM = RET_HEADS * HEAD_DIM
DECAY_LORA = 64
ICL_LORA = 64
GATE_LORA = 128
RWKV_IN = 3 * RWKV_DIM + DECAY_LORA + ICL_LORA + GATE_LORA
ATTN_IN = 3 * ATTN_DIM
RET_IN = 4 * RET_DIM
RWKV_GN_EPS = 64e-5
DECAY_SCALE = math.exp(-0.5)
DILATED_PATTERNS = ((128, 1), (512, 4), (2048, 16))
NUM_BUCKETS = 32
MAX_DISTANCE = 2048
ROPE_BASE = 10000.0
N_EXPERTS = 8
TOP_K = 2
LN_EPS = 1e-5

LANES = 128
WKV_CHUNK = 64
WKV_BLOCK = 256
ATTN_W = 128
ATTN_UNROLL = 4
RET_CHUNK = 128
RET_BLOCK = 1024
RET_UNROLL = 2
ROW_TILE = 512
MOE_TILE = 512
MOE_F_BLOCK = 1792
FFN_SUBCHUNK = 256
MASK_VALUE = -1e30
VMEM_LIMIT = 56 * 1024 * 1024


def _dot(a, b, prec=None):
    return jnp.dot(a, b, preferred_element_type=F32, precision=prec)


def _dot_nt(a, b, prec=None):
    return lax.dot_general(a, b, (((1,), (1,)), ((), ())), preferred_element_type=F32, precision=prec)


def _dot_tn(a, b, prec=None):
    return lax.dot_general(a, b, (((0,), (0,)), ((), ())), preferred_element_type=F32, precision=prec)


_DIMS = {"nn": (((1,), (0,)), ((), ())), "nt": (((1,), (1,)), ((), ())), "tn": (((0,), (0,)), ((), ()))}


def _split(x, terms):
    parts = []
    for _ in range(terms - 1):
        hi = x.astype(BF16)
        parts.append(hi)
        x = x - hi.astype(F32)
    parts.append(x.astype(BF16))
    return parts


def _mm(a, b, kind="nn", passes=3):
    dg = lambda p, q: lax.dot_general(p, q, _DIMS[kind], preferred_element_type=F32)
    if passes == 1:
        return dg(a.astype(BF16), b.astype(BF16))
    ah, al = _split(a, 2)
    bh, bl = _split(b, 2)
    return dg(ah, bh) + (dg(al, bh) + dg(ah, bl))


def _mm_ones(x, ones_bf16, ones_first=False, terms=3):
    parts = _split(x, terms)
    m, n = x.shape
    if ones_first:
        full = _dot(ones_bf16, jnp.concatenate(parts, axis=1))
        out = [full[:, i * n:(i + 1) * n] for i in range(terms)]
    else:
        full = _dot(jnp.concatenate(parts, axis=0), ones_bf16)
        out = [full[i * m:(i + 1) * m] for i in range(terms)]
    acc = out[-1]
    for o in reversed(out[:-1]):
        acc = acc + o
    return acc


def _sigmoid(x):
    return 1.0 / (1.0 + jnp.exp(-x))


def _layer_norm(z, g, b):
    mu = jnp.mean(z, axis=-1, keepdims=True)
    d = z - mu
    var = jnp.mean(d * d, axis=-1, keepdims=True)
    return d * lax.rsqrt(var + LN_EPS) * g + b


def _params(sem, vmem=VMEM_LIMIT):
    return pltpu.CompilerParams(dimension_semantics=sem, vmem_limit_bytes=vmem)


def _const_spec(shape):
    nd = len(shape)
    return pl.BlockSpec(shape, lambda *_: (0,) * nd)


def _layer_spec(stacked, layer):
    nd = stacked.ndim - 1
    return pl.BlockSpec((None,) + stacked.shape[1:], lambda *_: (layer,) + (0,) * nd)


CAST_BLOCK_BYTES = 4 * 1024 * 1024


def _cast_kernel(x_ref, o_ref):
    o_ref[...] = x_ref[...].astype(o_ref.dtype)


def to_bf16(w):
    shape = w.shape
    cols = shape[-1]
    rows = w.size // cols
    w2 = w.reshape(rows, cols)
    tr = 1 << int(math.log2(max(16, min(rows, CAST_BLOCK_BYTES // (4 * cols)))))
    while rows % tr:
        tr //= 2
    assert tr % 16 == 0
    out = pl.pallas_call(
        _cast_kernel,
        grid=(rows // tr,),
        in_specs=[pl.BlockSpec((tr, cols), lambda i: (i, 0))],
        out_specs=pl.BlockSpec((tr, cols), lambda i: (i, 0)),
        out_shape=jax.ShapeDtypeStruct((rows, cols), BF16),
        compiler_params=_params(("parallel",)),
        name="to_bf16",
    )(w2)
    return out.reshape(shape)


def _inproj_kernel(x_ref, w_ref, o_ref, *, n_chunk):
    xb = x_ref[...].astype(BF16)
    for n0 in range(0, o_ref.shape[1], n_chunk):
        o_ref[:, n0:n0 + n_chunk] = _dot(xb, w_ref[:, n0:n0 + n_chunk])


def in_projection(x2d, w_bf16, layer):
    t, d = x2d.shape
    n = w_bf16.shape[2]
    tm = min(ROW_TILE, t)
    return pl.pallas_call(
        functools.partial(_inproj_kernel, n_chunk=512),
        grid=(t // tm,),
        in_specs=[pl.BlockSpec((tm, d), lambda i: (i, 0)), _layer_spec(w_bf16, layer)],
        out_specs=pl.BlockSpec((tm, n), lambda i: (i, 0)),
        out_shape=jax.ShapeDtypeStruct((t, n), F32),
        compiler_params=_params(("parallel",)),
        name="in_projection",
    )(x2d, w_bf16)


def _rwkv_kernel(p_ref, mu_ref, w0_ref, wup_ref, a0_ref, aup_ref, gup_ref, kk_ref, ka_ref, rk_ref,
                 lng_ref, lnb_ref, ltri_ref, same_ref, hsum_ref, o_ref,
                 state_s, prev_s, kt_s, rt_s, bt_s, kn_s, v_s, btg_s, kng_s, etot_s, y_s, rp_s, y0_s, gt_s, zt_s):
    c = WKV_CHUNK
    tb = p_ref.shape[0]
    d = RWKV_DIM
    assert c == HEAD_DIM

    @pl.when(pl.program_id(1) == 0)
    def _():
        state_s[...] = jnp.zeros_like(state_s)
        prev_s[...] = jnp.zeros_like(prev_s)

    p = p_ref[...]
    row = lax.broadcasted_iota(jnp.int32, p.shape, 0)
    shifted = jnp.where(row == 0, prev_s[...], pltpu.roll(p, 1, axis=0))
    prev_s[...] = p[tb - 1:tb, :]
    ps = p + (shifted - p) * mu_ref[...]
    r = ps[:, 0:d]
    k = ps[:, d:2 * d]
    v = ps[:, 2 * d:3 * d]
    xw = ps[:, 3 * d:3 * d + DECAY_LORA]
    xa = ps[:, 3 * d + DECAY_LORA:3 * d + DECAY_LORA + ICL_LORA]
    xg = ps[:, 3 * d + DECAY_LORA + ICL_LORA:]

    hsum = hsum_ref[...]
    logw = -DECAY_SCALE * _sigmoid(w0_ref[...] + _mm(jnp.tanh(xw), wup_ref[...]))
    a = _sigmoid(a0_ref[...] + _mm(xa, aup_ref[...]))
    g = _dot(_sigmoid(xg).astype(BF16), gup_ref[...].astype(BF16))
    kap = k * kk_ref[...]
    kap = kap / jnp.maximum(jnp.sqrt(_mm_ones(kap * kap, hsum)), 1e-12)
    kn = k * (1.0 + (a - 1.0) * ka_ref[...])
    cum = _mm_ones(logw, ltri_ref[...], ones_first=True)
    tot = _mm_ones(logw, same_ref[...], ones_first=True)
    e_neg = jnp.exp(-cum)
    e_rem = jnp.exp(tot - cum)
    nb = -(a * kap)
    kt_s[...] = kap * jnp.exp(cum - logw)
    rt_s[...] = r * jnp.exp(cum)
    bt_s[...] = nb * e_neg
    kn_s[...] = kn * e_neg
    btg_s[...] = nb * e_rem
    kng_s[...] = kn * e_rem
    etot_s[...] = jnp.exp(tot)
    v_s[...] = v

    nh = RWKV_HEADS
    ri = lax.broadcasted_iota(jnp.int32, (c, d), 0)
    ci = lax.broadcasted_iota(jnp.int32, (c, d), 1) % HEAD_DIM
    strict = ci < ri
    incl = ci <= ri
    diag = ci == ri
    eye = diag.astype(F32)
    bi = lax.broadcasted_iota(jnp.int32, (d, d), 0) // HEAD_DIM
    bj = lax.broadcasted_iota(jnp.int32, (d, d), 1) // HEAD_DIM
    blocks = bi == bj
    zero16 = jnp.zeros((), BF16)

    def expand(x16):
        return jnp.where(blocks, jnp.concatenate([x16] * nh, axis=0), zero16)

    def bdmm(a, y, kind="nn", passes=3):
        dg = lambda p_, q_: lax.dot_general(p_, q_, _DIMS[kind], preferred_element_type=F32)
        if passes == 1:
            return dg(a.astype(BF16), expand(y.astype(BF16)))
        ah, al = _split(a, 2)
        yh, yl = _split(y, 2)
        m = a.shape[0]
        both = dg(jnp.concatenate([ah, al], axis=0), expand(yh))
        return both[:m] + (both[m:] + dg(ah, expand(yl)))

    def block_diagonal_of(full):
        outs = []
        for n0 in range(0, full.shape[1], d):
            m = jnp.where(blocks, full[:, n0:n0 + d], 0.0)
            acc = m[0:c]
            for h in range(1, nh):
                acc = acc + m[h * c:(h + 1) * c]
            outs.append(acc)
        return outs

    nchunk = tb // c
    chunks = range(nchunk)
    cat0 = lambda x, y: jnp.concatenate([x, y], axis=0)
    cat1 = lambda x, y: jnp.concatenate([x, y], axis=1)
    levels = int(math.log2(c)) - 1
    get = lambda ref: [ref[j * c:(j + 1) * c, :] for j in chunks]

    kt, rt, vv, btg = get(kt_s), get(rt_s), get(v_s), get(btg_s)
    lhs = [cat0(k_, r_) for k_, r_ in zip(kt, rt)]
    a_b = [bdmm(l_, b_, "nt") for l_, b_ in zip(lhs, get(bt_s))]
    a_k = [bdmm(l_, n_, "nt") for l_, n_ in zip(lhs, get(kn_s))]
    a_ab = [jnp.where(strict, m[:c], 0.0) for m in a_b]
    a_rb = [jnp.where(incl, m[c:], 0.0) for m in a_b]
    a_kr = [cat0(jnp.where(strict, m[:c], 0.0), jnp.where(incl, m[c:], 0.0)) for m in a_k]
    inv = [eye + m for m in a_ab]
    pw = [bdmm(m, m, passes=1) for m in a_ab]
    for lvl in range(levels):
        if lvl < levels - 1:
            both = [bdmm(cat0(x_, p_), p_, passes=1) for x_, p_ in zip(inv, pw)]
            inv = [x_ + b_[:c] for x_, b_ in zip(inv, both)]
            pw = [b_[c:] for b_ in both]
        else:
            inv = [x_ + bdmm(x_, p_, passes=1) for x_, p_ in zip(inv, pw)]
    av = [bdmm(m, v_) for m, v_ in zip(a_kr, vv)]
    wmat = [bdmm(x_, k_, passes=1) for x_, k_ in zip(inv, kt)]
    umat = [bdmm(x_, a_[:c], passes=1) for x_, a_ in zip(inv, av)]
    rw = [bdmm(m, w_, passes=1) for m, w_ in zip(a_rb, wmat)]
    ru = [bdmm(m, u_, passes=1) for m, u_ in zip(a_rb, umat)]
    gz = [block_diagonal_of(_mm(b_, cat1(w_, u_), "tn", passes=1))
          for b_, w_, u_ in zip(btg, wmat, umat)]
    kv = [block_diagonal_of(_mm(n_, v_, "tn"))[0] for n_, v_ in zip(get(kng_s), vv)]
    for j in chunks:
        rows = slice(j * c, (j + 1) * c)
        rp_s[rows, :] = rt[j] + rw[j]
        y0_s[rows, :] = ru[j] + av[j][c:]
        g_diag = jnp.where(diag, jnp.broadcast_to(etot_s[j * c:j * c + 1, :], (c, d)), 0.0)
        gt_s[rows, :] = g_diag + gz[j][0]
        zt_s[rows, :] = gz[j][1] + kv[j]

    state = state_s[...]
    for j in chunks:
        rows = slice(j * c, (j + 1) * c)
        ry = bdmm(cat0(rp_s[rows, :], gt_s[rows, :]), state)
        y_s[rows, :] = ry[:c] + y0_s[rows, :]
        state = ry[c:] + zt_s[rows, :]
    state_s[...] = state

    y = y_s[...]
    mean = _mm_ones(y, hsum) * (1.0 / HEAD_DIM)
    dy = y - mean
    var = _mm_ones(dy * dy, hsum) * (1.0 / HEAD_DIM)
    yn = dy * lax.rsqrt(var + RWKV_GN_EPS) * lng_ref[...] + lnb_ref[...]
    bonus = _mm_ones(r * kn * rk_ref[...], hsum) * v
    o_ref[...] = ((yn + bonus) * g).astype(o_ref.dtype)


def _chunk_masks(tb, c):
    i = np.arange(tb)
    same = (i[:, None] // c) == (i[None, :] // c)
    ltri = same & (i[None, :] <= i[:, None])
    return jnp.asarray(ltri, BF16), jnp.asarray(same, BF16)


def _head_sum_matrix(width):
    i = np.arange(width)
    return jnp.asarray((i[:, None] // HEAD_DIM) == (i[None, :] // HEAD_DIM), BF16)


def rwkv_time_mix(p, batch, seq, mu, w0, w_up, a0, a_up, g_up, k_k, k_a, r_k, ln_g, ln_b):
    t = batch * seq
    tb = min(WKV_BLOCK, seq)
    nblk = seq // tb
    ltri, same = _chunk_masks(tb, WKV_CHUNK)
    hsum = _head_sum_matrix(RWKV_DIM)
    row = lambda a: a.reshape(1, -1)
    consts = [row(mu), row(w0), w_up, row(a0), a_up, g_up, row(k_k), row(k_a), row(r_k), row(ln_g), row(ln_b),
              ltri, same, hsum]
    buf = lambda: pltpu.VMEM((tb, RWKV_DIM), F32)
    return pl.pallas_call(
        _rwkv_kernel,
        grid=(batch, nblk),
        in_specs=[pl.BlockSpec((tb, RWKV_IN), lambda b, j: (b * nblk + j, 0))]
                 + [_const_spec(a.shape) for a in consts],
        out_specs=pl.BlockSpec((tb, RWKV_DIM), lambda b, j: (b * nblk + j, 0)),
        out_shape=jax.ShapeDtypeStruct((t, RWKV_DIM), BF16),
        scratch_shapes=[pltpu.VMEM((HEAD_DIM, RWKV_DIM), F32), pltpu.VMEM((1, RWKV_IN), F32)]
                       + [buf() for _ in range(13)],
        compiler_params=_params(("parallel", "arbitrary")),
        name="rwkv_time_mix",
    )(p, *consts)


def _attn_kernel(q_ref, k_ref, v_ref, bias_ref, o_ref, acc_s, m_s, l_s):
    seq = q_ref.shape[0]
    w = ATTN_W
    scale = HEAD_DIM ** -0.5

    def rows_of(start, dil):
        return pl.ds(start, w) if dil == 1 else pl.ds(start, w, stride=dil)

    lane = lax.broadcasted_iota(jnp.int32, (w, LANES), 1)
    head0 = lane < HEAD_DIM
    zero = jnp.zeros((), BF16)
    one = jnp.ones((), BF16)

    def group(pi, dil, g, firsts):
        rows_l, q_l, k_l, v_l = [], [], [], []
        for u, first in enumerate(firsts):
            b = g * len(firsts) + u
            start = (b % dil) + (b // dil) * (dil * w)
            rows = rows_of(start, dil)
            q = (q_ref[rows, :] * scale).astype(BF16)
            kk = k_ref[rows, :].astype(BF16)
            vv = v_ref[rows, :].astype(BF16)
            if not first:
                prev = rows_of(start - dil * w, dil)
                kk = jnp.concatenate([k_ref[prev, :].astype(BF16), kk], axis=0)
                vv = jnp.concatenate([v_ref[prev, :].astype(BF16), vv], axis=0)
            rows_l.append(rows)
            q_l.append(q)
            k_l.append(kk)
            v_l.append(vv)
        s = [[_dot_nt(jnp.where(head0 if h == 0 else ~head0, q, zero), kk)
              + (bias_ref[pi, h, :, w:] if first else bias_ref[pi, h])
              for h in range(2)] for q, kk, first in zip(q_l, k_l, firsts)]
        m = [[jnp.max(sh, axis=-1, keepdims=True) for sh in su] for su in s]
        pr = [[jnp.exp(sh - mh).astype(BF16) for sh, mh in zip(su, mu)] for su, mu in zip(s, m)]
        kmask = lambda vv: lax.broadcasted_iota(jnp.int32, vv.shape, 1) < HEAD_DIM
        res = [[_dot(pu[0], jnp.where(kmask(vv), vv, one)), _dot(pu[1], jnp.where(kmask(vv), one, vv))]
               for pu, vv in zip(pr, v_l)]
        for rows, ru, mu in zip(rows_l, res, m):
            acc_s[pi, rows, :] = jnp.where(head0, ru[0], ru[1])
            l_s[pi, rows, :] = jnp.where(head0, ru[1], ru[0])
            m_s[pi, rows, :] = jnp.where(head0, mu[0], mu[1])

    n_groups = (seq // w) // ATTN_UNROLL
    for pi, (window, dil) in enumerate(DILATED_PATTERNS):
        flags = [tuple((g * ATTN_UNROLL + u) < dil for u in range(ATTN_UNROLL)) for g in range(n_groups)]
        g0 = 0
        while g0 < n_groups:
            g1 = g0
            while g1 < n_groups and flags[g1] == flags[g0]:
                g1 += 1
            if g1 - g0 == 1:
                group(pi, dil, g0, flags[g0])
            else:
                def body(g, carry, pi=pi, dil=dil, firsts=flags[g0]):
                    group(pi, dil, g, firsts)
                    return carry
                lax.fori_loop(g0, g1, body, 0)
            g0 = g1

    mt = 256

    def merge_body(i, carry):
        rows = pl.ds(pl.multiple_of(i * mt, mt), mt)
        m0, m1, m2 = m_s[0, rows, :], m_s[1, rows, :], m_s[2, rows, :]
        mx = jnp.maximum(jnp.maximum(m0, m1), m2)
        w0, w1, w2 = jnp.exp(m0 - mx), jnp.exp(m1 - mx), jnp.exp(m2 - mx)
        num = w0 * acc_s[0, rows, :] + w1 * acc_s[1, rows, :] + w2 * acc_s[2, rows, :]
        swap = lambda x: pltpu.roll(x, HEAD_DIM, axis=1)
        den = w0 * swap(l_s[0, rows, :]) + w1 * swap(l_s[1, rows, :]) + w2 * swap(l_s[2, rows, :])
        o_ref[rows, :] = (num / den).astype(o_ref.dtype)
        return carry

    lax.fori_loop(0, seq // mt, merge_body, 0)


def _t5_bucket(dist):
    max_exact = NUM_BUCKETS // 2
    large = max_exact + (np.log(np.maximum(dist, max_exact) / max_exact)
                         / math.log(MAX_DISTANCE / max_exact) * (NUM_BUCKETS - max_exact)).astype(np.int32)
    return np.where(dist < max_exact, dist, np.minimum(large, NUM_BUCKETS - 1)).astype(np.int32)


def _attn_bias(rel_bias):
    w = ATTN_W
    i = np.arange(w)[:, None]
    j = np.arange(2 * w)[None, :]
    rel = i + w - j
    band = (rel >= 0) & (rel <= w)
    tabs = []
    for window, dil in DILATED_PATTERNS:
        bucket = _t5_bucket(np.clip(rel, 0, None) * dil)
        onehot = jnp.asarray(bucket[..., None] == np.arange(NUM_BUCKETS), F32)
        bias = jnp.einsum('ijb,bh->hij', onehot, rel_bias.astype(F32), precision=HI)
        tabs.append(jnp.where(band[None], bias, MASK_VALUE))
    return jnp.stack(tabs)


def dilated_attention(p, batch, seq, rel_bias):
    t = batch * seq
    bias = _attn_bias(rel_bias)
    col0 = RWKV_IN // LANES
    npair = ATTN_DIM // LANES
    spec = lambda off: pl.BlockSpec((seq, LANES), lambda b, hp: (b, col0 + off + hp))
    return pl.pallas_call(
        _attn_kernel,
        grid=(batch, npair),
        in_specs=[spec(0), spec(npair), spec(2 * npair),
                  pl.BlockSpec((3, 2, ATTN_W, 2 * ATTN_W), lambda b, hp: (0, hp, 0, 0))],
        out_specs=pl.BlockSpec((seq, LANES), lambda b, hp: (b, hp)),
        out_shape=jax.ShapeDtypeStruct((t, ATTN_DIM), BF16),
        scratch_shapes=[pltpu.VMEM((3, seq, LANES), F32) for _ in range(3)],
        compiler_params=_params(("parallel", "parallel")),
        name="dilated_attention",
    )(p, p, p, bias)


def _ret_kernel(q_ref, k_ref, v_ref, g_ref, cos_ref, sin_ref, dmat_ref, xi_ref, zeta_ref, gng_ref, gnb_ref,
                hsum_ref, o_ref, state_s):
    c = RET_CHUNK
    tb = q_ref.shape[0]

    @pl.when(pl.program_id(1) == 0)
    def _():
        state_s[...] = jnp.zeros_like(state_s)

    lane = lax.broadcasted_iota(jnp.int32, (c, RET_DIM), 1)
    first_half = (lane % HEAD_DIM) < (HEAD_DIM // 2)

    def rotate(x, cos, sin):
        swapped = jnp.where(first_half, pltpu.roll(x, RET_DIM - HEAD_DIM // 2, axis=1),
                            pltpu.roll(x, HEAD_DIM // 2, axis=1))
        return x * cos + swapped * sin

    heads = range(RET_HEADS)
    hsl = [slice(h * HEAD_DIM, (h + 1) * HEAD_DIM) for h in heads]
    chunk_decay = [(1.0 - 2.0 ** (-5.0 - h)) ** c for h in heads]

    def group_body(g, carry):
        rows_l, qb, kb, qx, kz, vb = [], [], [], [], [], []
        for u in range(RET_UNROLL):
            rows = pl.ds(pl.multiple_of((g * RET_UNROLL + u) * c, c), c)
            cos, sin = cos_ref[rows, :], sin_ref[rows, :]
            q = rotate(q_ref[rows, :], cos, sin)
            k = rotate(k_ref[rows, :], cos, sin) * (HEAD_DIM ** -0.5)
            rows_l.append(rows)
            qb.append(q.astype(BF16))
            kb.append(k.astype(BF16))
            qx.append((q * xi_ref[...]).astype(BF16))
            kz.append((k * zeta_ref[...]).astype(BF16))
            vb.append(v_ref[rows, :].astype(BF16))
        sc = [[(_dot_nt(qb[u][:, sl], kb[u][:, sl]) * dmat_ref[h]).astype(BF16) for h, sl in zip(heads, hsl)]
              for u in range(RET_UNROLL)]
        intra = [[_dot(sc[u][h], vb[u][:, hsl[h]]) for h in heads] for u in range(RET_UNROLL)]
        kv = [[_dot_tn(kz[u][:, sl], vb[u][:, sl]) for sl in hsl] for u in range(RET_UNROLL)]
        states = [state_s[h] for h in heads]
        ys = []
        for u in range(RET_UNROLL):
            ys.append(jnp.concatenate(
                [intra[u][h] + _dot(qx[u][:, hsl[h]], states[h].astype(BF16)) for h in heads], axis=1))
            states = [states[h] * chunk_decay[h] + kv[u][h] for h in heads]
        for h in heads:
            state_s[h] = states[h]
        hsum = hsum_ref[...]
        for rows, y in zip(rows_l, ys):
            mean = _mm_ones(y, hsum) * (1.0 / HEAD_DIM)
            dy = y - mean
            var = _mm_ones(dy * dy, hsum) * (1.0 / HEAD_DIM)
            yn = dy * lax.rsqrt(var + LN_EPS) * gng_ref[...] + gnb_ref[...]
            gate = g_ref[rows, :]
            o_ref[rows, :] = (gate * _sigmoid(gate) * yn).astype(o_ref.dtype)
        return carry

    lax.fori_loop(0, tb // (c * RET_UNROLL), group_body, 0)


def _ret_tables(seq):
    c = RET_CHUNK
    half = HEAD_DIM // 2
    inv = ROPE_BASE ** (-jnp.arange(half, dtype=F32) / half)
    ang = jnp.arange(seq, dtype=F32)[:, None] * inv
    cos, sin = jnp.cos(ang), jnp.sin(ang)
    cos_t = jnp.tile(jnp.concatenate([cos, cos], axis=1), (1, RET_HEADS))
    sin_t = jnp.tile(jnp.concatenate([-sin, sin], axis=1), (1, RET_HEADS))
    log_g = jnp.log1p(-jnp.exp2(-5.0 - jnp.arange(RET_HEADS, dtype=F32)))
    n = jnp.arange(c, dtype=F32)
    diff = n[:, None] - n[None, :]
    dmat = jnp.where(diff >= 0, jnp.exp(log_g[:, None, None] * jnp.maximum(diff, 0.0)), 0.0)
    zeta = jnp.exp(log_g[:, None] * (c - 1 - n))
    xi = jnp.exp(log_g[:, None] * (n + 1))
    widen = lambda tab: jnp.repeat(tab.T, HEAD_DIM, axis=1)
    return cos_t, sin_t, dmat, widen(xi), widen(zeta)


def retention(p, batch, seq, gn_g, gn_b):
    t = batch * seq
    tb = min(RET_BLOCK, seq)
    nblk = seq // tb
    cos_t, sin_t, dmat, xi, zeta = _ret_tables(seq)
    hsum = _head_sum_matrix(RET_DIM)
    col0 = (RWKV_IN + ATTN_IN) // RET_DIM
    spec = lambda off: pl.BlockSpec((tb, RET_DIM), lambda b, j: (b * nblk + j, col0 + off))
    tab = pl.BlockSpec((tb, RET_DIM), lambda b, j: (j, 0))
    consts = [dmat, xi, zeta, gn_g.reshape(1, -1), gn_b.reshape(1, -1), hsum]
    return pl.pallas_call(
        _ret_kernel,
        grid=(batch, nblk),
        in_specs=[spec(0), spec(1), spec(2), spec(3), tab, tab] + [_const_spec(a.shape) for a in consts],
        out_specs=pl.BlockSpec((tb, RET_DIM), lambda b, j: (b * nblk + j, 0)),
        out_shape=jax.ShapeDtypeStruct((t, RET_DIM), BF16),
        scratch_shapes=[pltpu.VMEM((RET_HEADS, HEAD_DIM, HEAD_DIM), F32)],
        compiler_params=_params(("parallel", "arbitrary")),
        name="retention",
    )(p, p, p, p, cos_t, sin_t, *consts)


def _outproj_kernel(ya_ref, yb_ref, yc_ref, x_ref, w_ref, g_ref, b_ref, o_ref, *, alpha):
    acc = _dot(ya_ref[...], w_ref[0:RWKV_DIM, :])
    acc += _dot(yb_ref[...], w_ref[RWKV_DIM:RWKV_DIM + ATTN_DIM, :])
    acc += _dot(yc_ref[...], w_ref[RWKV_DIM + ATTN_DIM:, :])
    o_ref[...] = _layer_norm(alpha * x_ref[...] + acc, g_ref[...], b_ref[...])


def out_projection_ln(ya, yb, yc, x2d, w_bf16, layer, g, b, alpha):
    t, d = x2d.shape
    tm = min(ROW_TILE, t)
    rows = lambda width: pl.BlockSpec((tm, width), lambda i: (i, 0))
    return pl.pallas_call(
        functools.partial(_outproj_kernel, alpha=alpha),
        grid=(t // tm,),
        in_specs=[rows(RWKV_DIM), rows(ATTN_DIM), rows(RET_DIM), rows(d), _layer_spec(w_bf16, layer),
                  _const_spec((1, d)), _const_spec((1, d))],
        out_specs=rows(d),
        out_shape=jax.ShapeDtypeStruct((t, d), F32),
        compiler_params=_params(("parallel",)),
        name="out_projection_ln",
    )(ya, yb, yc, x2d, w_bf16, g.reshape(1, -1), b.reshape(1, -1))


def _swiglu(xb, wg, wu, wd, acc, f_chunk):
    ff = wg.shape[-1]
    starts = list(range(0, ff, f_chunk))
    gate_up = lambda f0: (_dot(xb, wg[:, f0:f0 + f_chunk]), _dot(xb, wu[:, f0:f0 + f_chunk]))
    nxt = gate_up(starts[0])
    for n, f0 in enumerate(starts):
        gate, up = nxt
        if n + 1 < len(starts):
            nxt = gate_up(starts[n + 1])
        hid = (gate * _sigmoid(gate) * up).astype(BF16)
        part = _dot(hid, wd[f0:f0 + f_chunk, :])
        acc = part if acc is None else acc + part
    return acc


def _ffn_kernel(x_ref, wg_ref, wu_ref, wd_ref, g_ref, b_ref, o_ref, *, alpha, f_chunk):
    x = x_ref[...]
    acc = _swiglu(x.astype(BF16), wg_ref, wu_ref, wd_ref, alpha * x, f_chunk)
    o_ref[...] = _layer_norm(acc, g_ref[...], b_ref[...])


def dense_ffn_ln(x2d, wg, wu, wd, layer, g, b, alpha):
    t, d = x2d.shape
    tm = min(ROW_TILE, t)
    return pl.pallas_call(
        functools.partial(_ffn_kernel, alpha=alpha, f_chunk=FFN_SUBCHUNK),
        grid=(t // tm,),
        in_specs=[pl.BlockSpec((tm, d), lambda i: (i, 0)), _layer_spec(wg, layer), _layer_spec(wu, layer),
                  _layer_spec(wd, layer), _const_spec((1, d)), _const_spec((1, d))],
        out_specs=pl.BlockSpec((tm, d), lambda i: (i, 0)),
        out_shape=jax.ShapeDtypeStruct((t, d), F32),
        compiler_params=_params(("parallel",)),
        name="dense_ffn_ln",
    )(x2d, wg, wu, wd, g.reshape(1, -1), b.reshape(1, -1))


def _router_kernel(x_ref, w_ref, o_ref):
    o_ref[...] = _dot_nt(w_ref[...].astype(BF16), x_ref[...].astype(BF16))


def router_logits(x2d, router):
    t, d = x2d.shape
    tm = min(ROW_TILE, t)
    wt = router.T
    return pl.pallas_call(
        _router_kernel,
        grid=(t // tm,),
        in_specs=[pl.BlockSpec((tm, d), lambda i: (i, 0)), _const_spec(wt.shape)],
        out_specs=pl.BlockSpec((N_EXPERTS, tm), lambda i: (0, i)),
        out_shape=jax.ShapeDtypeStruct((N_EXPERTS, t), F32),
        compiler_params=_params(("parallel",)),
        name="router_logits",
    )(x2d, wt)


def _expert_kernel(blk_e_ref, used_ref, x_ref, wg_ref, wu_ref, wd_ref, o_ref):
    i, j = pl.program_id(0), pl.program_id(1)

    @pl.when(i < used_ref[0])
    def _():
        part = _swiglu(x_ref[...].astype(BF16), wg_ref.at[0], wu_ref.at[0], wd_ref.at[0], None, FFN_SUBCHUNK)

        @pl.when(j == 0)
        def _():
            o_ref[...] = part

        @pl.when(j > 0)
        def _():
            o_ref[...] += part


def expert_ffn(xs, blk_e, n_used, wg, wu, wd, layer, f_chunk=MOE_F_BLOCK):
    rows, d = xs.shape
    ff = wg.shape[3]
    tm = MOE_TILE
    grid_spec = pltpu.PrefetchScalarGridSpec(
        num_scalar_prefetch=2,
        grid=(rows // tm, ff // f_chunk),
        in_specs=[pl.BlockSpec((tm, d), lambda i, j, be, nu: (i, 0)),
                  pl.BlockSpec((None, 1, d, f_chunk), lambda i, j, be, nu: (layer, be[i], 0, j)),
                  pl.BlockSpec((None, 1, d, f_chunk), lambda i, j, be, nu: (layer, be[i], 0, j)),
                  pl.BlockSpec((None, 1, f_chunk, d), lambda i, j, be, nu: (layer, be[i], j, 0))],
        out_specs=pl.BlockSpec((tm, d), lambda i, j, be, nu: (i, 0)),
    )
    return pl.pallas_call(
        _expert_kernel,
        grid_spec=grid_spec,
        out_shape=jax.ShapeDtypeStruct((rows, d), F32),
        compiler_params=_params(("parallel", "arbitrary")),
        name="expert_ffn",
    )(blk_e, n_used, xs, wg, wu, wd)


def _combine_ln_kernel(x_ref, y0_ref, y1_ref, gate_ref, g_ref, b_ref, o_ref, *, alpha):
    gates = gate_ref[...]
    f = y0_ref[...] * gates[:, 0:1] + y1_ref[...] * gates[:, 1:2]
    o_ref[...] = _layer_norm(alpha * x_ref[...] + f, g_ref[...], b_ref[...])


def combine_ln(x2d, y0, y1, gates, g, b, alpha):
    t, d = x2d.shape
    tm = min(ROW_TILE, t)
    rows = pl.BlockSpec((tm, d), lambda i: (i, 0))
    return pl.pallas_call(
        functools.partial(_combine_ln_kernel, alpha=alpha),
        grid=(t // tm,),
        in_specs=[rows, rows, rows, pl.BlockSpec((tm, TOP_K), lambda i: (i, 0)),
                  _const_spec((1, d)), _const_spec((1, d))],
        out_specs=rows,
        out_shape=jax.ShapeDtypeStruct((t, d), F32),
        compiler_params=_params(("parallel",)),
        name="combine_ln",
    )(x2d, y0, y1, gates, g.reshape(1, -1), b.reshape(1, -1))


def moe_ffn_ln(x2d, router, wg, wu, wd, layer, g, b, alpha):
    t, d = x2d.shape
    tm = MOE_TILE
    logits = router_logits(x2d, router).T
    top_val, top_idx = lax.top_k(logits, TOP_K)
    gates = jax.nn.softmax(top_val, axis=-1)
    member = (top_idx[:, :, None] == jnp.arange(N_EXPERTS)[None, None, :]).any(axis=1)
    counts = member.sum(axis=0).astype(jnp.int32)
    rank = jnp.cumsum(member.astype(jnp.int32), axis=0) - member.astype(jnp.int32)
    padded = (counts + tm - 1) // tm * tm
    pad_ends = jnp.cumsum(padded)
    pad_starts = pad_ends - padded
    pos = jnp.take_along_axis(pad_starts[None, :] + rank, top_idx, axis=1)
    rows = t * TOP_K + N_EXPERTS * tm
    n_blk = rows // tm
    order = jnp.argsort(top_idx.reshape(-1), stable=True).astype(jnp.int32)
    tok_sorted = order // TOP_K
    starts = jnp.cumsum(counts) - counts
    r = jnp.arange(rows, dtype=jnp.int32)
    row_e = jnp.minimum(jnp.searchsorted(pad_ends, r, side='right'), N_EXPERTS - 1).astype(jnp.int32)
    within = r - pad_starts[row_e]
    src = jnp.where(within < counts[row_e], tok_sorted[jnp.minimum(starts[row_e] + within, t * TOP_K - 1)], 0)
    xs = x2d[src]
    blk_e = row_e[::tm]
    n_used = (pad_ends[-1:] // tm).astype(jnp.int32)
    ys = expert_ffn(xs, blk_e, n_used, wg, wu, wd, layer)
    return combine_ln(x2d, ys[pos[:, 0]], ys[pos[:, 1]], gates, g, b, alpha)


def kernel(x, w_in, w_out, rwkv_mu, rwkv_w0, rwkv_w_up, rwkv_a0, rwkv_a_up, rwkv_g_up, rwkv_k_k, rwkv_k_a,
           rwkv_r_k, rwkv_ln_g, rwkv_ln_b, ret_gn_g, ret_gn_b, rel_bias, ln_g, ln_b, ffn_w_gate, ffn_w_up,
           ffn_w_down, moe_router, moe_w_gate, moe_w_up, moe_w_down):
    batch, seq, d = x.shape
    depth = w_in.shape[0]
    alpha = (2 * depth) ** 0.25
    h = x.reshape(batch * seq, d)
    w_in, w_out = to_bf16(w_in), to_bf16(w_out)
    ffn_w_gate, ffn_w_up, ffn_w_down = to_bf16(ffn_w_gate), to_bf16(ffn_w_up), to_bf16(ffn_w_down)
    moe_w_gate, moe_w_up, moe_w_down = to_bf16(moe_w_gate), to_bf16(moe_w_up), to_bf16(moe_w_down)
    for layer in range(depth):
        p = in_projection(h, w_in, layer)
        ya = rwkv_time_mix(p, batch, seq, rwkv_mu[layer], rwkv_w0[layer], rwkv_w_up[layer], rwkv_a0[layer],
                           rwkv_a_up[layer], rwkv_g_up[layer], rwkv_k_k[layer], rwkv_k_a[layer],
                           rwkv_r_k[layer], rwkv_ln_g[layer], rwkv_ln_b[layer])
        yb = dilated_attention(p, batch, seq, rel_bias)
        yc = retention(p, batch, seq, ret_gn_g[layer], ret_gn_b[layer])
        h = out_projection_ln(ya, yb, yc, h, w_out, layer, ln_g[layer, 0], ln_b[layer, 0], alpha)
        j = layer // 2
        if layer % 2 == 0:
            h = dense_ffn_ln(h, ffn_w_gate, ffn_w_up, ffn_w_down, j, ln_g[layer, 1], ln_b[layer, 1], alpha)
        else:
            h = moe_ffn_ln(h, moe_router[j], moe_w_gate, moe_w_up, moe_w_down, j,
                           ln_g[layer, 1], ln_b[layer, 1], alpha)
    return h.reshape(batch, seq, d)
```

```python
import functools
import math

import numpy as np
import jax
import jax.numpy as jnp
from jax import lax
from jax.experimental import pallas as pl
from jax.experimental.pallas import tpu as pltpu

F32 = jnp.float32
BF16 = jnp.bfloat16
HI = lax.Precision.HIGHEST

HEAD_DIM = 64
RWKV_HEADS = 4
ATTN_HEADS = 8
RET_HEADS = 4
RWKV_DIM = RWKV_HEADS * HEAD_DIM
ATTN_DIM = ATTN_HEADS * HEAD_DIM
RET_DIM = RET_HEADS * HEAD_DIM
DECAY_LORA = 64
ICL_LORA = 64
GATE_LORA = 128
RWKV_IN = 3 * RWKV_DIM + DECAY_LORA + ICL_LORA + GATE_LORA
ATTN_IN = 3 * ATTN_DIM
RET_IN = 4 * RET_DIM
RWKV_GN_EPS = 64e-5
DECAY_SCALE = math.exp(-0.5)
DILATED_PATTERNS = ((128, 1), (512, 4), (2048, 16))
NUM_BUCKETS = 32
MAX_DISTANCE = 2048
ROPE_BASE = 10000.0
N_EXPERTS = 8
TOP_K = 2
LN_EPS = 1e-5

LANES = 128
WKV_CHUNK = 64
WKV_BLOCK = 256
ATTN_W = 128
ATTN_UNROLL = 4
RET_CHUNK = 128
RET_BLOCK = 1024
RET_UNROLL = 2
ROW_TILE = 512
MOE_TILE = 512
MOE_F_BLOCK = 1792
FFN_SUBCHUNK = 256
MASK_VALUE = -1e30
VMEM_LIMIT = 56 * 1024 * 1024


def _dot(a, b, prec=None):
    return jnp.dot(a, b, preferred_element_type=F32, precision=prec)


def _dot_nt(a, b, prec=None):
    return lax.dot_general(a, b, (((1,), (1,)), ((), ())), preferred_element_type=F32, precision=prec)


def _dot_tn(a, b, prec=None):
    return lax.dot_general(a, b, (((0,), (0,)), ((), ())), preferred_element_type=F32, precision=prec)


_DIMS = {"nn": (((1,), (0,)), ((), ())), "nt": (((1,), (1,)), ((), ())), "tn": (((0,), (0,)), ((), ()))}


def _split(x, terms):
    parts = []
    for _ in range(terms - 1):
        hi = x.astype(BF16)
        parts.append(hi)
        x = x - hi.astype(F32)
    parts.append(x.astype(BF16))
    return parts


def _mm(a, b, kind="nn", passes=3):
    dg = lambda p, q: lax.dot_general(p, q, _DIMS[kind], preferred_element_type=F32)
    if passes == 1:
        return dg(a.astype(BF16), b.astype(BF16))
    ah, al = _split(a, 2)
    bh, bl = _split(b, 2)
    return dg(ah, bh) + (dg(al, bh) + dg(ah, bl))


def _mm_ones(x, ones_bf16, ones_first=False, terms=3):
    parts = _split(x, terms)
    m, n = x.shape
    if ones_first:
        full = _dot(ones_bf16, jnp.concatenate(parts, axis=1))
        out = [full[:, i * n:(i + 1) * n] for i in range(terms)]
    else:
        full = _dot(jnp.concatenate(parts, axis=0), ones_bf16)
        out = [full[i * m:(i + 1) * m] for i in range(terms)]
    acc = out[-1]
    for o in reversed(out[:-1]):
        acc = acc + o
    return acc


def _sigmoid(x):
    return 1.0 / (1.0 + jnp.exp(-x))


def _layer_norm(z, g, b):
    mu = jnp.mean(z, axis=-1, keepdims=True)
    d = z - mu
    var = jnp.mean(d * d, axis=-1, keepdims=True)
    return d * lax.rsqrt(var + LN_EPS) * g + b


def _params(sem, vmem=VMEM_LIMIT):
    return pltpu.CompilerParams(dimension_semantics=sem, vmem_limit_bytes=vmem)


def _const_spec(shape):
    nd = len(shape)
    return pl.BlockSpec(shape, lambda *_: (0,) * nd)


def _layer_spec(stacked, layer):
    nd = stacked.ndim - 1
    return pl.BlockSpec((None,) + stacked.shape[1:], lambda *_: (layer,) + (0,) * nd)


CAST_BLOCK_BYTES = 4 * 1024 * 1024


def _cast_kernel(x_ref, o_ref):
    o_ref[...] = x_ref[...].astype(o_ref.dtype)


def layer_to_bf16(w, layer):
    shape = w.shape
    cols = shape[-1]
    rows = w[0].size // cols
    w3 = w.reshape(shape[0], rows, cols)
    tr = 1 << int(math.log2(max(16, min(rows, CAST_BLOCK_BYTES // (4 * cols)))))
    while rows % tr:
        tr //= 2
    assert tr % 16 == 0
    out = pl.pallas_call(
        _cast_kernel,
        grid=(rows // tr,),
        in_specs=[pl.BlockSpec((None, tr, cols), lambda i: (layer, i, 0))],
        out_specs=pl.BlockSpec((tr, cols), lambda i: (i, 0)),
        out_shape=jax.ShapeDtypeStruct((rows, cols), BF16),
        compiler_params=_params(("parallel",)),
        name="layer_to_bf16",
    )(w3)
    return out.reshape((1,) + shape[1:])


def _inproj_kernel(x_ref, w_ref, o_ref, *, n_chunk):
    xb = x_ref[...].astype(BF16)
    for n0 in range(0, o_ref.shape[1], n_chunk):
        o_ref[:, n0:n0 + n_chunk] = _dot(xb, w_ref[:, n0:n0 + n_chunk])


def in_projection(x2d, w_bf16, layer):
    t, d = x2d.shape
    n = w_bf16.shape[2]
    tm = min(ROW_TILE, t)
    return pl.pallas_call(
        functools.partial(_inproj_kernel, n_chunk=512),
        grid=(t // tm,),
        in_specs=[pl.BlockSpec((tm, d), lambda i: (i, 0)), _layer_spec(w_bf16, layer)],
        out_specs=pl.BlockSpec((tm, n), lambda i: (i, 0)),
        out_shape=jax.ShapeDtypeStruct((t, n), F32),
        compiler_params=_params(("parallel",)),
        name="in_projection",
    )(x2d, w_bf16)


def _rwkv_kernel(p_ref, mu_ref, w0_ref, wup_ref, a0_ref, aup_ref, gup_ref, kk_ref, ka_ref, rk_ref,
                 lng_ref, lnb_ref, ltri_ref, same_ref, hsum_ref, o_ref,
                 state_s, prev_s, kt_s, rt_s, bt_s, kn_s, v_s, btg_s, kng_s, etot_s, y_s, rp_s, y0_s, gt_s, zt_s):
    c = WKV_CHUNK
    tb = p_ref.shape[0]
    d = RWKV_DIM
    assert c == HEAD_DIM

    @pl.when(pl.program_id(1) == 0)
    def _():
        state_s[...] = jnp.zeros_like(state_s)
        prev_s[...] = jnp.zeros_like(prev_s)

    p = p_ref[...]
    row = lax.broadcasted_iota(jnp.int32, p.shape, 0)
    shifted = jnp.where(row == 0, prev_s[...], pltpu.roll(p, 1, axis=0))
    prev_s[...] = p[tb - 1:tb, :]
    ps = p + (shifted - p) * mu_ref[...]
    r = ps[:, 0:d]
    k = ps[:, d:2 * d]
    v = ps[:, 2 * d:3 * d]
    xw = ps[:, 3 * d:3 * d + DECAY_LORA]
    xa = ps[:, 3 * d + DECAY_LORA:3 * d + DECAY_LORA + ICL_LORA]
    xg = ps[:, 3 * d + DECAY_LORA + ICL_LORA:]

    hsum = hsum_ref[...]
    logw = -DECAY_SCALE * _sigmoid(w0_ref[...] + _mm(jnp.tanh(xw), wup_ref[...]))
    a = _sigmoid(a0_ref[...] + _mm(xa, aup_ref[...]))
    g = _dot(_sigmoid(xg).astype(BF16), gup_ref[...].astype(BF16))
    kap = k * kk_ref[...]
    kap = kap / jnp.maximum(jnp.sqrt(_mm_ones(kap * kap, hsum)), 1e-12)
    kn = k * (1.0 + (a - 1.0) * ka_ref[...])
    cum = _mm_ones(logw, ltri_ref[...], ones_first=True)
    tot = _mm_ones(logw, same_ref[...], ones_first=True)
    e_neg = jnp.exp(-cum)
    e_rem = jnp.exp(tot - cum)
    nb = -(a * kap)
    kt_s[...] = kap * jnp.exp(cum - logw)
    rt_s[...] = r * jnp.exp(cum)
    bt_s[...] = nb * e_neg
    kn_s[...] = kn * e_neg
    btg_s[...] = nb * e_rem
    kng_s[...] = kn * e_rem
    etot_s[...] = jnp.exp(tot)
    v_s[...] = v

    nh = RWKV_HEADS
    ri = lax.broadcasted_iota(jnp.int32, (c, d), 0)
    ci = lax.broadcasted_iota(jnp.int32, (c, d), 1) % HEAD_DIM
    strict = ci < ri
    incl = ci <= ri
    diag = ci == ri
    eye = diag.astype(F32)
    bi = lax.broadcasted_iota(jnp.int32, (d, d), 0) // HEAD_DIM
    bj = lax.broadcasted_iota(jnp.int32, (d, d), 1) // HEAD_DIM
    blocks = bi == bj
    zero16 = jnp.zeros((), BF16)

    def expand(x16):
        return jnp.where(blocks, jnp.concatenate([x16] * nh, axis=0), zero16)

    def bdmm(a, y, kind="nn", passes=3):
        dg = lambda p_, q_: lax.dot_general(p_, q_, _DIMS[kind], preferred_element_type=F32)
        if passes == 1:
            return dg(a.astype(BF16), expand(y.astype(BF16)))
        ah, al = _split(a, 2)
        yh, yl = _split(y, 2)
        m = a.shape[0]
        both = dg(jnp.concatenate([ah, al], axis=0), expand(yh))
        return both[:m] + (both[m:] + dg(ah, expand(yl)))

    def block_diagonal_of(full):
        outs = []
        for n0 in range(0, full.shape[1], d):
            m = jnp.where(blocks, full[:, n0:n0 + d], 0.0)
            acc = m[0:c]
            for h in range(1, nh):
                acc = acc + m[h * c:(h + 1) * c]
            outs.append(acc)
        return outs

    nchunk = tb // c
    chunks = range(nchunk)
    cat0 = lambda x, y: jnp.concatenate([x, y], axis=0)
    cat1 = lambda x, y: jnp.concatenate([x, y], axis=1)
    levels = int(math.log2(c)) - 1
    get = lambda ref: [ref[j * c:(j + 1) * c, :] for j in chunks]

    kt, rt, vv, btg = get(kt_s), get(rt_s), get(v_s), get(btg_s)
    lhs = [cat0(k_, r_) for k_, r_ in zip(kt, rt)]
    a_b = [bdmm(l_, b_, "nt") for l_, b_ in zip(lhs, get(bt_s))]
    a_k = [bdmm(l_, n_, "nt") for l_, n_ in zip(lhs, get(kn_s))]
    a_ab = [jnp.where(strict, m[:c], 0.0) for m in a_b]
    a_rb = [jnp.where(incl, m[c:], 0.0) for m in a_b]
    a_kr = [cat0(jnp.where(strict, m[:c], 0.0), jnp.where(incl, m[c:], 0.0)) for m in a_k]
    inv = [eye + m for m in a_ab]
    pw = [bdmm(m, m, passes=1) for m in a_ab]
    for lvl in range(levels):
        if lvl < levels - 1:
            both = [bdmm(cat0(x_, p_), p_, passes=1) for x_, p_ in zip(inv, pw)]
            inv = [x_ + b_[:c] for x_, b_ in zip(inv, both)]
            pw = [b_[c:] for b_ in both]
        else:
            inv = [x_ + bdmm(x_, p_, passes=1) for x_, p_ in zip(inv, pw)]
    av = [bdmm(m, v_) for m, v_ in zip(a_kr, vv)]
    wmat = [bdmm(x_, k_, passes=1) for x_, k_ in zip(inv, kt)]
    umat = [bdmm(x_, a_[:c], passes=1) for x_, a_ in zip(inv, av)]
    rw = [bdmm(m, w_, passes=1) for m, w_ in zip(a_rb, wmat)]
    ru = [bdmm(m, u_, passes=1) for m, u_ in zip(a_rb, umat)]
    gz = [block_diagonal_of(_mm(b_, cat1(w_, u_), "tn", passes=1))
          for b_, w_, u_ in zip(btg, wmat, umat)]
    kv = [block_diagonal_of(_mm(n_, v_, "tn"))[0] for n_, v_ in zip(get(kng_s), vv)]
    for j in chunks:
        rows = slice(j * c, (j + 1) * c)
        rp_s[rows, :] = rt[j] + rw[j]
        y0_s[rows, :] = ru[j] + av[j][c:]
        g_diag = jnp.where(diag, jnp.broadcast_to(etot_s[j * c:j * c + 1, :], (c, d)), 0.0)
        gt_s[rows, :] = g_diag + gz[j][0]
        zt_s[rows, :] = gz[j][1] + kv[j]

    state = state_s[...]
    for j in chunks:
        rows = slice(j * c, (j + 1) * c)
        ry = bdmm(cat0(rp_s[rows, :], gt_s[rows, :]), state)
        y_s[rows, :] = ry[:c] + y0_s[rows, :]
        state = ry[c:] + zt_s[rows, :]
    state_s[...] = state

    y = y_s[...]
    mean = _mm_ones(y, hsum) * (1.0 / HEAD_DIM)
    dy = y - mean
    var = _mm_ones(dy * dy, hsum) * (1.0 / HEAD_DIM)
    yn = dy * lax.rsqrt(var + RWKV_GN_EPS) * lng_ref[...] + lnb_ref[...]
    bonus = _mm_ones(r * kn * rk_ref[...], hsum) * v
    o_ref[...] = ((yn + bonus) * g).astype(o_ref.dtype)


def _chunk_masks(tb, c):
    i = np.arange(tb)
    same = (i[:, None] // c) == (i[None, :] // c)
    ltri = same & (i[None, :] <= i[:, None])
    return jnp.asarray(ltri, BF16), jnp.asarray(same, BF16)


def _head_sum_matrix(width):
    i = np.arange(width)
    return jnp.asarray((i[:, None] // HEAD_DIM) == (i[None, :] // HEAD_DIM), BF16)


def rwkv_time_mix(p, batch, seq, mu, w0, w_up, a0, a_up, g_up, k_k, k_a, r_k, ln_g, ln_b):
    t = batch * seq
    tb = min(WKV_BLOCK, seq)
    nblk = seq // tb
    ltri, same = _chunk_masks(tb, WKV_CHUNK)
    hsum = _head_sum_matrix(RWKV_DIM)
    row = lambda a: a.reshape(1, -1)
    consts = [row(mu), row(w0), w_up, row(a0), a_up, g_up, row(k_k), row(k_a), row(r_k), row(ln_g), row(ln_b),
              ltri, same, hsum]
    buf = lambda: pltpu.VMEM((tb, RWKV_DIM), F32)
    return pl.pallas_call(
        _rwkv_kernel,
        grid=(batch, nblk),
        in_specs=[pl.BlockSpec((tb, RWKV_IN), lambda b, j: (b * nblk + j, 0))]
                 + [_const_spec(a.shape) for a in consts],
        out_specs=pl.BlockSpec((tb, RWKV_DIM), lambda b, j: (b * nblk + j, 0)),
        out_shape=jax.ShapeDtypeStruct((t, RWKV_DIM), BF16),
        scratch_shapes=[pltpu.VMEM((HEAD_DIM, RWKV_DIM), F32), pltpu.VMEM((1, RWKV_IN), F32)]
                       + [buf() for _ in range(13)],
        compiler_params=_params(("parallel", "arbitrary")),
        name="rwkv_time_mix",
    )(p, *consts)


def _attn_kernel(q_ref, k_ref, v_ref, bias_ref, o_ref, acc_s, m_s, l_s):
    seq = q_ref.shape[0]
    w = ATTN_W
    scale = HEAD_DIM ** -0.5

    def rows_of(start, dil):
        return pl.ds(start, w) if dil == 1 else pl.ds(start, w, stride=dil)

    lane = lax.broadcasted_iota(jnp.int32, (w, LANES), 1)
    head0 = lane < HEAD_DIM
    zero = jnp.zeros((), BF16)
    one = jnp.ones((), BF16)

    def group(pi, dil, g, firsts):
        rows_l, q_l, k_l, v_l = [], [], [], []
        for u, first in enumerate(firsts):
            b = g * len(firsts) + u
            start = (b % dil) + (b // dil) * (dil * w)
            rows = rows_of(start, dil)
            q = (q_ref[rows, :] * scale).astype(BF16)
            kk = k_ref[rows, :].astype(BF16)
            vv = v_ref[rows, :].astype(BF16)
            if not first:
                prev = rows_of(start - dil * w, dil)
                kk = jnp.concatenate([k_ref[prev, :].astype(BF16), kk], axis=0)
                vv = jnp.concatenate([v_ref[prev, :].astype(BF16), vv], axis=0)
            rows_l.append(rows)
            q_l.append(q)
            k_l.append(kk)
            v_l.append(vv)
        s = [[_dot_nt(jnp.where(head0 if h == 0 else ~head0, q, zero), kk)
              + (bias_ref[pi, h, :, w:] if first else bias_ref[pi, h])
              for h in range(2)] for q, kk, first in zip(q_l, k_l, firsts)]
        m = [[jnp.max(sh, axis=-1, keepdims=True) for sh in su] for su in s]
        pr = [[jnp.exp(sh - mh).astype(BF16) for sh, mh in zip(su, mu)] for su, mu in zip(s, m)]
        kmask = lambda vv: lax.broadcasted_iota(jnp.int32, vv.shape, 1) < HEAD_DIM
        res = [[_dot(pu[0], jnp.where(kmask(vv), vv, one)), _dot(pu[1], jnp.where(kmask(vv), one, vv))]
               for pu, vv in zip(pr, v_l)]
        for rows, ru, mu in zip(rows_l, res, m):
            acc_s[pi, rows, :] = jnp.where(head0, ru[0], ru[1])
            l_s[pi, rows, :] = jnp.where(head0, ru[1], ru[0])
            m_s[pi, rows, :] = jnp.where(head0, mu[0], mu[1])

    n_groups = (seq // w) // ATTN_UNROLL
    for pi, (window, dil) in enumerate(DILATED_PATTERNS):
        flags = [tuple((g * ATTN_UNROLL + u) < dil for u in range(ATTN_UNROLL)) for g in range(n_groups)]
        g0 = 0
        while g0 < n_groups:
            g1 = g0
            while g1 < n_groups and flags[g1] == flags[g0]:
                g1 += 1
            if g1 - g0 == 1:
                group(pi, dil, g0, flags[g0])
            else:
                def body(g, carry, pi=pi, dil=dil, firsts=flags[g0]):
                    group(pi, dil, g, firsts)
                    return carry
                lax.fori_loop(g0, g1, body, 0)
            g0 = g1

    mt = 256

    def merge_body(i, carry):
        rows = pl.ds(pl.multiple_of(i * mt, mt), mt)
        m0, m1, m2 = m_s[0, rows, :], m_s[1, rows, :], m_s[2, rows, :]
        mx = jnp.maximum(jnp.maximum(m0, m1), m2)
        w0, w1, w2 = jnp.exp(m0 - mx), jnp.exp(m1 - mx), jnp.exp(m2 - mx)
        num = w0 * acc_s[0, rows, :] + w1 * acc_s[1, rows, :] + w2 * acc_s[2, rows, :]
        swap = lambda x: pltpu.roll(x, HEAD_DIM, axis=1)
        den = w0 * swap(l_s[0, rows, :]) + w1 * swap(l_s[1, rows, :]) + w2 * swap(l_s[2, rows, :])
        o_ref[rows, :] = (num / den).astype(o_ref.dtype)
        return carry

    lax.fori_loop(0, seq // mt, merge_body, 0)


def _t5_bucket(dist):
    max_exact = NUM_BUCKETS // 2
    large = max_exact + (np.log(np.maximum(dist, max_exact) / max_exact)
                         / math.log(MAX_DISTANCE / max_exact) * (NUM_BUCKETS - max_exact)).astype(np.int32)
    return np.where(dist < max_exact, dist, np.minimum(large, NUM_BUCKETS - 1)).astype(np.int32)


def _attn_bias(rel_bias):
    w = ATTN_W
    i = np.arange(w)[:, None]
    j = np.arange(2 * w)[None, :]
    rel = i + w - j
    band = (rel >= 0) & (rel <= w)
    tabs = []
    for window, dil in DILATED_PATTERNS:
        bucket = _t5_bucket(np.clip(rel, 0, None) * dil)
        onehot = jnp.asarray(bucket[..., None] == np.arange(NUM_BUCKETS), F32)
        bias = jnp.einsum('ijb,bh->hij', onehot, rel_bias.astype(F32), precision=HI)
        tabs.append(jnp.where(band[None], bias, MASK_VALUE))
    return jnp.stack(tabs)


def dilated_attention(p, batch, seq, rel_bias):
    t = batch * seq
    bias = _attn_bias(rel_bias)
    col0 = RWKV_IN // LANES
    npair = ATTN_DIM // LANES
    spec = lambda off: pl.BlockSpec((seq, LANES), lambda b, hp: (b, col0 + off + hp))
    return pl.pallas_call(
        _attn_kernel,
        grid=(batch, npair),
        in_specs=[spec(0), spec(npair), spec(2 * npair),
                  pl.BlockSpec((3, 2, ATTN_W, 2 * ATTN_W), lambda b, hp: (0, hp, 0, 0))],
        out_specs=pl.BlockSpec((seq, LANES), lambda b, hp: (b, hp)),
        out_shape=jax.ShapeDtypeStruct((t, ATTN_DIM), BF16),
        scratch_shapes=[pltpu.VMEM((3, seq, LANES), F32) for _ in range(3)],
        compiler_params=_params(("parallel", "parallel")),
        name="dilated_attention",
    )(p, p, p, bias)


def _ret_kernel(q_ref, k_ref, v_ref, g_ref, cos_ref, sin_ref, dmat_ref, xi_ref, zeta_ref, gng_ref, gnb_ref,
                hsum_ref, o_ref, state_s):
    c = RET_CHUNK
    tb = q_ref.shape[0]

    @pl.when(pl.program_id(1) == 0)
    def _():
        state_s[...] = jnp.zeros_like(state_s)

    lane = lax.broadcasted_iota(jnp.int32, (c, RET_DIM), 1)
    first_half = (lane % HEAD_DIM) < (HEAD_DIM // 2)

    def rotate(x, cos, sin):
        swapped = jnp.where(first_half, pltpu.roll(x, RET_DIM - HEAD_DIM // 2, axis=1),
                            pltpu.roll(x, HEAD_DIM // 2, axis=1))
        return x * cos + swapped * sin

    heads = range(RET_HEADS)
    hsl = [slice(h * HEAD_DIM, (h + 1) * HEAD_DIM) for h in heads]
    chunk_decay = [(1.0 - 2.0 ** (-5.0 - h)) ** c for h in heads]

    def group_body(g, carry):
        rows_l, qb, kb, qx, kz, vb = [], [], [], [], [], []
        for u in range(RET_UNROLL):
            rows = pl.ds(pl.multiple_of((g * RET_UNROLL + u) * c, c), c)
            cos, sin = cos_ref[rows, :], sin_ref[rows, :]
            q = rotate(q_ref[rows, :], cos, sin)
            k = rotate(k_ref[rows, :], cos, sin) * (HEAD_DIM ** -0.5)
            rows_l.append(rows)
            qb.append(q.astype(BF16))
            kb.append(k.astype(BF16))
            qx.append((q * xi_ref[...]).astype(BF16))
            kz.append((k * zeta_ref[...]).astype(BF16))
            vb.append(v_ref[rows, :].astype(BF16))
        sc = [[(_dot_nt(qb[u][:, sl], kb[u][:, sl]) * dmat_ref[h]).astype(BF16) for h, sl in zip(heads, hsl)]
              for u in range(RET_UNROLL)]
        intra = [[_dot(sc[u][h], vb[u][:, hsl[h]]) for h in heads] for u in range(RET_UNROLL)]
        kv = [[_dot_tn(kz[u][:, sl], vb[u][:, sl]) for sl in hsl] for u in range(RET_UNROLL)]
        states = [state_s[h] for h in heads]
        ys = []
        for u in range(RET_UNROLL):
            ys.append(jnp.concatenate(
                [intra[u][h] + _dot(qx[u][:, hsl[h]], states[h].astype(BF16)) for h in heads], axis=1))
            states = [states[h] * chunk_decay[h] + kv[u][h] for h in heads]
        for h in heads:
            state_s[h] = states[h]
        hsum = hsum_ref[...]
        for rows, y in zip(rows_l, ys):
            mean = _mm_ones(y, hsum) * (1.0 / HEAD_DIM)
            dy = y - mean
            var = _mm_ones(dy * dy, hsum) * (1.0 / HEAD_DIM)
            yn = dy * lax.rsqrt(var + LN_EPS) * gng_ref[...] + gnb_ref[...]
            gate = g_ref[rows, :]
            o_ref[rows, :] = (gate * _sigmoid(gate) * yn).astype(o_ref.dtype)
        return carry

    lax.fori_loop(0, tb // (c * RET_UNROLL), group_body, 0)


def _ret_tables(seq):
    c = RET_CHUNK
    half = HEAD_DIM // 2
    inv = ROPE_BASE ** (-jnp.arange(half, dtype=F32) / half)
    ang = jnp.arange(seq, dtype=F32)[:, None] * inv
    cos, sin = jnp.cos(ang), jnp.sin(ang)
    cos_t = jnp.tile(jnp.concatenate([cos, cos], axis=1), (1, RET_HEADS))
    sin_t = jnp.tile(jnp.concatenate([-sin, sin], axis=1), (1, RET_HEADS))
    log_g = jnp.log1p(-jnp.exp2(-5.0 - jnp.arange(RET_HEADS, dtype=F32)))
    n = jnp.arange(c, dtype=F32)
    diff = n[:, None] - n[None, :]
    dmat = jnp.where(diff >= 0, jnp.exp(log_g[:, None, None] * jnp.maximum(diff, 0.0)), 0.0)
    zeta = jnp.exp(log_g[:, None] * (c - 1 - n))
    xi = jnp.exp(log_g[:, None] * (n + 1))
    widen = lambda tab: jnp.repeat(tab.T, HEAD_DIM, axis=1)
    return cos_t, sin_t, dmat, widen(xi), widen(zeta)


def retention(p, batch, seq, gn_g, gn_b):
    t = batch * seq
    tb = min(RET_BLOCK, seq)
    nblk = seq // tb
    cos_t, sin_t, dmat, xi, zeta = _ret_tables(seq)
    hsum = _head_sum_matrix(RET_DIM)
    col0 = (RWKV_IN + ATTN_IN) // RET_DIM
    spec = lambda off: pl.BlockSpec((tb, RET_DIM), lambda b, j: (b * nblk + j, col0 + off))
    tab = pl.BlockSpec((tb, RET_DIM), lambda b, j: (j, 0))
    consts = [dmat, xi, zeta, gn_g.reshape(1, -1), gn_b.reshape(1, -1), hsum]
    return pl.pallas_call(
        _ret_kernel,
        grid=(batch, nblk),
        in_specs=[spec(0), spec(1), spec(2), spec(3), tab, tab] + [_const_spec(a.shape) for a in consts],
        out_specs=pl.BlockSpec((tb, RET_DIM), lambda b, j: (b * nblk + j, 0)),
        out_shape=jax.ShapeDtypeStruct((t, RET_DIM), BF16),
        scratch_shapes=[pltpu.VMEM((RET_HEADS, HEAD_DIM, HEAD_DIM), F32)],
        compiler_params=_params(("parallel", "arbitrary")),
        name="retention",
    )(p, p, p, p, cos_t, sin_t, *consts)


def _outproj_kernel(ya_ref, yb_ref, yc_ref, x_ref, w_ref, g_ref, b_ref, o_ref, *, alpha):
    acc = _dot(ya_ref[...], w_ref[0:RWKV_DIM, :])
    acc += _dot(yb_ref[...], w_ref[RWKV_DIM:RWKV_DIM + ATTN_DIM, :])
    acc += _dot(yc_ref[...], w_ref[RWKV_DIM + ATTN_DIM:, :])
    o_ref[...] = _layer_norm(alpha * x_ref[...] + acc, g_ref[...], b_ref[...])


def out_projection_ln(ya, yb, yc, x2d, w_bf16, layer, g, b, alpha):
    t, d = x2d.shape
    tm = min(ROW_TILE, t)
    rows = lambda width: pl.BlockSpec((tm, width), lambda i: (i, 0))
    return pl.pallas_call(
        functools.partial(_outproj_kernel, alpha=alpha),
        grid=(t // tm,),
        in_specs=[rows(RWKV_DIM), rows(ATTN_DIM), rows(RET_DIM), rows(d), _layer_spec(w_bf16, layer),
                  _const_spec((1, d)), _const_spec((1, d))],
        out_specs=rows(d),
        out_shape=jax.ShapeDtypeStruct((t, d), F32),
        compiler_params=_params(("parallel",)),
        name="out_projection_ln",
    )(ya, yb, yc, x2d, w_bf16, g.reshape(1, -1), b.reshape(1, -1))


def _swiglu(xb, wg, wu, wd, acc, f_chunk):
    ff = wg.shape[-1]
    starts = list(range(0, ff, f_chunk))
    gate_up = lambda f0: (_dot(xb, wg[:, f0:f0 + f_chunk]), _dot(xb, wu[:, f0:f0 + f_chunk]))
    nxt = gate_up(starts[0])
    for n, f0 in enumerate(starts):
        gate, up = nxt
        if n + 1 < len(starts):
            nxt = gate_up(starts[n + 1])
        hid = (gate * _sigmoid(gate) * up).astype(BF16)
        part = _dot(hid, wd[f0:f0 + f_chunk, :])
        acc = part if acc is None else acc + part
    return acc


def _ffn_kernel(x_ref, wg_ref, wu_ref, wd_ref, g_ref, b_ref, o_ref, *, alpha, f_chunk):
    x = x_ref[...]
    acc = _swiglu(x.astype(BF16), wg_ref, wu_ref, wd_ref, alpha * x, f_chunk)
    o_ref[...] = _layer_norm(acc, g_ref[...], b_ref[...])


def dense_ffn_ln(x2d, wg, wu, wd, layer, g, b, alpha):
    t, d = x2d.shape
    tm = min(ROW_TILE, t)
    return pl.pallas_call(
        functools.partial(_ffn_kernel, alpha=alpha, f_chunk=FFN_SUBCHUNK),
        grid=(t // tm,),
        in_specs=[pl.BlockSpec((tm, d), lambda i: (i, 0)), _layer_spec(wg, layer), _layer_spec(wu, layer),
                  _layer_spec(wd, layer), _const_spec((1, d)), _const_spec((1, d))],
        out_specs=pl.BlockSpec((tm, d), lambda i: (i, 0)),
        out_shape=jax.ShapeDtypeStruct((t, d), F32),
        compiler_params=_params(("parallel",)),
        name="dense_ffn_ln",
    )(x2d, wg, wu, wd, g.reshape(1, -1), b.reshape(1, -1))


def _router_kernel(x_ref, w_ref, o_ref):
    o_ref[...] = _dot_nt(w_ref[...].astype(BF16), x_ref[...].astype(BF16))


def router_logits(x2d, router):
    t, d = x2d.shape
    tm = min(ROW_TILE, t)
    wt = router.T
    return pl.pallas_call(
        _router_kernel,
        grid=(t // tm,),
        in_specs=[pl.BlockSpec((tm, d), lambda i: (i, 0)), _const_spec(wt.shape)],
        out_specs=pl.BlockSpec((N_EXPERTS, tm), lambda i: (0, i)),
        out_shape=jax.ShapeDtypeStruct((N_EXPERTS, t), F32),
        compiler_params=_params(("parallel",)),
        name="router_logits",
    )(x2d, wt)


def _expert_kernel(blk_e_ref, used_ref, x_ref, wg_ref, wu_ref, wd_ref, *rest, blk_off):
    o_ref = rest[-1]
    i, j = pl.program_id(0) + blk_off, pl.program_id(1)

    @pl.when(i < used_ref[0])
    def _():
        part = _swiglu(x_ref[...].astype(BF16), wg_ref.at[0], wu_ref.at[0], wd_ref.at[0], None, FFN_SUBCHUNK)

        @pl.when(j == 0)
        def _():
            o_ref[...] = part

        @pl.when(j > 0)
        def _():
            o_ref[...] += part


def expert_ffn(xs, blk_e, n_used, wg, wu, wd, layer, blk_off, out_rows, earlier=None, f_chunk=MOE_F_BLOCK):
    rows, d = xs.shape
    ff = wg.shape[3]
    tm = MOE_TILE
    in_specs = [pl.BlockSpec((tm, d), lambda i, j, be, nu: (i, 0)),
                pl.BlockSpec((None, 1, d, f_chunk), lambda i, j, be, nu: (layer, be[i + blk_off], 0, j)),
                pl.BlockSpec((None, 1, d, f_chunk), lambda i, j, be, nu: (layer, be[i + blk_off], 0, j)),
                pl.BlockSpec((None, 1, f_chunk, d), lambda i, j, be, nu: (layer, be[i + blk_off], j, 0))]
    args = [blk_e, n_used, xs, wg, wu, wd]
    aliases = {}
    if earlier is not None:
        in_specs.append(pl.BlockSpec(memory_space=pl.ANY))
        args.append(earlier)
        aliases = {len(args) - 1: 0}
    grid_spec = pltpu.PrefetchScalarGridSpec(
        num_scalar_prefetch=2,
        grid=(rows // tm, ff // f_chunk),
        in_specs=in_specs,
        out_specs=pl.BlockSpec((tm, d), lambda i, j, be, nu: (i + blk_off, 0)),
    )
    return pl.pallas_call(
        functools.partial(_expert_kernel, blk_off=blk_off),
        grid_spec=grid_spec,
        out_shape=jax.ShapeDtypeStruct((out_rows, d), F32),
        input_output_aliases=aliases,
        compiler_params=_params(("parallel", "arbitrary")),
        name="expert_ffn",
    )(*args)


def _combine_ln_kernel(x_ref, y0_ref, y1_ref, gate_ref, g_ref, b_ref, o_ref, *, alpha):
    gates = gate_ref[...]
    f = y0_ref[...] * gates[:, 0:1] + y1_ref[...] * gates[:, 1:2]
    o_ref[...] = _layer_norm(alpha * x_ref[...] + f, g_ref[...], b_ref[...])


def combine_ln(x2d, y0, y1, gates, g, b, alpha):
    t, d = x2d.shape
    tm = min(ROW_TILE, t)
    rows = pl.BlockSpec((tm, d), lambda i: (i, 0))
    return pl.pallas_call(
        functools.partial(_combine_ln_kernel, alpha=alpha),
        grid=(t // tm,),
        in_specs=[rows, rows, rows, pl.BlockSpec((tm, TOP_K), lambda i: (i, 0)),
                  _const_spec((1, d)), _const_spec((1, d))],
        out_specs=rows,
        out_shape=jax.ShapeDtypeStruct((t, d), F32),
        compiler_params=_params(("parallel",)),
        name="combine_ln",
    )(x2d, y0, y1, gates, g.reshape(1, -1), b.reshape(1, -1))


def moe_ffn_ln(x2d, router, wg, wu, wd, layer, g, b, alpha):
    t, d = x2d.shape
    tm = MOE_TILE
    logits = router_logits(x2d, router).T
    top_val, top_idx = lax.top_k(logits, TOP_K)
    gates = jax.nn.softmax(top_val, axis=-1)
    member = (top_idx[:, :, None] == jnp.arange(N_EXPERTS)[None, None, :]).any(axis=1)
    counts = member.sum(axis=0).astype(jnp.int32)
    rank = jnp.cumsum(member.astype(jnp.int32), axis=0) - member.astype(jnp.int32)
    padded = (counts + tm - 1) // tm * tm
    pad_ends = jnp.cumsum(padded)
    pad_starts = pad_ends - padded
    pos = jnp.take_along_axis(pad_starts[None, :] + rank, top_idx, axis=1)
    rows = t * TOP_K + N_EXPERTS * tm
    n_blk = rows // tm
    order = jnp.argsort(top_idx.reshape(-1), stable=True).astype(jnp.int32)
    tok_sorted = order // TOP_K
    starts = jnp.cumsum(counts) - counts
    r = jnp.arange(rows, dtype=jnp.int32)
    row_e = jnp.minimum(jnp.searchsorted(pad_ends, r, side='right'), N_EXPERTS - 1).astype(jnp.int32)
    within = r - pad_starts[row_e]
    src = jnp.where(within < counts[row_e], tok_sorted[jnp.minimum(starts[row_e] + within, t * TOP_K - 1)], 0)
    blk_e = row_e[::tm]
    n_used = (pad_ends[-1:] // tm).astype(jnp.int32)
    half = (n_blk // 2) * tm
    ys = expert_ffn(x2d[src[:half]], blk_e, n_used, wg, wu, wd, layer, 0, rows)
    ys = expert_ffn(x2d[src[half:]], blk_e, n_used, wg, wu, wd, layer, half // tm, rows, earlier=ys)
    return combine_ln(x2d, ys[pos[:, 0]], ys[pos[:, 1]], gates, g, b, alpha)


def kernel(x, w_in, w_out, rwkv_mu, rwkv_w0, rwkv_w_up, rwkv_a0, rwkv_a_up, rwkv_g_up, rwkv_k_k, rwkv_k_a,
           rwkv_r_k, rwkv_ln_g, rwkv_ln_b, ret_gn_g, ret_gn_b, rel_bias, ln_g, ln_b, ffn_w_gate, ffn_w_up,
           ffn_w_down, moe_router, moe_w_gate, moe_w_up, moe_w_down):
    batch, seq, d = x.shape
    depth = w_in.shape[0]
    alpha = (2 * depth) ** 0.25
    h = x.reshape(batch * seq, d)
    w_in, w_out = w_in.astype(BF16), w_out.astype(BF16)
    ffn_w_gate, ffn_w_up, ffn_w_down = ffn_w_gate.astype(BF16), ffn_w_up.astype(BF16), ffn_w_down.astype(BF16)
    for layer in range(depth):
        p = in_projection(h, w_in, layer)
        ya = rwkv_time_mix(p, batch, seq, rwkv_mu[layer], rwkv_w0[layer], rwkv_w_up[layer], rwkv_a0[layer],
                           rwkv_a_up[layer], rwkv_g_up[layer], rwkv_k_k[layer], rwkv_k_a[layer],
                           rwkv_r_k[layer], rwkv_ln_g[layer], rwkv_ln_b[layer])
        yb = dilated_attention(p, batch, seq, rel_bias)
        yc = retention(p, batch, seq, ret_gn_g[layer], ret_gn_b[layer])
        h = out_projection_ln(ya, yb, yc, h, w_out, layer, ln_g[layer, 0], ln_b[layer, 0], alpha)
        j = layer // 2
        if layer % 2 == 0:
            h = dense_ffn_ln(h, ffn_w_gate, ffn_w_up, ffn_w_down, j, ln_g[layer, 1], ln_b[layer, 1], alpha)
        else:
            h = moe_ffn_ln(h, moe_router[j], layer_to_bf16(moe_w_gate, j), layer_to_bf16(moe_w_up, j),
                           layer_to_bf16(moe_w_down, j), 0, ln_g[layer, 1], ln_b[layer, 1], alpha)
    return h.reshape(batch, seq, d)
```

```python
import functools
import math

import numpy as np
import jax
import jax.numpy as jnp
from jax import lax
from jax.experimental import pallas as pl
from jax.experimental.pallas import tpu as pltpu

F32 = jnp.float32
BF16 = jnp.bfloat16
HI = lax.Precision.HIGHEST

HEAD_DIM = 64
RWKV_HEADS = 4
ATTN_HEADS = 8
RET_HEADS = 4
RWKV_DIM = RWKV_HEADS * HEAD_DIM
ATTN_DIM = ATTN_HEADS * HEAD_DIM
RET_DIM = RET_HEADS * HEAD_DIM
DECAY_LORA = 64
ICL_LORA = 64
GATE_LORA = 128
RWKV_IN = 3 * RWKV_DIM + DECAY_LORA + ICL_LORA + GATE_LORA
ATTN_IN = 3 * ATTN_DIM
RET_IN = 4 * RET_DIM
RWKV_GN_EPS = 64e-5
DECAY_SCALE = math.exp(-0.5)
DILATED_PATTERNS = ((128, 1), (512, 4), (2048, 16))
NUM_BUCKETS = 32
MAX_DISTANCE = 2048
ROPE_BASE = 10000.0
N_EXPERTS = 8
TOP_K = 2
LN_EPS = 1e-5

LANES = 128
WKV_CHUNK = 64
WKV_BLOCK = 256
ATTN_W = 128
ATTN_UNROLL = 4
RET_CHUNK = 128
RET_BLOCK = 1024
RET_UNROLL = 4
ROW_TILE = 512
MOE_TILE = 512
MOE_F_BLOCK = 1792
FFN_SUBCHUNK = 256
MASK_VALUE = -1e30
VMEM_LIMIT = 56 * 1024 * 1024


def _dot(a, b, prec=None):
    return jnp.dot(a, b, preferred_element_type=F32, precision=prec)


def _dot_nt(a, b, prec=None):
    return lax.dot_general(a, b, (((1,), (1,)), ((), ())), preferred_element_type=F32, precision=prec)


def _dot_tn(a, b, prec=None):
    return lax.dot_general(a, b, (((0,), (0,)), ((), ())), preferred_element_type=F32, precision=prec)


_DIMS = {"nn": (((1,), (0,)), ((), ())), "nt": (((1,), (1,)), ((), ())), "tn": (((0,), (0,)), ((), ()))}


def _split(x, terms):
    parts = []
    for _ in range(terms - 1):
        hi = x.astype(BF16)
        parts.append(hi)
        x = x - hi.astype(F32)
    parts.append(x.astype(BF16))
    return parts


def _mm(a, b, kind="nn", passes=3):
    dg = lambda p, q: lax.dot_general(p, q, _DIMS[kind], preferred_element_type=F32)
    if passes == 1:
        return dg(a.astype(BF16), b.astype(BF16))
    ah, al = _split(a, 2)
    bh, bl = _split(b, 2)
    return dg(ah, bh) + (dg(al, bh) + dg(ah, bl))


def _mm_ones(x, ones_bf16, ones_first=False, terms=3):
    parts = _split(x, terms)
    m, n = x.shape
    if ones_first:
        full = _dot(ones_bf16, jnp.concatenate(parts, axis=1))
        out = [full[:, i * n:(i + 1) * n] for i in range(terms)]
    else:
        full = _dot(jnp.concatenate(parts, axis=0), ones_bf16)
        out = [full[i * m:(i + 1) * m] for i in range(terms)]
    acc = out[-1]
    for o in reversed(out[:-1]):
        acc = acc + o
    return acc


def _sigmoid(x):
    return 1.0 / (1.0 + jnp.exp(-x))


def _layer_norm(z, g, b):
    mu = jnp.mean(z, axis=-1, keepdims=True)
    d = z - mu
    var = jnp.mean(d * d, axis=-1, keepdims=True)
    return d * lax.rsqrt(var + LN_EPS) * g + b


def _params(sem, vmem=VMEM_LIMIT):
    return pltpu.CompilerParams(dimension_semantics=sem, vmem_limit_bytes=vmem)


def _const_spec(shape):
    nd = len(shape)
    return pl.BlockSpec(shape, lambda *_: (0,) * nd)


def _layer_spec(stacked, layer):
    nd = stacked.ndim - 1
    return pl.BlockSpec((None,) + stacked.shape[1:], lambda *_: (layer,) + (0,) * nd)


CAST_BLOCK_BYTES = 8 * 1024 * 1024


def _cast_kernel(x_ref, o_ref):
    o_ref[...] = x_ref[...].astype(o_ref.dtype)


def layer_to_bf16(w, layer):
    shape = w.shape
    cols = shape[-1]
    rows = w[0].size // cols
    w3 = w.reshape(shape[0], rows, cols)
    tr = 1 << int(math.log2(max(16, min(rows, CAST_BLOCK_BYTES // (4 * cols)))))
    while rows % tr:
        tr //= 2
    assert tr % 16 == 0
    out = pl.pallas_call(
        _cast_kernel,
        grid=(rows // tr,),
        in_specs=[pl.BlockSpec((None, tr, cols), lambda i: (layer, i, 0))],
        out_specs=pl.BlockSpec((tr, cols), lambda i: (i, 0)),
        out_shape=jax.ShapeDtypeStruct((rows, cols), BF16),
        compiler_params=_params(("parallel",)),
        name="layer_to_bf16",
    )(w3)
    return out.reshape((1,) + shape[1:])


def _inproj_kernel(x_ref, w_ref, o_ref, *, n_chunk):
    xb = x_ref[...].astype(BF16)
    for n0 in range(0, o_ref.shape[1], n_chunk):
        o_ref[:, n0:n0 + n_chunk] = _dot(xb, w_ref[:, n0:n0 + n_chunk])


def in_projection(x2d, w_bf16, layer):
    t, d = x2d.shape
    n = w_bf16.shape[2]
    tm = min(ROW_TILE, t)
    return pl.pallas_call(
        functools.partial(_inproj_kernel, n_chunk=512),
        grid=(t // tm,),
        in_specs=[pl.BlockSpec((tm, d), lambda i: (i, 0)), _layer_spec(w_bf16, layer)],
        out_specs=pl.BlockSpec((tm, n), lambda i: (i, 0)),
        out_shape=jax.ShapeDtypeStruct((t, n), F32),
        compiler_params=_params(("parallel",)),
        name="in_projection",
    )(x2d, w_bf16)


def _rwkv_kernel(p_ref, mu_ref, w0_ref, wup_ref, a0_ref, aup_ref, gup_ref, kk_ref, ka_ref, rk_ref,
                 lng_ref, lnb_ref, ltri_ref, same_ref, hsum_ref, o_ref,
                 state_s, prev_s, kt_s, rt_s, bt_s, kn_s, v_s, btg_s, kng_s, etot_s, y_s, rp_s, y0_s, gt_s, zt_s):
    c = WKV_CHUNK
    tb = p_ref.shape[0]
    d = RWKV_DIM
    assert c == HEAD_DIM

    @pl.when(pl.program_id(1) == 0)
    def _():
        state_s[...] = jnp.zeros_like(state_s)
        prev_s[...] = jnp.zeros_like(prev_s)

    p = p_ref[...]
    row = lax.broadcasted_iota(jnp.int32, p.shape, 0)
    shifted = jnp.where(row == 0, prev_s[...], pltpu.roll(p, 1, axis=0))
    prev_s[...] = p[tb - 1:tb, :]
    ps = p + (shifted - p) * mu_ref[...]
    r = ps[:, 0:d]
    k = ps[:, d:2 * d]
    v = ps[:, 2 * d:3 * d]
    xw = ps[:, 3 * d:3 * d + DECAY_LORA]
    xa = ps[:, 3 * d + DECAY_LORA:3 * d + DECAY_LORA + ICL_LORA]
    xg = ps[:, 3 * d + DECAY_LORA + ICL_LORA:]

    hsum = hsum_ref[...]
    logw = -DECAY_SCALE * _sigmoid(w0_ref[...] + _mm(jnp.tanh(xw), wup_ref[...]))
    a = _sigmoid(a0_ref[...] + _mm(xa, aup_ref[...]))
    g = _dot(_sigmoid(xg).astype(BF16), gup_ref[...].astype(BF16))
    kap = k * kk_ref[...]
    kap = kap / jnp.maximum(jnp.sqrt(_mm_ones(kap * kap, hsum)), 1e-12)
    kn = k * (1.0 + (a - 1.0) * ka_ref[...])
    sums = _mm_ones(logw, jnp.concatenate([ltri_ref[...], same_ref[...]], axis=0), ones_first=True)
    cum = sums[:tb]
    tot = sums[tb:]
    e_neg = jnp.exp(-cum)
    e_rem = jnp.exp(tot - cum)
    nb = -(a * kap)
    kt_s[...] = kap * jnp.exp(cum - logw)
    rt_s[...] = r * jnp.exp(cum)
    bt_s[...] = nb * e_neg
    kn_s[...] = kn * e_neg
    btg_s[...] = nb * e_rem
    kng_s[...] = kn * e_rem
    etot_s[...] = jnp.exp(tot)
    v_s[...] = v

    nh = RWKV_HEADS
    ri = lax.broadcasted_iota(jnp.int32, (c, d), 0)
    ci = lax.broadcasted_iota(jnp.int32, (c, d), 1) % HEAD_DIM
    strict = ci < ri
    incl = ci <= ri
    diag = ci == ri
    eye = diag.astype(F32)
    bi = lax.broadcasted_iota(jnp.int32, (d, d), 0) // HEAD_DIM
    bj = lax.broadcasted_iota(jnp.int32, (d, d), 1) // HEAD_DIM
    blocks = bi == bj
    zero16 = jnp.zeros((), BF16)

    def expand(x16):
        return jnp.where(blocks, jnp.concatenate([x16] * nh, axis=0), zero16)

    def bdmm(a, y, kind="nn", passes=3):
        dg = lambda p_, q_: lax.dot_general(p_, q_, _DIMS[kind], preferred_element_type=F32)
        if passes == 1:
            return dg(a.astype(BF16), expand(y.astype(BF16)))
        ah, al = _split(a, 2)
        yh, yl = _split(y, 2)
        m = a.shape[0]
        both = dg(jnp.concatenate([ah, al], axis=0), expand(yh))
        return both[:m] + (both[m:] + dg(ah, expand(yl)))

    def block_diagonal_of(full):
        outs = []
        for n0 in range(0, full.shape[1], d):
            m = jnp.where(blocks, full[:, n0:n0 + d], 0.0)
            acc = m[0:c]
            for h in range(1, nh):
                acc = acc + m[h * c:(h + 1) * c]
            outs.append(acc)
        return outs

    nchunk = tb // c
    chunks = range(nchunk)
    cat0 = lambda x, y: jnp.concatenate([x, y], axis=0)
    cat1 = lambda x, y: jnp.concatenate([x, y], axis=1)
    levels = int(math.log2(c)) - 1
    get = lambda ref: [ref[j * c:(j + 1) * c, :] for j in chunks]

    kt, rt, vv, btg = get(kt_s), get(rt_s), get(v_s), get(btg_s)
    lhs = [cat0(k_, r_) for k_, r_ in zip(kt, rt)]
    a_b = [bdmm(l_, b_, "nt") for l_, b_ in zip(lhs, get(bt_s))]
    a_k = [bdmm(l_, n_, "nt") for l_, n_ in zip(lhs, get(kn_s))]
    a_ab = [jnp.where(strict, m[:c], 0.0) for m in a_b]
    a_rb = [jnp.where(incl, m[c:], 0.0) for m in a_b]
    a_kr = [cat0(jnp.where(strict, m[:c], 0.0), jnp.where(incl, m[c:], 0.0)) for m in a_k]
    inv = [eye + m for m in a_ab]
    pw = [bdmm(m, m, passes=1) for m in a_ab]
    for lvl in range(levels):
        if lvl < levels - 1:
            both = [bdmm(cat0(x_, p_), p_, passes=1) for x_, p_ in zip(inv, pw)]
            inv = [x_ + b_[:c] for x_, b_ in zip(inv, both)]
            pw = [b_[c:] for b_ in both]
        else:
            inv = [x_ + bdmm(x_, p_, passes=1) for x_, p_ in zip(inv, pw)]
    av = [bdmm(m, v_) for m, v_ in zip(a_kr, vv)]
    wmat = [bdmm(x_, k_, passes=1) for x_, k_ in zip(inv, kt)]
    umat = [bdmm(x_, a_[:c], passes=1) for x_, a_ in zip(inv, av)]
    rw = [bdmm(m, w_, passes=1) for m, w_ in zip(a_rb, wmat)]
    ru = [bdmm(m, u_, passes=1) for m, u_ in zip(a_rb, umat)]
    gz = [block_diagonal_of(_mm(b_, cat1(w_, u_), "tn", passes=1))
          for b_, w_, u_ in zip(btg, wmat, umat)]
    kv = [block_diagonal_of(_mm(n_, v_, "tn"))[0] for n_, v_ in zip(get(kng_s), vv)]
    for j in chunks:
        rows = slice(j * c, (j + 1) * c)
        rp_s[rows, :] = rt[j] + rw[j]
        y0_s[rows, :] = ru[j] + av[j][c:]
        g_diag = jnp.where(diag, jnp.broadcast_to(etot_s[j * c:j * c + 1, :], (c, d)), 0.0)
        gt_s[rows, :] = g_diag + gz[j][0]
        zt_s[rows, :] = gz[j][1] + kv[j]

    state = state_s[...]
    for j in chunks:
        rows = slice(j * c, (j + 1) * c)
        ry = bdmm(cat0(rp_s[rows, :], gt_s[rows, :]), state)
        y_s[rows, :] = ry[:c] + y0_s[rows, :]
        state = ry[c:] + zt_s[rows, :]
    state_s[...] = state

    y = y_s[...]
    mean = _mm_ones(y, hsum, terms=2) * (1.0 / HEAD_DIM)
    dy = y - mean
    var = _mm_ones(dy * dy, hsum, terms=2) * (1.0 / HEAD_DIM)
    yn = dy * lax.rsqrt(var + RWKV_GN_EPS) * lng_ref[...] + lnb_ref[...]
    bonus = _mm_ones(r * kn * rk_ref[...], hsum, terms=2) * v
    o_ref[...] = ((yn + bonus) * g).astype(o_ref.dtype)


def _chunk_masks(tb, c):
    i = np.arange(tb)
    same = (i[:, None] // c) == (i[None, :] // c)
    ltri = same & (i[None, :] <= i[:, None])
    return jnp.asarray(ltri, BF16), jnp.asarray(same, BF16)


def _head_sum_matrix(width):
    i = np.arange(width)
    return jnp.asarray((i[:, None] // HEAD_DIM) == (i[None, :] // HEAD_DIM), BF16)


def rwkv_time_mix(p, batch, seq, mu, w0, w_up, a0, a_up, g_up, k_k, k_a, r_k, ln_g, ln_b):
    t = batch * seq
    tb = min(WKV_BLOCK, seq)
    nblk = seq // tb
    ltri, same = _chunk_masks(tb, WKV_CHUNK)
    hsum = _head_sum_matrix(RWKV_DIM)
    row = lambda a: a.reshape(1, -1)
    consts = [row(mu), row(w0), w_up, row(a0), a_up, g_up, row(k_k), row(k_a), row(r_k), row(ln_g), row(ln_b),
              ltri, same, hsum]
    buf = lambda: pltpu.VMEM((tb, RWKV_DIM), F32)
    return pl.pallas_call(
        _rwkv_kernel,
        grid=(batch, nblk),
        in_specs=[pl.BlockSpec((tb, RWKV_IN), lambda b, j: (b * nblk + j, 0))]
                 + [_const_spec(a.shape) for a in consts],
        out_specs=pl.BlockSpec((tb, RWKV_DIM), lambda b, j: (b * nblk + j, 0)),
        out_shape=jax.ShapeDtypeStruct((t, RWKV_DIM), BF16),
        scratch_shapes=[pltpu.VMEM((HEAD_DIM, RWKV_DIM), F32), pltpu.VMEM((1, RWKV_IN), F32)]
                       + [buf() for _ in range(13)],
        compiler_params=_params(("parallel", "arbitrary")),
        name="rwkv_time_mix",
    )(p, *consts)


def _attn_kernel(q_ref, k_ref, v_ref, bias_ref, o_ref, acc_s, m_s, l_s):
    seq = q_ref.shape[0]
    w = ATTN_W
    scale = HEAD_DIM ** -0.5

    def rows_of(start, dil):
        return pl.ds(start, w) if dil == 1 else pl.ds(start, w, stride=dil)

    lane = lax.broadcasted_iota(jnp.int32, (w, LANES), 1)
    head0 = lane < HEAD_DIM
    zero = jnp.zeros((), BF16)
    one = jnp.ones((), BF16)

    def group(pi, dil, g, firsts):
        rows_l, q_l, k_l, v_l = [], [], [], []
        for u, first in enumerate(firsts):
            b = g * len(firsts) + u
            start = (b % dil) + (b // dil) * (dil * w)
            rows = rows_of(start, dil)
            q = (q_ref[rows, :] * scale).astype(BF16)
            kk = k_ref[rows, :].astype(BF16)
            vv = v_ref[rows, :].astype(BF16)
            if not first:
                prev = rows_of(start - dil * w, dil)
                kk = jnp.concatenate([k_ref[prev, :].astype(BF16), kk], axis=0)
                vv = jnp.concatenate([v_ref[prev, :].astype(BF16), vv], axis=0)
            rows_l.append(rows)
            q_l.append(q)
            k_l.append(kk)
            v_l.append(vv)
        s = [[_dot_nt(jnp.where(head0 if h == 0 else ~head0, q, zero), kk)
              + (bias_ref[pi, h, :, w:] if first else bias_ref[pi, h])
              for h in range(2)] for q, kk, first in zip(q_l, k_l, firsts)]
        m = [[jnp.max(sh, axis=-1, keepdims=True) for sh in su] for su in s]
        pr = [[jnp.exp(sh - mh).astype(BF16) for sh, mh in zip(su, mu)] for su, mu in zip(s, m)]
        kmask = lambda vv: lax.broadcasted_iota(jnp.int32, vv.shape, 1) < HEAD_DIM
        res = [[_dot(pu[0], jnp.where(kmask(vv), vv, one)), _dot(pu[1], jnp.where(kmask(vv), one, vv))]
               for pu, vv in zip(pr, v_l)]
        for rows, ru, mu in zip(rows_l, res, m):
            acc_s[pi, rows, :] = jnp.where(head0, ru[0], ru[1])
            l_s[pi, rows, :] = jnp.where(head0, ru[1], ru[0])
            m_s[pi, rows, :] = jnp.where(head0, mu[0], mu[1])

    n_groups = (seq // w) // ATTN_UNROLL
    for pi, (window, dil) in enumerate(DILATED_PATTERNS):
        flags = [tuple((g * ATTN_UNROLL + u) < dil for u in range(ATTN_UNROLL)) for g in range(n_groups)]
        g0 = 0
        while g0 < n_groups:
            g1 = g0
            while g1 < n_groups and flags[g1] == flags[g0]:
                g1 += 1
            if g1 - g0 == 1:
                group(pi, dil, g0, flags[g0])
            else:
                def body(g, carry, pi=pi, dil=dil, firsts=flags[g0]):
                    group(pi, dil, g, firsts)
                    return carry
                lax.fori_loop(g0, g1, body, 0)
            g0 = g1

    mt = 256

    def merge_body(i, carry):
        rows = pl.ds(pl.multiple_of(i * mt, mt), mt)
        m0, m1, m2 = m_s[0, rows, :], m_s[1, rows, :], m_s[2, rows, :]
        mx = jnp.maximum(jnp.maximum(m0, m1), m2)
        w0, w1, w2 = jnp.exp(m0 - mx), jnp.exp(m1 - mx), jnp.exp(m2 - mx)
        num = w0 * acc_s[0, rows, :] + w1 * acc_s[1, rows, :] + w2 * acc_s[2, rows, :]
        swap = lambda x: pltpu.roll(x, HEAD_DIM, axis=1)
        den = w0 * swap(l_s[0, rows, :]) + w1 * swap(l_s[1, rows, :]) + w2 * swap(l_s[2, rows, :])
        o_ref[rows, :] = (num / den).astype(o_ref.dtype)
        return carry

    lax.fori_loop(0, seq // mt, merge_body, 0)


def _t5_bucket(dist):
    max_exact = NUM_BUCKETS // 2
    large = max_exact + (np.log(np.maximum(dist, max_exact) / max_exact)
                         / math.log(MAX_DISTANCE / max_exact) * (NUM_BUCKETS - max_exact)).astype(np.int32)
    return np.where(dist < max_exact, dist, np.minimum(large, NUM_BUCKETS - 1)).astype(np.int32)


def _attn_bias(rel_bias):
    w = ATTN_W
    i = np.arange(w)[:, None]
    j = np.arange(2 * w)[None, :]
    rel = i + w - j
    band = (rel >= 0) & (rel <= w)
    tabs = []
    for window, dil in DILATED_PATTERNS:
        bucket = _t5_bucket(np.clip(rel, 0, None) * dil)
        onehot = jnp.asarray(bucket[..., None] == np.arange(NUM_BUCKETS), F32)
        bias = jnp.einsum('ijb,bh->hij', onehot, rel_bias.astype(F32), precision=HI)
        tabs.append(jnp.where(band[None], bias, MASK_VALUE))
    return jnp.stack(tabs)


def dilated_attention(p, batch, seq, rel_bias):
    t = batch * seq
    bias = _attn_bias(rel_bias)
    col0 = RWKV_IN // LANES
    npair = ATTN_DIM // LANES
    spec = lambda off: pl.BlockSpec((seq, LANES), lambda b, hp: (b, col0 + off + hp))
    return pl.pallas_call(
        _attn_kernel,
        grid=(batch, npair),
        in_specs=[spec(0), spec(npair), spec(2 * npair),
                  pl.BlockSpec((3, 2, ATTN_W, 2 * ATTN_W), lambda b, hp: (0, hp, 0, 0))],
        out_specs=pl.BlockSpec((seq, LANES), lambda b, hp: (b, hp)),
        out_shape=jax.ShapeDtypeStruct((t, ATTN_DIM), BF16),
        scratch_shapes=[pltpu.VMEM((3, seq, LANES), F32) for _ in range(3)],
        compiler_params=_params(("parallel", "parallel")),
        name="dilated_attention",
    )(p, p, p, bias)


def _ret_kernel(q_ref, k_ref, v_ref, g_ref, cos_ref, sin_ref, dmat_ref, xi_ref, zeta_ref, gng_ref, gnb_ref,
                hsum_ref, o_ref, state_s):
    c = RET_CHUNK
    tb = q_ref.shape[0]

    @pl.when(pl.program_id(1) == 0)
    def _():
        state_s[...] = jnp.zeros_like(state_s)

    lane = lax.broadcasted_iota(jnp.int32, (c, RET_DIM), 1)
    first_half = (lane % HEAD_DIM) < (HEAD_DIM // 2)

    def rotate(x, cos, sin):
        swapped = jnp.where(first_half, pltpu.roll(x, RET_DIM - HEAD_DIM // 2, axis=1),
                            pltpu.roll(x, HEAD_DIM // 2, axis=1))
        return x * cos + swapped * sin

    heads = range(RET_HEADS)
    hsl = [slice(h * HEAD_DIM, (h + 1) * HEAD_DIM) for h in heads]
    chunk_decay = [(1.0 - 2.0 ** (-5.0 - h)) ** c for h in heads]

    def group_body(g, carry):
        rows_l, qb, kb, qx, kz, vb = [], [], [], [], [], []
        for u in range(RET_UNROLL):
            rows = pl.ds(pl.multiple_of((g * RET_UNROLL + u) * c, c), c)
            cos, sin = cos_ref[rows, :], sin_ref[rows, :]
            q = rotate(q_ref[rows, :], cos, sin)
            k = rotate(k_ref[rows, :], cos, sin) * (HEAD_DIM ** -0.5)
            rows_l.append(rows)
            qb.append(q.astype(BF16))
            kb.append(k.astype(BF16))
            qx.append((q * xi_ref[...]).astype(BF16))
            kz.append((k * zeta_ref[...]).astype(BF16))
            vb.append(v_ref[rows, :].astype(BF16))
        sc = [[(_dot_nt(qb[u][:, sl], kb[u][:, sl]) * dmat_ref[h]).astype(BF16) for h, sl in zip(heads, hsl)]
              for u in range(RET_UNROLL)]
        intra = [[_dot(sc[u][h], vb[u][:, hsl[h]]) for h in heads] for u in range(RET_UNROLL)]
        kv = [[_dot_tn(kz[u][:, sl], vb[u][:, sl]) for sl in hsl] for u in range(RET_UNROLL)]
        states = [state_s[h] for h in heads]
        ys = []
        for u in range(RET_UNROLL):
            ys.append(jnp.concatenate(
                [intra[u][h] + _dot(qx[u][:, hsl[h]], states[h].astype(BF16)) for h in heads], axis=1))
            states = [states[h] * chunk_decay[h] + kv[u][h] for h in heads]
        for h in heads:
            state_s[h] = states[h]
        hsum = hsum_ref[...]
        for rows, y in zip(rows_l, ys):
            mean = _mm_ones(y, hsum, terms=2) * (1.0 / HEAD_DIM)
            dy = y - mean
            var = _mm_ones(dy * dy, hsum, terms=2) * (1.0 / HEAD_DIM)
            yn = dy * lax.rsqrt(var + LN_EPS) * gng_ref[...] + gnb_ref[...]
            gate = g_ref[rows, :]
            o_ref[rows, :] = (gate * _sigmoid(gate) * yn).astype(o_ref.dtype)
        return carry

    lax.fori_loop(0, tb // (c * RET_UNROLL), group_body, 0)


def _ret_tables(seq):
    c = RET_CHUNK
    half = HEAD_DIM // 2
    inv = ROPE_BASE ** (-jnp.arange(half, dtype=F32) / half)
    ang = jnp.arange(seq, dtype=F32)[:, None] * inv
    cos, sin = jnp.cos(ang), jnp.sin(ang)
    cos_t = jnp.tile(jnp.concatenate([cos, cos], axis=1), (1, RET_HEADS))
    sin_t = jnp.tile(jnp.concatenate([-sin, sin], axis=1), (1, RET_HEADS))
    log_g = jnp.log1p(-jnp.exp2(-5.0 - jnp.arange(RET_HEADS, dtype=F32)))
    n = jnp.arange(c, dtype=F32)
    diff = n[:, None] - n[None, :]
    dmat = jnp.where(diff >= 0, jnp.exp(log_g[:, None, None] * jnp.maximum(diff, 0.0)), 0.0)
    zeta = jnp.exp(log_g[:, None] * (c - 1 - n))
    xi = jnp.exp(log_g[:, None] * (n + 1))
    widen = lambda tab: jnp.repeat(tab.T, HEAD_DIM, axis=1)
    return cos_t, sin_t, dmat, widen(xi), widen(zeta)


def retention(p, batch, seq, gn_g, gn_b):
    t = batch * seq
    tb = min(RET_BLOCK, seq)
    nblk = seq // tb
    cos_t, sin_t, dmat, xi, zeta = _ret_tables(seq)
    hsum = _head_sum_matrix(RET_DIM)
    col0 = (RWKV_IN + ATTN_IN) // RET_DIM
    spec = lambda off: pl.BlockSpec((tb, RET_DIM), lambda b, j: (b * nblk + j, col0 + off))
    tab = pl.BlockSpec((tb, RET_DIM), lambda b, j: (j, 0))
    consts = [dmat, xi, zeta, gn_g.reshape(1, -1), gn_b.reshape(1, -1), hsum]
    return pl.pallas_call(
        _ret_kernel,
        grid=(batch, nblk),
        in_specs=[spec(0), spec(1), spec(2), spec(3), tab, tab] + [_const_spec(a.shape) for a in consts],
        out_specs=pl.BlockSpec((tb, RET_DIM), lambda b, j: (b * nblk + j, 0)),
        out_shape=jax.ShapeDtypeStruct((t, RET_DIM), BF16),
        scratch_shapes=[pltpu.VMEM((RET_HEADS, HEAD_DIM, HEAD_DIM), F32)],
        compiler_params=_params(("parallel", "arbitrary")),
        name="retention",
    )(p, p, p, p, cos_t, sin_t, *consts)


def _outproj_kernel(ya_ref, yb_ref, yc_ref, x_ref, w_ref, g_ref, b_ref, *rest, alpha):
    acc = _dot(ya_ref[...], w_ref[0:RWKV_DIM, :])
    acc += _dot(yb_ref[...], w_ref[RWKV_DIM:RWKV_DIM + ATTN_DIM, :])
    acc += _dot(yc_ref[...], w_ref[RWKV_DIM + ATTN_DIM:, :])
    h = _layer_norm(alpha * x_ref[...] + acc, g_ref[...], b_ref[...])
    if len(rest) == 1:
        (o_ref,) = rest
    else:
        router_ref, o_ref, logit_ref = rest
        logit_ref[...] = _dot_nt(router_ref[...].astype(BF16), h.astype(BF16))
    o_ref[...] = h


def out_projection_ln(ya, yb, yc, x2d, w_bf16, layer, g, b, alpha, router=None):
    t, d = x2d.shape
    tm = min(ROW_TILE, t)
    rows = lambda width: pl.BlockSpec((tm, width), lambda i: (i, 0))
    in_specs = [rows(RWKV_DIM), rows(ATTN_DIM), rows(RET_DIM), rows(d), _layer_spec(w_bf16, layer),
                _const_spec((1, d)), _const_spec((1, d))]
    args = [ya, yb, yc, x2d, w_bf16, g.reshape(1, -1), b.reshape(1, -1)]
    out_specs, out_shape = rows(d), jax.ShapeDtypeStruct((t, d), F32)
    if router is not None:
        in_specs.append(_const_spec((N_EXPERTS, d)))
        args.append(router.T)
        out_specs = [out_specs, pl.BlockSpec((N_EXPERTS, tm), lambda i: (0, i))]
        out_shape = [out_shape, jax.ShapeDtypeStruct((N_EXPERTS, t), F32)]
    return pl.pallas_call(
        functools.partial(_outproj_kernel, alpha=alpha),
        grid=(t // tm,),
        in_specs=in_specs,
        out_specs=out_specs,
        out_shape=out_shape,
        compiler_params=_params(("parallel",)),
        name="out_projection_ln",
    )(*args)


def _swiglu(xb, wg, wu, wd, acc, f_chunk):
    ff = wg.shape[-1]
    starts = list(range(0, ff, f_chunk))
    gate_up = lambda f0: (_dot(xb, wg[:, f0:f0 + f_chunk]), _dot(xb, wu[:, f0:f0 + f_chunk]))
    nxt = gate_up(starts[0])
    for n, f0 in enumerate(starts):
        gate, up = nxt
        if n + 1 < len(starts):
            nxt = gate_up(starts[n + 1])
        hid = (gate * _sigmoid(gate) * up).astype(BF16)
        part = _dot(hid, wd[f0:f0 + f_chunk, :])
        acc = part if acc is None else acc + part
    return acc


def _ffn_kernel(x_ref, wg_ref, wu_ref, wd_ref, g_ref, b_ref, o_ref, *, alpha, f_chunk):
    x = x_ref[...]
    acc = _swiglu(x.astype(BF16), wg_ref, wu_ref, wd_ref, alpha * x, f_chunk)
    o_ref[...] = _layer_norm(acc, g_ref[...], b_ref[...])


def dense_ffn_ln(x2d, wg, wu, wd, layer, g, b, alpha):
    t, d = x2d.shape
    tm = min(ROW_TILE, t)
    return pl.pallas_call(
        functools.partial(_ffn_kernel, alpha=alpha, f_chunk=FFN_SUBCHUNK),
        grid=(t // tm,),
        in_specs=[pl.BlockSpec((tm, d), lambda i: (i, 0)), _layer_spec(wg, layer), _layer_spec(wu, layer),
                  _layer_spec(wd, layer), _const_spec((1, d)), _const_spec((1, d))],
        out_specs=pl.BlockSpec((tm, d), lambda i: (i, 0)),
        out_shape=jax.ShapeDtypeStruct((t, d), F32),
        compiler_params=_params(("parallel",)),
        name="dense_ffn_ln",
    )(x2d, wg, wu, wd, g.reshape(1, -1), b.reshape(1, -1))


def _expert_kernel(blk_e_ref, used_ref, x_ref, wg_ref, wu_ref, wd_ref, *rest, blk_off):
    o_ref, acc_s = rest[-2:]
    i, j = pl.program_id(0) + blk_off, pl.program_id(1)
    last = pl.num_programs(1) - 1

    @pl.when(i < used_ref[0])
    def _():
        part = _swiglu(x_ref[...].astype(BF16), wg_ref.at[0], wu_ref.at[0], wd_ref.at[0], None, FFN_SUBCHUNK)

        @pl.when(j == 0)
        def _():
            acc_s[...] = part

        @pl.when((j > 0) & (j < last))
        def _():
            acc_s[...] += part

        @pl.when(j == last)
        def _():
            o_ref[...] = (acc_s[...] + part).astype(o_ref.dtype)


def expert_ffn(xs, blk_e, n_used, wg, wu, wd, layer, blk_off, out_rows, earlier=None, f_chunk=MOE_F_BLOCK):
    rows, d = xs.shape
    ff = wg.shape[3]
    tm = MOE_TILE
    in_specs = [pl.BlockSpec((tm, d), lambda i, j, be, nu: (i, 0)),
                pl.BlockSpec((None, 1, d, f_chunk), lambda i, j, be, nu: (layer, be[i + blk_off], 0, j)),
                pl.BlockSpec((None, 1, d, f_chunk), lambda i, j, be, nu: (layer, be[i + blk_off], 0, j)),
                pl.BlockSpec((None, 1, f_chunk, d), lambda i, j, be, nu: (layer, be[i + blk_off], j, 0))]
    args = [blk_e, n_used, xs, wg, wu, wd]
    aliases = {}
    if earlier is not None:
        in_specs.append(pl.BlockSpec(memory_space=pl.ANY))
        args.append(earlier)
        aliases = {len(args) - 1: 0}
    assert ff // f_chunk >= 2
    grid_spec = pltpu.PrefetchScalarGridSpec(
        num_scalar_prefetch=2,
        grid=(rows // tm, ff // f_chunk),
        in_specs=in_specs,
        out_specs=pl.BlockSpec((tm, d), lambda i, j, be, nu: (i + blk_off, 0)),
        scratch_shapes=[pltpu.VMEM((tm, d), F32)],
    )
    return pl.pallas_call(
        functools.partial(_expert_kernel, blk_off=blk_off),
        grid_spec=grid_spec,
        out_shape=jax.ShapeDtypeStruct((out_rows, d), BF16),
        input_output_aliases=aliases,
        compiler_params=_params(("parallel", "arbitrary")),
        name="expert_ffn",
    )(*args)


def _combine_ln_kernel(x_ref, y0_ref, y1_ref, gate_ref, g_ref, b_ref, o_ref, *, alpha):
    gates = gate_ref[...]
    f = y0_ref[...] * gates[:, 0:1] + y1_ref[...] * gates[:, 1:2]
    o_ref[...] = _layer_norm(alpha * x_ref[...] + f, g_ref[...], b_ref[...])


def combine_ln(x2d, y0, y1, gates, g, b, alpha):
    t, d = x2d.shape
    tm = min(ROW_TILE, t)
    rows = pl.BlockSpec((tm, d), lambda i: (i, 0))
    return pl.pallas_call(
        functools.partial(_combine_ln_kernel, alpha=alpha),
        grid=(t // tm,),
        in_specs=[rows, rows, rows, pl.BlockSpec((tm, TOP_K), lambda i: (i, 0)),
                  _const_spec((1, d)), _const_spec((1, d))],
        out_specs=rows,
        out_shape=jax.ShapeDtypeStruct((t, d), F32),
        compiler_params=_params(("parallel",)),
        name="combine_ln",
    )(x2d, y0, y1, gates, g.reshape(1, -1), b.reshape(1, -1))


def moe_ffn_ln(x2d, logits_t, wg, wu, wd, layer, g, b, alpha):
    t, d = x2d.shape
    tm = MOE_TILE
    logits = logits_t.T
    top_val, top_idx = lax.top_k(logits, TOP_K)
    gates = jax.nn.softmax(top_val, axis=-1)
    member = (top_idx[:, :, None] == jnp.arange(N_EXPERTS)[None, None, :]).any(axis=1)
    counts = member.sum(axis=0).astype(jnp.int32)
    rank = jnp.cumsum(member.astype(jnp.int32), axis=0) - member.astype(jnp.int32)
    padded = (counts + tm - 1) // tm * tm
    pad_ends = jnp.cumsum(padded)
    pad_starts = pad_ends - padded
    pos = jnp.take_along_axis(pad_starts[None, :] + rank, top_idx, axis=1)
    rows = t * TOP_K + N_EXPERTS * tm
    n_blk = rows // tm
    order = jnp.argsort(top_idx.reshape(-1), stable=True).astype(jnp.int32)
    tok_sorted = order // TOP_K
    starts = jnp.cumsum(counts) - counts
    r = jnp.arange(rows, dtype=jnp.int32)
    row_e = jnp.minimum(jnp.searchsorted(pad_ends, r, side='right'), N_EXPERTS - 1).astype(jnp.int32)
    within = r - pad_starts[row_e]
    src = jnp.where(within < counts[row_e], tok_sorted[jnp.minimum(starts[row_e] + within, t * TOP_K - 1)], 0)
    blk_e = row_e[::tm]
    n_used = (pad_ends[-1:] // tm).astype(jnp.int32)
    half = (n_blk // 2) * tm
    ys = expert_ffn(x2d[src[:half]], blk_e, n_used, wg, wu, wd, layer, 0, rows)
    ys = expert_ffn(x2d[src[half:]], blk_e, n_used, wg, wu, wd, layer, half // tm, rows, earlier=ys)
    return combine_ln(x2d, ys[pos[:, 0]], ys[pos[:, 1]], gates, g, b, alpha)


def kernel(x, w_in, w_out, rwkv_mu, rwkv_w0, rwkv_w_up, rwkv_a0, rwkv_a_up, rwkv_g_up, rwkv_k_k, rwkv_k_a,
           rwkv_r_k, rwkv_ln_g, rwkv_ln_b, ret_gn_g, ret_gn_b, rel_bias, ln_g, ln_b, ffn_w_gate, ffn_w_up,
           ffn_w_down, moe_router, moe_w_gate, moe_w_up, moe_w_down):
    batch, seq, d = x.shape
    depth = w_in.shape[0]
    alpha = (2 * depth) ** 0.25
    h = x.reshape(batch * seq, d)
    w_in, w_out = w_in.astype(BF16), w_out.astype(BF16)
    ffn_w_gate, ffn_w_up, ffn_w_down = ffn_w_gate.astype(BF16), ffn_w_up.astype(BF16), ffn_w_down.astype(BF16)
    for layer in range(depth):
        p = in_projection(h, w_in, layer)
        ya = rwkv_time_mix(p, batch, seq, rwkv_mu[layer], rwkv_w0[layer], rwkv_w_up[layer], rwkv_a0[layer],
                           rwkv_a_up[layer], rwkv_g_up[layer], rwkv_k_k[layer], rwkv_k_a[layer],
                           rwkv_r_k[layer], rwkv_ln_g[layer], rwkv_ln_b[layer])
        yb = dilated_attention(p, batch, seq, rel_bias)
        yc = retention(p, batch, seq, ret_gn_g[layer], ret_gn_b[layer])
        j = layer // 2
        if layer % 2 == 0:
            h = out_projection_ln(ya, yb, yc, h, w_out, layer, ln_g[layer, 0], ln_b[layer, 0], alpha)
            h = dense_ffn_ln(h, ffn_w_gate, ffn_w_up, ffn_w_down, j, ln_g[layer, 1], ln_b[layer, 1], alpha)
        else:
            h, logits_t = out_projection_ln(ya, yb, yc, h, w_out, layer, ln_g[layer, 0], ln_b[layer, 0], alpha,
                                            router=moe_router[j])
            h = moe_ffn_ln(h, logits_t, layer_to_bf16(moe_w_gate, j), layer_to_bf16(moe_w_up, j),
                           layer_to_bf16(moe_w_down, j), 0, ln_g[layer, 1], ln_b[layer, 1], alpha)
    return h.reshape(batch, seq, d)
```

```python
import functools
import math

import numpy as np
import jax
import jax.numpy as jnp
from jax import lax
from jax.experimental import pallas as pl
from jax.experimental.pallas import tpu as pltpu

F32 = jnp.float32
BF16 = jnp.bfloat16
HI = lax.Precision.HIGHEST

HEAD_DIM = 64
RWKV_HEADS = 4
ATTN_HEADS = 8
RET_HEADS = 4
RWKV_DIM = RWKV_HEADS * HEAD_DIM
ATTN_DIM = ATTN_HEADS * HEAD_DIM
RET_DIM = RET_HEADS * HEAD_DIM
DECAY_LORA = 64
ICL_LORA = 64
GATE_LORA = 128
RWKV_IN = 3 * RWKV_DIM + DECAY_LORA + ICL_LORA + GATE_LORA
ATTN_IN = 3 * ATTN_DIM
RET_IN = 4 * RET_DIM
RWKV_GN_EPS = 64e-5
DECAY_SCALE = math.exp(-0.5)
DILATED_PATTERNS = ((128, 1), (512, 4), (2048, 16))
NUM_BUCKETS = 32
MAX_DISTANCE = 2048
ROPE_BASE = 10000.0
N_EXPERTS = 8
TOP_K = 2
LN_EPS = 1e-5

LANES = 128
WKV_CHUNK = 64
WKV_BLOCK = 512
WKV_MASK_ROWS = 256
WKV_GROUP = 4
ATTN_W = 128
ATTN_UNROLL = 4
RET_CHUNK = 128
RET_BLOCK = 1024
RET_UNROLL = 4
ROW_TILE = 512
MOE_TILE = 512
MOE_F_BLOCK = 1792
FFN_SUBCHUNK = 256
MASK_VALUE = -1e30
VMEM_LIMIT = 56 * 1024 * 1024


def _dot(a, b, prec=None):
    return jnp.dot(a, b, preferred_element_type=F32, precision=prec)


def _dot_nt(a, b, prec=None):
    return lax.dot_general(a, b, (((1,), (1,)), ((), ())), preferred_element_type=F32, precision=prec)


def _dot_tn(a, b, prec=None):
    return lax.dot_general(a, b, (((0,), (0,)), ((), ())), preferred_element_type=F32, precision=prec)


_DIMS = {"nn": (((1,), (0,)), ((), ())), "nt": (((1,), (1,)), ((), ())), "tn": (((0,), (0,)), ((), ()))}


def _split(x, terms):
    parts = []
    for _ in range(terms - 1):
        hi = x.astype(BF16)
        parts.append(hi)
        x = x - hi.astype(F32)
    parts.append(x.astype(BF16))
    return parts


def _mm(a, b, kind="nn", passes=3):
    dg = lambda p, q: lax.dot_general(p, q, _DIMS[kind], preferred_element_type=F32)
    if passes == 1:
        return dg(a.astype(BF16), b.astype(BF16))
    ah, al = _split(a, 2)
    bh, bl = _split(b, 2)
    return dg(ah, bh) + (dg(al, bh) + dg(ah, bl))


def _mm_ones(x, ones_bf16, ones_first=False, terms=3):
    parts = _split(x, terms)
    m, n = x.shape
    if ones_first:
        full = _dot(ones_bf16, jnp.concatenate(parts, axis=1))
        out = [full[:, i * n:(i + 1) * n] for i in range(terms)]
    else:
        full = _dot(jnp.concatenate(parts, axis=0), ones_bf16)
        out = [full[i * m:(i + 1) * m] for i in range(terms)]
    acc = out[-1]
    for o in reversed(out[:-1]):
        acc = acc + o
    return acc


def _sigmoid(x):
    return 1.0 / (1.0 + jnp.exp(-x))


def _layer_norm(z, g, b):
    mu = jnp.mean(z, axis=-1, keepdims=True)
    d = z - mu
    var = jnp.mean(d * d, axis=-1, keepdims=True)
    return d * lax.rsqrt(var + LN_EPS) * g + b


def _params(sem, vmem=VMEM_LIMIT):
    return pltpu.CompilerParams(dimension_semantics=sem, vmem_limit_bytes=vmem)


def _const_spec(shape):
    nd = len(shape)
    return pl.BlockSpec(shape, lambda *_: (0,) * nd)


def _layer_spec(stacked, layer):
    nd = stacked.ndim - 1
    return pl.BlockSpec((None,) + stacked.shape[1:], lambda *_: (layer,) + (0,) * nd)


CAST_BLOCK_BYTES = 8 * 1024 * 1024


def _cast_kernel(x_ref, o_ref):
    o_ref[...] = x_ref[...].astype(o_ref.dtype)


def layer_to_bf16(w, layer):
    shape = w.shape
    cols = shape[-1]
    rows = w[0].size // cols
    w3 = w.reshape(shape[0], rows, cols)
    tr = 1 << int(math.log2(max(16, min(rows, CAST_BLOCK_BYTES // (4 * cols)))))
    while rows % tr:
        tr //= 2
    assert tr % 16 == 0
    out = pl.pallas_call(
        _cast_kernel,
        grid=(rows // tr,),
        in_specs=[pl.BlockSpec((None, tr, cols), lambda i: (layer, i, 0))],
        out_specs=pl.BlockSpec((tr, cols), lambda i: (i, 0)),
        out_shape=jax.ShapeDtypeStruct((rows, cols), BF16),
        compiler_params=_params(("parallel",)),
        name="layer_to_bf16",
    )(w3)
    return out.reshape((1,) + shape[1:])


def _inproj_kernel(x_ref, w_ref, o_ref, *, n_chunk):
    xb = x_ref[...].astype(BF16)
    for n0 in range(0, o_ref.shape[1], n_chunk):
        o_ref[:, n0:n0 + n_chunk] = _dot(xb, w_ref[:, n0:n0 + n_chunk])


def in_projection(x2d, w_bf16, layer):
    t, d = x2d.shape
    n = w_bf16.shape[2]
    tm = min(ROW_TILE, t)
    return pl.pallas_call(
        functools.partial(_inproj_kernel, n_chunk=512),
        grid=(t // tm,),
        in_specs=[pl.BlockSpec((tm, d), lambda i: (i, 0)), _layer_spec(w_bf16, layer)],
        out_specs=pl.BlockSpec((tm, n), lambda i: (i, 0)),
        out_shape=jax.ShapeDtypeStruct((t, n), F32),
        compiler_params=_params(("parallel",)),
        name="in_projection",
    )(x2d, w_bf16)


def _rwkv_kernel(p_ref, mu_ref, w0_ref, wup_ref, a0_ref, aup_ref, gup_ref, kk_ref, ka_ref, rk_ref,
                 lng_ref, lnb_ref, ltri_ref, same_ref, hsum_ref, o_ref,
                 state_s, prev_s, kt_s, rt_s, bt_s, kn_s, v_s, btg_s, kng_s, etot_s, y_s, rp_s, y0_s, gt_s, zt_s):
    c = WKV_CHUNK
    tb = p_ref.shape[0]
    d = RWKV_DIM
    assert c == HEAD_DIM

    @pl.when(pl.program_id(1) == 0)
    def _():
        state_s[...] = jnp.zeros_like(state_s)
        prev_s[...] = jnp.zeros_like(prev_s)

    p = p_ref[...]
    row = lax.broadcasted_iota(jnp.int32, p.shape, 0)
    shifted = jnp.where(row == 0, prev_s[...], pltpu.roll(p, 1, axis=0))
    prev_s[...] = p[tb - 1:tb, :]
    ps = p + (shifted - p) * mu_ref[...]
    r = ps[:, 0:d]
    k = ps[:, d:2 * d]
    v = ps[:, 2 * d:3 * d]
    xw = ps[:, 3 * d:3 * d + DECAY_LORA]
    xa = ps[:, 3 * d + DECAY_LORA:3 * d + DECAY_LORA + ICL_LORA]
    xg = ps[:, 3 * d + DECAY_LORA + ICL_LORA:]

    hsum = hsum_ref[...]
    logw = -DECAY_SCALE * _sigmoid(w0_ref[...] + _mm(jnp.tanh(xw), wup_ref[...]))
    a = _sigmoid(a0_ref[...] + _mm(xa, aup_ref[...]))
    g = _dot(_sigmoid(xg).astype(BF16), gup_ref[...].astype(BF16))
    kap = k * kk_ref[...]
    kap = kap / jnp.maximum(jnp.sqrt(_mm_ones(kap * kap, hsum)), 1e-12)
    kn = k * (1.0 + (a - 1.0) * ka_ref[...])
    mb = ltri_ref.shape[0]
    ones2 = jnp.concatenate([ltri_ref[...], same_ref[...]], axis=0)
    sums = [_mm_ones(logw[s0:s0 + mb], ones2, ones_first=True) for s0 in range(0, tb, mb)]
    cum = jnp.concatenate([s[:mb] for s in sums], axis=0)
    tot = jnp.concatenate([s[mb:] for s in sums], axis=0)
    e_neg = jnp.exp(-cum)
    e_rem = jnp.exp(tot - cum)
    nb = -(a * kap)
    kt_s[...] = kap * jnp.exp(cum - logw)
    rt_s[...] = r * jnp.exp(cum)
    bt_s[...] = nb * e_neg
    kn_s[...] = kn * e_neg
    btg_s[...] = nb * e_rem
    kng_s[...] = kn * e_rem
    etot_s[...] = jnp.exp(tot)
    v_s[...] = v

    nh = RWKV_HEADS
    ri = lax.broadcasted_iota(jnp.int32, (c, d), 0)
    ci = lax.broadcasted_iota(jnp.int32, (c, d), 1) % HEAD_DIM
    strict = ci < ri
    incl = ci <= ri
    diag = ci == ri
    eye = diag.astype(F32)
    bi = lax.broadcasted_iota(jnp.int32, (d, d), 0) // HEAD_DIM
    bj = lax.broadcasted_iota(jnp.int32, (d, d), 1) // HEAD_DIM
    blocks = bi == bj
    zero16 = jnp.zeros((), BF16)

    def expand(x16):
        return jnp.where(blocks, jnp.concatenate([x16] * nh, axis=0), zero16)

    def bdmm(a, y, kind="nn", passes=3):
        dg = lambda p_, q_: lax.dot_general(p_, q_, _DIMS[kind], preferred_element_type=F32)
        if passes == 1:
            return dg(a.astype(BF16), expand(y.astype(BF16)))
        ah, al = _split(a, 2)
        yh, yl = _split(y, 2)
        m = a.shape[0]
        both = dg(jnp.concatenate([ah, al], axis=0), expand(yh))
        return both[:m] + (both[m:] + dg(ah, expand(yl)))

    def block_diagonal_of(full):
        outs = []
        for n0 in range(0, full.shape[1], d):
            m = jnp.where(blocks, full[:, n0:n0 + d], 0.0)
            acc = m[0:c]
            for h in range(1, nh):
                acc = acc + m[h * c:(h + 1) * c]
            outs.append(acc)
        return outs

    nchunk = tb // c
    cat0 = lambda x, y: jnp.concatenate([x, y], axis=0)
    cat1 = lambda x, y: jnp.concatenate([x, y], axis=1)
    levels = int(math.log2(c)) - 1

    def independent_part(chunks):
        get = lambda ref: [ref[j * c:(j + 1) * c, :] for j in chunks]
        kt, rt, vv, btg = get(kt_s), get(rt_s), get(v_s), get(btg_s)
        lhs = [cat0(k_, r_) for k_, r_ in zip(kt, rt)]
        a_b = [bdmm(l_, b_, "nt") for l_, b_ in zip(lhs, get(bt_s))]
        yield
        a_k = [bdmm(l_, n_, "nt") for l_, n_ in zip(lhs, get(kn_s))]
        yield
        a_ab = [jnp.where(strict, m[:c], 0.0) for m in a_b]
        a_rb = [jnp.where(incl, m[c:], 0.0) for m in a_b]
        a_kr = [cat0(jnp.where(strict, m[:c], 0.0), jnp.where(incl, m[c:], 0.0)) for m in a_k]
        inv = [eye + m for m in a_ab]
        pw = [bdmm(m, m, passes=1) for m in a_ab]
        yield
        for lvl in range(levels):
            if lvl < levels - 1:
                both = [bdmm(cat0(x_, p_), p_, passes=1) for x_, p_ in zip(inv, pw)]
                inv = [x_ + b_[:c] for x_, b_ in zip(inv, both)]
                pw = [b_[c:] for b_ in both]
            else:
                inv = [x_ + bdmm(x_, p_, passes=1) for x_, p_ in zip(inv, pw)]
            yield
        av = [bdmm(m, v_) for m, v_ in zip(a_kr, vv)]
        yield
        wmat = [bdmm(x_, k_, passes=1) for x_, k_ in zip(inv, kt)]
        umat = [bdmm(x_, a_[:c], passes=1) for x_, a_ in zip(inv, av)]
        yield
        rw = [bdmm(m, w_, passes=1) for m, w_ in zip(a_rb, wmat)]
        ru = [bdmm(m, u_, passes=1) for m, u_ in zip(a_rb, umat)]
        yield
        gz = [block_diagonal_of(_mm(b_, cat1(w_, u_), "tn", passes=1))
              for b_, w_, u_ in zip(btg, wmat, umat)]
        kv = [block_diagonal_of(_mm(n_, v_, "tn"))[0] for n_, v_ in zip(get(kng_s), vv)]
        for i, j in enumerate(chunks):
            rows = slice(j * c, (j + 1) * c)
            rp_s[rows, :] = rt[i] + rw[i]
            y0_s[rows, :] = ru[i] + av[i][c:]
            g_diag = jnp.where(diag, jnp.broadcast_to(etot_s[j * c:j * c + 1, :], (c, d)), 0.0)
            gt_s[rows, :] = g_diag + gz[i][0]
            zt_s[rows, :] = gz[i][1] + kv[i]

    state = [state_s[...]]

    def sequential_part(chunks):
        for j in chunks:
            rows = slice(j * c, (j + 1) * c)
            ry = bdmm(cat0(rp_s[rows, :], gt_s[rows, :]), state[0])
            y_s[rows, :] = ry[:c] + y0_s[rows, :]
            state[0] = ry[c:] + zt_s[rows, :]
            yield

    pending = iter(())
    for g0 in range(0, nchunk, WKV_GROUP):
        chunks = range(g0, min(g0 + WKV_GROUP, nchunk))
        for _ in independent_part(chunks):
            next(pending, None)
        for _ in pending:
            pass
        pending = sequential_part(chunks)
    for _ in pending:
        pass
    state_s[...] = state[0]

    y = y_s[...]
    mean = _mm_ones(y, hsum, terms=2) * (1.0 / HEAD_DIM)
    dy = y - mean
    var = _mm_ones(dy * dy, hsum, terms=2) * (1.0 / HEAD_DIM)
    yn = dy * lax.rsqrt(var + RWKV_GN_EPS) * lng_ref[...] + lnb_ref[...]
    bonus = _mm_ones(r * kn * rk_ref[...], hsum, terms=2) * v
    o_ref[...] = ((yn + bonus) * g).astype(o_ref.dtype)


def _chunk_masks(tb, c):
    i = np.arange(tb)
    same = (i[:, None] // c) == (i[None, :] // c)
    ltri = same & (i[None, :] <= i[:, None])
    return jnp.asarray(ltri, BF16), jnp.asarray(same, BF16)


def _head_sum_matrix(width):
    i = np.arange(width)
    return jnp.asarray((i[:, None] // HEAD_DIM) == (i[None, :] // HEAD_DIM), BF16)


def rwkv_time_mix(p, batch, seq, mu, w0, w_up, a0, a_up, g_up, k_k, k_a, r_k, ln_g, ln_b):
    t = batch * seq
    tb = min(WKV_BLOCK, seq)
    nblk = seq // tb
    ltri, same = _chunk_masks(min(WKV_MASK_ROWS, tb), WKV_CHUNK)
    hsum = _head_sum_matrix(RWKV_DIM)
    row = lambda a: a.reshape(1, -1)
    consts = [row(mu), row(w0), w_up, row(a0), a_up, g_up, row(k_k), row(k_a), row(r_k), row(ln_g), row(ln_b),
              ltri, same, hsum]
    buf = lambda: pltpu.VMEM((tb, RWKV_DIM), F32)
    return pl.pallas_call(
        _rwkv_kernel,
        grid=(batch, nblk),
        in_specs=[pl.BlockSpec((tb, RWKV_IN), lambda b, j: (b * nblk + j, 0))]
                 + [_const_spec(a.shape) for a in consts],
        out_specs=pl.BlockSpec((tb, RWKV_DIM), lambda b, j: (b * nblk + j, 0)),
        out_shape=jax.ShapeDtypeStruct((t, RWKV_DIM), BF16),
        scratch_shapes=[pltpu.VMEM((HEAD_DIM, RWKV_DIM), F32), pltpu.VMEM((1, RWKV_IN), F32)]
                       + [buf() for _ in range(13)],
        compiler_params=_params(("parallel", "arbitrary")),
        name="rwkv_time_mix",
    )(p, *consts)


def _attn_kernel(q_ref, k_ref, v_ref, bias_ref, o_ref, acc_s, m_s, l_s):
    seq = q_ref.shape[0]
    w = ATTN_W
    scale = HEAD_DIM ** -0.5

    def rows_of(start, dil):
        return pl.ds(start, w) if dil == 1 else pl.ds(start, w, stride=dil)

    lane = lax.broadcasted_iota(jnp.int32, (w, LANES), 1)
    head0 = lane < HEAD_DIM
    zero = jnp.zeros((), BF16)
    one = jnp.ones((), BF16)

    def group(pi, dil, g, firsts):
        rows_l, q_l, k_l, v_l = [], [], [], []
        for u, first in enumerate(firsts):
            b = g * len(firsts) + u
            start = (b % dil) + (b // dil) * (dil * w)
            rows = rows_of(start, dil)
            q = (q_ref[rows, :] * scale).astype(BF16)
            kk = k_ref[rows, :].astype(BF16)
            vv = v_ref[rows, :].astype(BF16)
            if not first:
                prev = rows_of(start - dil * w, dil)
                kk = jnp.concatenate([k_ref[prev, :].astype(BF16), kk], axis=0)
                vv = jnp.concatenate([v_ref[prev, :].astype(BF16), vv], axis=0)
            rows_l.append(rows)
            q_l.append(q)
            k_l.append(kk)
            v_l.append(vv)
        s = [[_dot_nt(jnp.where(head0 if h == 0 else ~head0, q, zero), kk)
              + (bias_ref[pi, h, :, w:] if first else bias_ref[pi, h])
              for h in range(2)] for q, kk, first in zip(q_l, k_l, firsts)]
        m = [[jnp.max(sh, axis=-1, keepdims=True) for sh in su] for su in s]
        pr = [[jnp.exp(sh - mh).astype(BF16) for sh, mh in zip(su, mu)] for su, mu in zip(s, m)]
        kmask = lambda vv: lax.broadcasted_iota(jnp.int32, vv.shape, 1) < HEAD_DIM
        res = [[_dot(pu[0], jnp.where(kmask(vv), vv, one)), _dot(pu[1], jnp.where(kmask(vv), one, vv))]
               for pu, vv in zip(pr, v_l)]
        for rows, ru, mu in zip(rows_l, res, m):
            acc_s[pi, rows, :] = jnp.where(head0, ru[0], ru[1])
            l_s[pi, rows, :] = jnp.where(head0, ru[1], ru[0])
            m_s[pi, rows, :] = jnp.where(head0, mu[0], mu[1])

    n_groups = (seq // w) // ATTN_UNROLL
    for pi, (window, dil) in enumerate(DILATED_PATTERNS):
        flags = [tuple((g * ATTN_UNROLL + u) < dil for u in range(ATTN_UNROLL)) for g in range(n_groups)]
        g0 = 0
        while g0 < n_groups:
            g1 = g0
            while g1 < n_groups and flags[g1] == flags[g0]:
                g1 += 1
            if g1 - g0 == 1:
                group(pi, dil, g0, flags[g0])
            else:
                def body(g, carry, pi=pi, dil=dil, firsts=flags[g0]):
                    group(pi, dil, g, firsts)
                    return carry
                lax.fori_loop(g0, g1, body, 0)
            g0 = g1

    mt = 256

    def merge_body(i, carry):
        rows = pl.ds(pl.multiple_of(i * mt, mt), mt)
        m0, m1, m2 = m_s[0, rows, :], m_s[1, rows, :], m_s[2, rows, :]
        mx = jnp.maximum(jnp.maximum(m0, m1), m2)
        w0, w1, w2 = jnp.exp(m0 - mx), jnp.exp(m1 - mx), jnp.exp(m2 - mx)
        num = w0 * acc_s[0, rows, :] + w1 * acc_s[1, rows, :] + w2 * acc_s[2, rows, :]
        swap = lambda x: pltpu.roll(x, HEAD_DIM, axis=1)
        den = w0 * swap(l_s[0, rows, :]) + w1 * swap(l_s[1, rows, :]) + w2 * swap(l_s[2, rows, :])
        o_ref[rows, :] = (num / den).astype(o_ref.dtype)
        return carry

    lax.fori_loop(0, seq // mt, merge_body, 0)


def _t5_bucket(dist):
    max_exact = NUM_BUCKETS // 2
    large = max_exact + (np.log(np.maximum(dist, max_exact) / max_exact)
                         / math.log(MAX_DISTANCE / max_exact) * (NUM_BUCKETS - max_exact)).astype(np.int32)
    return np.where(dist < max_exact, dist, np.minimum(large, NUM_BUCKETS - 1)).astype(np.int32)


def _attn_bias(rel_bias):
    w = ATTN_W
    i = np.arange(w)[:, None]
    j = np.arange(2 * w)[None, :]
    rel = i + w - j
    band = (rel >= 0) & (rel <= w)
    tabs = []
    for window, dil in DILATED_PATTERNS:
        bucket = _t5_bucket(np.clip(rel, 0, None) * dil)
        onehot = jnp.asarray(bucket[..., None] == np.arange(NUM_BUCKETS), F32)
        bias = jnp.einsum('ijb,bh->hij', onehot, rel_bias.astype(F32), precision=HI)
        tabs.append(jnp.where(band[None], bias, MASK_VALUE))
    return jnp.stack(tabs)


def dilated_attention(p, batch, seq, rel_bias):
    t = batch * seq
    bias = _attn_bias(rel_bias)
    col0 = RWKV_IN // LANES
    npair = ATTN_DIM // LANES
    spec = lambda off: pl.BlockSpec((seq, LANES), lambda b, hp: (b, col0 + off + hp))
    return pl.pallas_call(
        _attn_kernel,
        grid=(batch, npair),
        in_specs=[spec(0), spec(npair), spec(2 * npair),
                  pl.BlockSpec((3, 2, ATTN_W, 2 * ATTN_W), lambda b, hp: (0, hp, 0, 0))],
        out_specs=pl.BlockSpec((seq, LANES), lambda b, hp: (b, hp)),
        out_shape=jax.ShapeDtypeStruct((t, ATTN_DIM), BF16),
        scratch_shapes=[pltpu.VMEM((3, seq, LANES), F32) for _ in range(3)],
        compiler_params=_params(("parallel", "parallel")),
        name="dilated_attention",
    )(p, p, p, bias)


def _ret_kernel(q_ref, k_ref, v_ref, g_ref, cos_ref, sin_ref, dmat_ref, xi_ref, zeta_ref, gng_ref, gnb_ref,
                hsum_ref, o_ref, state_s):
    c = RET_CHUNK
    tb = q_ref.shape[0]

    @pl.when(pl.program_id(1) == 0)
    def _():
        state_s[...] = jnp.zeros_like(state_s)

    lane = lax.broadcasted_iota(jnp.int32, (c, RET_DIM), 1)
    first_half = (lane % HEAD_DIM) < (HEAD_DIM // 2)

    def rotate(x, cos, sin):
        swapped = jnp.where(first_half, pltpu.roll(x, RET_DIM - HEAD_DIM // 2, axis=1),
                            pltpu.roll(x, HEAD_DIM // 2, axis=1))
        return x * cos + swapped * sin

    heads = range(RET_HEADS)
    hsl = [slice(h * HEAD_DIM, (h + 1) * HEAD_DIM) for h in heads]
    chunk_decay = [(1.0 - 2.0 ** (-5.0 - h)) ** c for h in heads]

    def group_body(g, carry):
        rows_l, qb, kb, qx, kz, vb = [], [], [], [], [], []
        for u in range(RET_UNROLL):
            rows = pl.ds(pl.multiple_of((g * RET_UNROLL + u) * c, c), c)
            cos, sin = cos_ref[rows, :], sin_ref[rows, :]
            q = rotate(q_ref[rows, :], cos, sin)
            k = rotate(k_ref[rows, :], cos, sin) * (HEAD_DIM ** -0.5)
            rows_l.append(rows)
            qb.append(q.astype(BF16))
            kb.append(k.astype(BF16))
            qx.append((q * xi_ref[...]).astype(BF16))
            kz.append((k * zeta_ref[...]).astype(BF16))
            vb.append(v_ref[rows, :].astype(BF16))
        sc = [[(_dot_nt(qb[u][:, sl], kb[u][:, sl]) * dmat_ref[h]).astype(BF16) for h, sl in zip(heads, hsl)]
              for u in range(RET_UNROLL)]
        intra = [[_dot(sc[u][h], vb[u][:, hsl[h]]) for h in heads] for u in range(RET_UNROLL)]
        kv = [[_dot_tn(kz[u][:, sl], vb[u][:, sl]) for sl in hsl] for u in range(RET_UNROLL)]
        states = [state_s[h] for h in heads]
        ys = []
        for u in range(RET_UNROLL):
            ys.append(jnp.concatenate(
                [intra[u][h] + _dot(qx[u][:, hsl[h]], states[h].astype(BF16)) for h in heads], axis=1))
            states = [states[h] * chunk_decay[h] + kv[u][h] for h in heads]
        for h in heads:
            state_s[h] = states[h]
        hsum = hsum_ref[...]
        for rows, y in zip(rows_l, ys):
            mean = _mm_ones(y, hsum, terms=2) * (1.0 / HEAD_DIM)
            dy = y - mean
            var = _mm_ones(dy * dy, hsum, terms=2) * (1.0 / HEAD_DIM)
            yn = dy * lax.rsqrt(var + LN_EPS) * gng_ref[...] + gnb_ref[...]
            gate = g_ref[rows, :]
            o_ref[rows, :] = (gate * _sigmoid(gate) * yn).astype(o_ref.dtype)
        return carry

    lax.fori_loop(0, tb // (c * RET_UNROLL), group_body, 0)


def _ret_tables(seq):
    c = RET_CHUNK
    half = HEAD_DIM // 2
    inv = ROPE_BASE ** (-jnp.arange(half, dtype=F32) / half)
    ang = jnp.arange(seq, dtype=F32)[:, None] * inv
    cos, sin = jnp.cos(ang), jnp.sin(ang)
    cos_t = jnp.tile(jnp.concatenate([cos, cos], axis=1), (1, RET_HEADS))
    sin_t = jnp.tile(jnp.concatenate([-sin, sin], axis=1), (1, RET_HEADS))
    log_g = jnp.log1p(-jnp.exp2(-5.0 - jnp.arange(RET_HEADS, dtype=F32)))
    n = jnp.arange(c, dtype=F32)
    diff = n[:, None] - n[None, :]
    dmat = jnp.where(diff >= 0, jnp.exp(log_g[:, None, None] * jnp.maximum(diff, 0.0)), 0.0)
    zeta = jnp.exp(log_g[:, None] * (c - 1 - n))
    xi = jnp.exp(log_g[:, None] * (n + 1))
    widen = lambda tab: jnp.repeat(tab.T, HEAD_DIM, axis=1)
    return cos_t, sin_t, dmat, widen(xi), widen(zeta)


def retention(p, batch, seq, gn_g, gn_b):
    t = batch * seq
    tb = min(RET_BLOCK, seq)
    nblk = seq // tb
    cos_t, sin_t, dmat, xi, zeta = _ret_tables(seq)
    hsum = _head_sum_matrix(RET_DIM)
    col0 = (RWKV_IN + ATTN_IN) // RET_DIM
    spec = lambda off: pl.BlockSpec((tb, RET_DIM), lambda b, j: (b * nblk + j, col0 + off))
    tab = pl.BlockSpec((tb, RET_DIM), lambda b, j: (j, 0))
    consts = [dmat, xi, zeta, gn_g.reshape(1, -1), gn_b.reshape(1, -1), hsum]
    return pl.pallas_call(
        _ret_kernel,
        grid=(batch, nblk),
        in_specs=[spec(0), spec(1), spec(2), spec(3), tab, tab] + [_const_spec(a.shape) for a in consts],
        out_specs=pl.BlockSpec((tb, RET_DIM), lambda b, j: (b * nblk + j, 0)),
        out_shape=jax.ShapeDtypeStruct((t, RET_DIM), BF16),
        scratch_shapes=[pltpu.VMEM((RET_HEADS, HEAD_DIM, HEAD_DIM), F32)],
        compiler_params=_params(("parallel", "arbitrary")),
        name="retention",
    )(p, p, p, p, cos_t, sin_t, *consts)


def _outproj_kernel(ya_ref, yb_ref, yc_ref, x_ref, w_ref, g_ref, b_ref, *rest, alpha):
    acc = _dot(ya_ref[...], w_ref[0:RWKV_DIM, :])
    acc += _dot(yb_ref[...], w_ref[RWKV_DIM:RWKV_DIM + ATTN_DIM, :])
    acc += _dot(yc_ref[...], w_ref[RWKV_DIM + ATTN_DIM:, :])
    h = _layer_norm(alpha * x_ref[...] + acc, g_ref[...], b_ref[...])
    if len(rest) == 1:
        (o_ref,) = rest
    else:
        router_ref, o_ref, logit_ref = rest
        logit_ref[...] = _dot_nt(router_ref[...].astype(BF16), h.astype(BF16))
    o_ref[...] = h


def out_projection_ln(ya, yb, yc, x2d, w_bf16, layer, g, b, alpha, router=None):
    t, d = x2d.shape
    tm = min(ROW_TILE, t)
    rows = lambda width: pl.BlockSpec((tm, width), lambda i: (i, 0))
    in_specs = [rows(RWKV_DIM), rows(ATTN_DIM), rows(RET_DIM), rows(d), _layer_spec(w_bf16, layer),
                _const_spec((1, d)), _const_spec((1, d))]
    args = [ya, yb, yc, x2d, w_bf16, g.reshape(1, -1), b.reshape(1, -1)]
    out_specs, out_shape = rows(d), jax.ShapeDtypeStruct((t, d), F32)
    if router is not None:
        in_specs.append(_const_spec((N_EXPERTS, d)))
        args.append(router.T)
        out_specs = [out_specs, pl.BlockSpec((N_EXPERTS, tm), lambda i: (0, i))]
        out_shape = [out_shape, jax.ShapeDtypeStruct((N_EXPERTS, t), F32)]
    return pl.pallas_call(
        functools.partial(_outproj_kernel, alpha=alpha),
        grid=(t // tm,),
        in_specs=in_specs,
        out_specs=out_specs,
        out_shape=out_shape,
        compiler_params=_params(("parallel",)),
        name="out_projection_ln",
    )(*args)


def _swiglu(xb, wg, wu, wd, acc, f_chunk):
    ff = wg.shape[-1]
    starts = list(range(0, ff, f_chunk))
    gate_up = lambda f0: (_dot(xb, wg[:, f0:f0 + f_chunk]), _dot(xb, wu[:, f0:f0 + f_chunk]))
    nxt = gate_up(starts[0])
    for n, f0 in enumerate(starts):
        gate, up = nxt
        if n + 1 < len(starts):
            nxt = gate_up(starts[n + 1])
        hid = (gate * _sigmoid(gate) * up).astype(BF16)
        part = _dot(hid, wd[f0:f0 + f_chunk, :])
        acc = part if acc is None else acc + part
    return acc


def _ffn_kernel(x_ref, wg_ref, wu_ref, wd_ref, g_ref, b_ref, o_ref, *, alpha, f_chunk):
    x = x_ref[...]
    acc = _swiglu(x.astype(BF16), wg_ref, wu_ref, wd_ref, alpha * x, f_chunk)
    o_ref[...] = _layer_norm(acc, g_ref[...], b_ref[...])


def dense_ffn_ln(x2d, wg, wu, wd, layer, g, b, alpha):
    t, d = x2d.shape
    tm = min(ROW_TILE, t)
    return pl.pallas_call(
        functools.partial(_ffn_kernel, alpha=alpha, f_chunk=FFN_SUBCHUNK),
        grid=(t // tm,),
        in_specs=[pl.BlockSpec((tm, d), lambda i: (i, 0)), _layer_spec(wg, layer), _layer_spec(wu, layer),
                  _layer_spec(wd, layer), _const_spec((1, d)), _const_spec((1, d))],
        out_specs=pl.BlockSpec((tm, d), lambda i: (i, 0)),
        out_shape=jax.ShapeDtypeStruct((t, d), F32),
        compiler_params=_params(("parallel",)),
        name="dense_ffn_ln",
    )(x2d, wg, wu, wd, g.reshape(1, -1), b.reshape(1, -1))


def _expert_kernel(blk_e_ref, used_ref, x_ref, wg_ref, wu_ref, wd_ref, *rest, blk_off):
    o_ref, acc_s = rest[-2:]
    i, j = pl.program_id(0) + blk_off, pl.program_id(1)
    last = pl.num_programs(1) - 1

    @pl.when(i < used_ref[0])
    def _():
        part = _swiglu(x_ref[...].astype(BF16), wg_ref.at[0], wu_ref.at[0], wd_ref.at[0], None, FFN_SUBCHUNK)

        @pl.when(j == 0)
        def _():
            acc_s[...] = part

        @pl.when((j > 0) & (j < last))
        def _():
            acc_s[...] += part

        @pl.when(j == last)
        def _():
            o_ref[...] = (acc_s[...] + part).astype(o_ref.dtype)


def expert_ffn(xs, blk_e, n_used, wg, wu, wd, layer, blk_off, out_rows, earlier=None, f_chunk=MOE_F_BLOCK):
    rows, d = xs.shape
    ff = wg.shape[3]
    tm = MOE_TILE
    in_specs = [pl.BlockSpec((tm, d), lambda i, j, be, nu: (i, 0)),
                pl.BlockSpec((None, 1, d, f_chunk), lambda i, j, be, nu: (layer, be[i + blk_off], 0, j)),
                pl.BlockSpec((None, 1, d, f_chunk), lambda i, j, be, nu: (layer, be[i + blk_off], 0, j)),
                pl.BlockSpec((None, 1, f_chunk, d), lambda i, j, be, nu: (layer, be[i + blk_off], j, 0))]
    args = [blk_e, n_used, xs, wg, wu, wd]
    aliases = {}
    if earlier is not None:
        in_specs.append(pl.BlockSpec(memory_space=pl.ANY))
        args.append(earlier)
        aliases = {len(args) - 1: 0}
    assert ff // f_chunk >= 2
    grid_spec = pltpu.PrefetchScalarGridSpec(
        num_scalar_prefetch=2,
        grid=(rows // tm, ff // f_chunk),
        in_specs=in_specs,
        out_specs=pl.BlockSpec((tm, d), lambda i, j, be, nu: (i + blk_off, 0)),
        scratch_shapes=[pltpu.VMEM((tm, d), F32)],
    )
    return pl.pallas_call(
        functools.partial(_expert_kernel, blk_off=blk_off),
        grid_spec=grid_spec,
        out_shape=jax.ShapeDtypeStruct((out_rows, d), BF16),
        input_output_aliases=aliases,
        compiler_params=_params(("parallel", "arbitrary")),
        name="expert_ffn",
    )(*args)


def _combine_ln_kernel(x_ref, y0_ref, y1_ref, gate_ref, g_ref, b_ref, o_ref, *, alpha):
    gates = gate_ref[...]
    f = y0_ref[...] * gates[:, 0:1] + y1_ref[...] * gates[:, 1:2]
    o_ref[...] = _layer_norm(alpha * x_ref[...] + f, g_ref[...], b_ref[...])


def combine_ln(x2d, y0, y1, gates, g, b, alpha):
    t, d = x2d.shape
    tm = min(ROW_TILE, t)
    rows = pl.BlockSpec((tm, d), lambda i: (i, 0))
    return pl.pallas_call(
        functools.partial(_combine_ln_kernel, alpha=alpha),
        grid=(t // tm,),
        in_specs=[rows, rows, rows, pl.BlockSpec((tm, TOP_K), lambda i: (i, 0)),
                  _const_spec((1, d)), _const_spec((1, d))],
        out_specs=rows,
        out_shape=jax.ShapeDtypeStruct((t, d), F32),
        compiler_params=_params(("parallel",)),
        name="combine_ln",
    )(x2d, y0, y1, gates, g.reshape(1, -1), b.reshape(1, -1))


def moe_ffn_ln(x2d, logits_t, wg, wu, wd, layer, g, b, alpha):
    t, d = x2d.shape
    tm = MOE_TILE
    logits = logits_t.T
    top_val, top_idx = lax.top_k(logits, TOP_K)
    gates = jax.nn.softmax(top_val, axis=-1)
    member = (top_idx[:, :, None] == jnp.arange(N_EXPERTS)[None, None, :]).any(axis=1)
    counts = member.sum(axis=0).astype(jnp.int32)
    rank = jnp.cumsum(member.astype(jnp.int32), axis=0) - member.astype(jnp.int32)
    padded = (counts + tm - 1) // tm * tm
    pad_ends = jnp.cumsum(padded)
    pad_starts = pad_ends - padded
    pos = jnp.take_along_axis(pad_starts[None, :] + rank, top_idx, axis=1)
    rows = t * TOP_K + N_EXPERTS * tm
    n_blk = rows // tm
    order = jnp.argsort(top_idx.reshape(-1), stable=True).astype(jnp.int32)
    tok_sorted = order // TOP_K
    starts = jnp.cumsum(counts) - counts
    r = jnp.arange(rows, dtype=jnp.int32)
    row_e = jnp.minimum(jnp.searchsorted(pad_ends, r, side='right'), N_EXPERTS - 1).astype(jnp.int32)
    within = r - pad_starts[row_e]
    src = jnp.where(within < counts[row_e], tok_sorted[jnp.minimum(starts[row_e] + within, t * TOP_K - 1)], 0)
    blk_e = row_e[::tm]
    n_used = (pad_ends[-1:] // tm).astype(jnp.int32)
    ys = expert_ffn(x2d[src], blk_e, n_used, wg, wu, wd, layer, 0, rows)
    return combine_ln(x2d, ys[pos[:, 0]], ys[pos[:, 1]], gates, g, b, alpha)


def kernel(x, w_in, w_out, rwkv_mu, rwkv_w0, rwkv_w_up, rwkv_a0, rwkv_a_up, rwkv_g_up, rwkv_k_k, rwkv_k_a,
           rwkv_r_k, rwkv_ln_g, rwkv_ln_b, ret_gn_g, ret_gn_b, rel_bias, ln_g, ln_b, ffn_w_gate, ffn_w_up,
           ffn_w_down, moe_router, moe_w_gate, moe_w_up, moe_w_down):
    batch, seq, d = x.shape
    depth = w_in.shape[0]
    alpha = (2 * depth) ** 0.25
    h = x.reshape(batch * seq, d)
    w_in, w_out = w_in.astype(BF16), w_out.astype(BF16)
    ffn_w_gate, ffn_w_up, ffn_w_down = ffn_w_gate.astype(BF16), ffn_w_up.astype(BF16), ffn_w_down.astype(BF16)
    for layer in range(depth):
        p = in_projection(h, w_in, layer)
        ya = rwkv_time_mix(p, batch, seq, rwkv_mu[layer], rwkv_w0[layer], rwkv_w_up[layer], rwkv_a0[layer],
                           rwkv_a_up[layer], rwkv_g_up[layer], rwkv_k_k[layer], rwkv_k_a[layer],
                           rwkv_r_k[layer], rwkv_ln_g[layer], rwkv_ln_b[layer])
        yb = dilated_attention(p, batch, seq, rel_bias)
        yc = retention(p, batch, seq, ret_gn_g[layer], ret_gn_b[layer])
        j = layer // 2
        if layer % 2 == 0:
            h = out_projection_ln(ya, yb, yc, h, w_out, layer, ln_g[layer, 0], ln_b[layer, 0], alpha)
            h = dense_ffn_ln(h, ffn_w_gate, ffn_w_up, ffn_w_down, j, ln_g[layer, 1], ln_b[layer, 1], alpha)
        else:
            h, logits_t = out_projection_ln(ya, yb, yc, h, w_out, layer, ln_g[layer, 0], ln_b[layer, 0], alpha,
                                            router=moe_router[j])
            h = moe_ffn_ln(h, logits_t, layer_to_bf16(moe_w_gate, j), layer_to_bf16(moe_w_up, j),
                           layer_to_bf16(moe_w_down, j), 0, ln_g[layer, 1], ln_b[layer, 1], alpha)
    return h.reshape(batch, seq, d)
```

```python
import functools
import math

import numpy as np
import jax
import jax.numpy as jnp
from jax import lax
from jax.experimental import pallas as pl
from jax.experimental.pallas import tpu as pltpu

F32 = jnp.float32
BF16 = jnp.bfloat16
HI = lax.Precision.HIGHEST

HEAD_DIM = 64
RWKV_HEADS = 4
ATTN_HEADS = 8
RET_HEADS = 4
RWKV_DIM = RWKV_HEADS * HEAD_DIM
ATTN_DIM = ATTN_HEADS * HEAD_DIM
RET_DIM = RET_HEADS * HEAD_DIM
DECAY_LORA = 64
ICL_LORA = 64
GATE_LORA = 128
RWKV_IN = 3 * RWKV_DIM + DECAY_LORA + ICL_LORA + GATE_LORA
ATTN_IN = 3 * ATTN_DIM
RET_IN = 4 * RET_DIM
RWKV_GN_EPS = 64e-5
DECAY_SCALE = math.exp(-0.5)
DILATED_PATTERNS = ((128, 1), (512, 4), (2048, 16))
NUM_BUCKETS = 32
MAX_DISTANCE = 2048
ROPE_BASE = 10000.0
N_EXPERTS = 8
TOP_K = 2
LN_EPS = 1e-5

LANES = 128
WKV_CHUNK = 64
WKV_BLOCK = 512
WKV_MASK_ROWS = 256
WKV_GROUP = 4
ATTN_W = 128
ATTN_UNROLL = 4
RET_CHUNK = 128
RET_BLOCK = 1024
RET_UNROLL = 4
ROW_TILE = 512
MOE_TILE = 1024
MOE_F_BLOCK = 512
FFN_SUBCHUNK = 256
MASK_VALUE = -1e30
VMEM_LIMIT = 56 * 1024 * 1024


def _dot(a, b, prec=None):
    return jnp.dot(a, b, preferred_element_type=F32, precision=prec)


def _dot_nt(a, b, prec=None):
    return lax.dot_general(a, b, (((1,), (1,)), ((), ())), preferred_element_type=F32, precision=prec)


def _dot_tn(a, b, prec=None):
    return lax.dot_general(a, b, (((0,), (0,)), ((), ())), preferred_element_type=F32, precision=prec)


_DIMS = {"nn": (((1,), (0,)), ((), ())), "nt": (((1,), (1,)), ((), ())), "tn": (((0,), (0,)), ((), ()))}


def _split(x, terms):
    parts = []
    for _ in range(terms - 1):
        hi = x.astype(BF16)
        parts.append(hi)
        x = x - hi.astype(F32)
    parts.append(x.astype(BF16))
    return parts


def _mm(a, b, kind="nn", passes=3):
    dg = lambda p, q: lax.dot_general(p, q, _DIMS[kind], preferred_element_type=F32)
    if passes == 1:
        return dg(a.astype(BF16), b.astype(BF16))
    ah, al = _split(a, 2)
    bh, bl = _split(b, 2)
    return dg(ah, bh) + (dg(al, bh) + dg(ah, bl))


def _mm_ones(x, ones_bf16, ones_first=False, terms=3):
    parts = _split(x, terms)
    m, n = x.shape
    if ones_first:
        full = _dot(ones_bf16, jnp.concatenate(parts, axis=1))
        out = [full[:, i * n:(i + 1) * n] for i in range(terms)]
    else:
        full = _dot(jnp.concatenate(parts, axis=0), ones_bf16)
        out = [full[i * m:(i + 1) * m] for i in range(terms)]
    acc = out[-1]
    for o in reversed(out[:-1]):
        acc = acc + o
    return acc


def _sigmoid(x):
    return 1.0 / (1.0 + jnp.exp(-x))


def _layer_norm(z, g, b):
    mu = jnp.mean(z, axis=-1, keepdims=True)
    d = z - mu
    var = jnp.mean(d * d, axis=-1, keepdims=True)
    return d * lax.rsqrt(var + LN_EPS) * g + b


def _params(sem, vmem=VMEM_LIMIT):
    return pltpu.CompilerParams(dimension_semantics=sem, vmem_limit_bytes=vmem)


def _const_spec(shape):
    nd = len(shape)
    return pl.BlockSpec(shape, lambda *_: (0,) * nd)


def _layer_spec(stacked, layer):
    nd = stacked.ndim - 1
    return pl.BlockSpec((None,) + stacked.shape[1:], lambda *_: (layer,) + (0,) * nd)


def _inproj_kernel(x_ref, w_ref, o_ref, *, n_chunk):
    xb = x_ref[...].astype(BF16)
    for n0 in range(0, o_ref.shape[1], n_chunk):
        o_ref[:, n0:n0 + n_chunk] = _dot(xb, w_ref[:, n0:n0 + n_chunk])


def in_projection(x2d, w_bf16, layer):
    t, d = x2d.shape
    n = w_bf16.shape[2]
    tm = min(ROW_TILE, t)
    return pl.pallas_call(
        functools.partial(_inproj_kernel, n_chunk=512),
        grid=(t // tm,),
        in_specs=[pl.BlockSpec((tm, d), lambda i: (i, 0)), _layer_spec(w_bf16, layer)],
        out_specs=pl.BlockSpec((tm, n), lambda i: (i, 0)),
        out_shape=jax.ShapeDtypeStruct((t, n), F32),
        compiler_params=_params(("parallel",)),
        name="in_projection",
    )(x2d, w_bf16)


def _rwkv_kernel(p_ref, mu_ref, w0_ref, wup_ref, a0_ref, aup_ref, gup_ref, kk_ref, ka_ref, rk_ref,
                 lng_ref, lnb_ref, ltri_ref, same_ref, hsum_ref, o_ref,
                 state_s, prev_s, kt_s, rt_s, bt_s, kn_s, v_s, btg_s, kng_s, etot_s, y_s, rp_s, y0_s, gt_s, zt_s):
    c = WKV_CHUNK
    tb = p_ref.shape[0]
    d = RWKV_DIM
    assert c == HEAD_DIM

    @pl.when(pl.program_id(1) == 0)
    def _():
        state_s[...] = jnp.zeros_like(state_s)
        prev_s[...] = jnp.zeros_like(prev_s)

    p = p_ref[...]
    row = lax.broadcasted_iota(jnp.int32, p.shape, 0)
    shifted = jnp.where(row == 0, prev_s[...], pltpu.roll(p, 1, axis=0))
    prev_s[...] = p[tb - 1:tb, :]
    ps = p + (shifted - p) * mu_ref[...]
    r = ps[:, 0:d]
    k = ps[:, d:2 * d]
    v = ps[:, 2 * d:3 * d]
    xw = ps[:, 3 * d:3 * d + DECAY_LORA]
    xa = ps[:, 3 * d + DECAY_LORA:3 * d + DECAY_LORA + ICL_LORA]
    xg = ps[:, 3 * d + DECAY_LORA + ICL_LORA:]

    hsum = hsum_ref[...]
    logw = -DECAY_SCALE * _sigmoid(w0_ref[...] + _mm(jnp.tanh(xw), wup_ref[...]))
    a = _sigmoid(a0_ref[...] + _mm(xa, aup_ref[...]))
    g = _dot(_sigmoid(xg).astype(BF16), gup_ref[...].astype(BF16))
    kap = k * kk_ref[...]
    kap = kap / jnp.maximum(jnp.sqrt(_mm_ones(kap * kap, hsum)), 1e-12)
    kn = k * (1.0 + (a - 1.0) * ka_ref[...])
    mb = ltri_ref.shape[0]
    ones2 = jnp.concatenate([ltri_ref[...], same_ref[...]], axis=0)
    sums = [_mm_ones(logw[s0:s0 + mb], ones2, ones_first=True) for s0 in range(0, tb, mb)]
    cum = jnp.concatenate([s[:mb] for s in sums], axis=0)
    tot = jnp.concatenate([s[mb:] for s in sums], axis=0)
    e_neg = jnp.exp(-cum)
    e_rem = jnp.exp(tot - cum)
    nb = -(a * kap)
    kt_s[...] = kap * jnp.exp(cum - logw)
    rt_s[...] = r * jnp.exp(cum)
    bt_s[...] = nb * e_neg
    kn_s[...] = kn * e_neg
    btg_s[...] = nb * e_rem
    kng_s[...] = kn * e_rem
    etot_s[...] = jnp.exp(tot)
    v_s[...] = v

    nh = RWKV_HEADS
    ri = lax.broadcasted_iota(jnp.int32, (c, d), 0)
    ci = lax.broadcasted_iota(jnp.int32, (c, d), 1) % HEAD_DIM
    strict = ci < ri
    incl = ci <= ri
    diag = ci == ri
    eye = diag.astype(F32)
    bi = lax.broadcasted_iota(jnp.int32, (d, d), 0) // HEAD_DIM
    bj = lax.broadcasted_iota(jnp.int32, (d, d), 1) // HEAD_DIM
    blocks = bi == bj
    zero16 = jnp.zeros((), BF16)

    def expand(x16):
        return jnp.where(blocks, jnp.concatenate([x16] * nh, axis=0), zero16)

    def bdmm(a, y, kind="nn", passes=3):
        dg = lambda p_, q_: lax.dot_general(p_, q_, _DIMS[kind], preferred_element_type=F32)
        if passes == 1:
            return dg(a.astype(BF16), expand(y.astype(BF16)))
        ah, al = _split(a, 2)
        yh, yl = _split(y, 2)
        m = a.shape[0]
        both = dg(jnp.concatenate([ah, al], axis=0), expand(yh))
        return both[:m] + (both[m:] + dg(ah, expand(yl)))

    def block_diagonal_of(full):
        outs = []
        for n0 in range(0, full.shape[1], d):
            m = jnp.where(blocks, full[:, n0:n0 + d], 0.0)
            acc = m[0:c]
            for h in range(1, nh):
                acc = acc + m[h * c:(h + 1) * c]
            outs.append(acc)
        return outs

    nchunk = tb // c
    cat0 = lambda x, y: jnp.concatenate([x, y], axis=0)
    cat1 = lambda x, y: jnp.concatenate([x, y], axis=1)
    levels = int(math.log2(c)) - 1

    def independent_part(chunks):
        get = lambda ref: [ref[j * c:(j + 1) * c, :] for j in chunks]
        kt, rt, vv, btg = get(kt_s), get(rt_s), get(v_s), get(btg_s)
        lhs = [cat0(k_, r_) for k_, r_ in zip(kt, rt)]
        a_b = [bdmm(l_, b_, "nt") for l_, b_ in zip(lhs, get(bt_s))]
        yield
        a_k = [bdmm(l_, n_, "nt") for l_, n_ in zip(lhs, get(kn_s))]
        yield
        a_ab = [jnp.where(strict, m[:c], 0.0) for m in a_b]
        a_rb = [jnp.where(incl, m[c:], 0.0) for m in a_b]
        a_kr = [cat0(jnp.where(strict, m[:c], 0.0), jnp.where(incl, m[c:], 0.0)) for m in a_k]
        inv = [eye + m for m in a_ab]
        pw = [bdmm(m, m, passes=1) for m in a_ab]
        yield
        for lvl in range(levels):
            if lvl < levels - 1:
                both = [bdmm(cat0(x_, p_), p_, passes=1) for x_, p_ in zip(inv, pw)]
                inv = [x_ + b_[:c] for x_, b_ in zip(inv, both)]
                pw = [b_[c:] for b_ in both]
            else:
                inv = [x_ + bdmm(x_, p_, passes=1) for x_, p_ in zip(inv, pw)]
            yield
        av = [bdmm(m, v_) for m, v_ in zip(a_kr, vv)]
        yield
        wmat = [bdmm(x_, k_, passes=1) for x_, k_ in zip(inv, kt)]
        umat = [bdmm(x_, a_[:c], passes=1) for x_, a_ in zip(inv, av)]
        yield
        rw = [bdmm(m, w_, passes=1) for m, w_ in zip(a_rb, wmat)]
        ru = [bdmm(m, u_, passes=1) for m, u_ in zip(a_rb, umat)]
        yield
        gz = [block_diagonal_of(_mm(b_, cat1(w_, u_), "tn", passes=1))
              for b_, w_, u_ in zip(btg, wmat, umat)]
        kv = [block_diagonal_of(_mm(n_, v_, "tn"))[0] for n_, v_ in zip(get(kng_s), vv)]
        for i, j in enumerate(chunks):
            rows = slice(j * c, (j + 1) * c)
            rp_s[rows, :] = rt[i] + rw[i]
            y0_s[rows, :] = ru[i] + av[i][c:]
            g_diag = jnp.where(diag, jnp.broadcast_to(etot_s[j * c:j * c + 1, :], (c, d)), 0.0)
            gt_s[rows, :] = g_diag + gz[i][0]
            zt_s[rows, :] = gz[i][1] + kv[i]

    state = [state_s[...]]

    def sequential_part(chunks):
        for j in chunks:
            rows = slice(j * c, (j + 1) * c)
            ry = bdmm(cat0(rp_s[rows, :], gt_s[rows, :]), state[0])
            y_s[rows, :] = ry[:c] + y0_s[rows, :]
            state[0] = ry[c:] + zt_s[rows, :]
            yield

    pending = iter(())
    for g0 in range(0, nchunk, WKV_GROUP):
        chunks = range(g0, min(g0 + WKV_GROUP, nchunk))
        for _ in independent_part(chunks):
            next(pending, None)
        for _ in pending:
            pass
        pending = sequential_part(chunks)
    for _ in pending:
        pass
    state_s[...] = state[0]

    y = y_s[...]
    mean = _mm_ones(y, hsum, terms=2) * (1.0 / HEAD_DIM)
    dy = y - mean
    var = _mm_ones(dy * dy, hsum, terms=2) * (1.0 / HEAD_DIM)
    yn = dy * lax.rsqrt(var + RWKV_GN_EPS) * lng_ref[...] + lnb_ref[...]
    bonus = _mm_ones(r * kn * rk_ref[...], hsum, terms=2) * v
    o_ref[...] = ((yn + bonus) * g).astype(o_ref.dtype)


def _chunk_masks(tb, c):
    i = np.arange(tb)
    same = (i[:, None] // c) == (i[None, :] // c)
    ltri = same & (i[None, :] <= i[:, None])
    return jnp.asarray(ltri, BF16), jnp.asarray(same, BF16)


def _head_sum_matrix(width):
    i = np.arange(width)
    return jnp.asarray((i[:, None] // HEAD_DIM) == (i[None, :] // HEAD_DIM), BF16)


def rwkv_time_mix(p, batch, seq, mu, w0, w_up, a0, a_up, g_up, k_k, k_a, r_k, ln_g, ln_b):
    t = batch * seq
    tb = min(WKV_BLOCK, seq)
    nblk = seq // tb
    ltri, same = _chunk_masks(min(WKV_MASK_ROWS, tb), WKV_CHUNK)
    hsum = _head_sum_matrix(RWKV_DIM)
    row = lambda a: a.reshape(1, -1)
    consts = [row(mu), row(w0), w_up, row(a0), a_up, g_up, row(k_k), row(k_a), row(r_k), row(ln_g), row(ln_b),
              ltri, same, hsum]
    buf = lambda: pltpu.VMEM((tb, RWKV_DIM), F32)
    return pl.pallas_call(
        _rwkv_kernel,
        grid=(batch, nblk),
        in_specs=[pl.BlockSpec((tb, RWKV_IN), lambda b, j: (b * nblk + j, 0))]
                 + [_const_spec(a.shape) for a in consts],
        out_specs=pl.BlockSpec((tb, RWKV_DIM), lambda b, j: (b * nblk + j, 0)),
        out_shape=jax.ShapeDtypeStruct((t, RWKV_DIM), BF16),
        scratch_shapes=[pltpu.VMEM((HEAD_DIM, RWKV_DIM), F32), pltpu.VMEM((1, RWKV_IN), F32)]
                       + [buf() for _ in range(13)],
        compiler_params=_params(("parallel", "arbitrary")),
        name="rwkv_time_mix",
    )(p, *consts)


def _attn_kernel(q_ref, k_ref, v_ref, bias_ref, o_ref, acc_s, m_s, l_s):
    seq = q_ref.shape[0]
    w = ATTN_W
    scale = HEAD_DIM ** -0.5

    def rows_of(start, dil):
        return pl.ds(start, w) if dil == 1 else pl.ds(start, w, stride=dil)

    lane = lax.broadcasted_iota(jnp.int32, (w, LANES), 1)
    head0 = lane < HEAD_DIM
    zero = jnp.zeros((), BF16)
    one = jnp.ones((), BF16)

    def group(pi, dil, g, firsts):
        rows_l, q_l, k_l, v_l = [], [], [], []
        for u, first in enumerate(firsts):
            b = g * len(firsts) + u
            start = (b % dil) + (b // dil) * (dil * w)
            rows = rows_of(start, dil)
            q = (q_ref[rows, :] * scale).astype(BF16)
            kk = k_ref[rows, :].astype(BF16)
            vv = v_ref[rows, :].astype(BF16)
            if not first:
                prev = rows_of(start - dil * w, dil)
                kk = jnp.concatenate([k_ref[prev, :].astype(BF16), kk], axis=0)
                vv = jnp.concatenate([v_ref[prev, :].astype(BF16), vv], axis=0)
            rows_l.append(rows)
            q_l.append(q)
            k_l.append(kk)
            v_l.append(vv)
        s = [[_dot_nt(jnp.where(head0 if h == 0 else ~head0, q, zero), kk)
              + (bias_ref[pi, h, :, w:] if first else bias_ref[pi, h])
              for h in range(2)] for q, kk, first in zip(q_l, k_l, firsts)]
        m = [[jnp.max(sh, axis=-1, keepdims=True) for sh in su] for su in s]
        pr = [[jnp.exp(sh - mh).astype(BF16) for sh, mh in zip(su, mu)] for su, mu in zip(s, m)]
        kmask = lambda vv: lax.broadcasted_iota(jnp.int32, vv.shape, 1) < HEAD_DIM
        res = [[_dot(pu[0], jnp.where(kmask(vv), vv, one)), _dot(pu[1], jnp.where(kmask(vv), one, vv))]
               for pu, vv in zip(pr, v_l)]
        for rows, ru, mu in zip(rows_l, res, m):
            acc_s[pi, rows, :] = jnp.where(head0, ru[0], ru[1])
            l_s[pi, rows, :] = jnp.where(head0, ru[1], ru[0])
            m_s[pi, rows, :] = jnp.where(head0, mu[0], mu[1])

    n_groups = (seq // w) // ATTN_UNROLL
    for pi, (window, dil) in enumerate(DILATED_PATTERNS):
        flags = [tuple((g * ATTN_UNROLL + u) < dil for u in range(ATTN_UNROLL)) for g in range(n_groups)]
        g0 = 0
        while g0 < n_groups:
            g1 = g0
            while g1 < n_groups and flags[g1] == flags[g0]:
                g1 += 1
            if g1 - g0 == 1:
                group(pi, dil, g0, flags[g0])
            else:
                def body(g, carry, pi=pi, dil=dil, firsts=flags[g0]):
                    group(pi, dil, g, firsts)
                    return carry
                lax.fori_loop(g0, g1, body, 0)
            g0 = g1

    mt = 256

    def merge_body(i, carry):
        rows = pl.ds(pl.multiple_of(i * mt, mt), mt)
        m0, m1, m2 = m_s[0, rows, :], m_s[1, rows, :], m_s[2, rows, :]
        mx = jnp.maximum(jnp.maximum(m0, m1), m2)
        w0, w1, w2 = jnp.exp(m0 - mx), jnp.exp(m1 - mx), jnp.exp(m2 - mx)
        num = w0 * acc_s[0, rows, :] + w1 * acc_s[1, rows, :] + w2 * acc_s[2, rows, :]
        swap = lambda x: pltpu.roll(x, HEAD_DIM, axis=1)
        den = w0 * swap(l_s[0, rows, :]) + w1 * swap(l_s[1, rows, :]) + w2 * swap(l_s[2, rows, :])
        o_ref[rows, :] = (num / den).astype(o_ref.dtype)
        return carry

    lax.fori_loop(0, seq // mt, merge_body, 0)


def _t5_bucket(dist):
    max_exact = NUM_BUCKETS // 2
    large = max_exact + (np.log(np.maximum(dist, max_exact) / max_exact)
                         / math.log(MAX_DISTANCE / max_exact) * (NUM_BUCKETS - max_exact)).astype(np.int32)
    return np.where(dist < max_exact, dist, np.minimum(large, NUM_BUCKETS - 1)).astype(np.int32)


def _attn_bias(rel_bias):
    w = ATTN_W
    i = np.arange(w)[:, None]
    j = np.arange(2 * w)[None, :]
    rel = i + w - j
    band = (rel >= 0) & (rel <= w)
    tabs = []
    for window, dil in DILATED_PATTERNS:
        bucket = _t5_bucket(np.clip(rel, 0, None) * dil)
        onehot = jnp.asarray(bucket[..., None] == np.arange(NUM_BUCKETS), F32)
        bias = jnp.einsum('ijb,bh->hij', onehot, rel_bias.astype(F32), precision=HI)
        tabs.append(jnp.where(band[None], bias, MASK_VALUE))
    return jnp.stack(tabs)


def dilated_attention(p, batch, seq, rel_bias):
    t = batch * seq
    bias = _attn_bias(rel_bias)
    col0 = RWKV_IN // LANES
    npair = ATTN_DIM // LANES
    spec = lambda off: pl.BlockSpec((seq, LANES), lambda b, hp: (b, col0 + off + hp))
    return pl.pallas_call(
        _attn_kernel,
        grid=(batch, npair),
        in_specs=[spec(0), spec(npair), spec(2 * npair),
                  pl.BlockSpec((3, 2, ATTN_W, 2 * ATTN_W), lambda b, hp: (0, hp, 0, 0))],
        out_specs=pl.BlockSpec((seq, LANES), lambda b, hp: (b, hp)),
        out_shape=jax.ShapeDtypeStruct((t, ATTN_DIM), BF16),
        scratch_shapes=[pltpu.VMEM((3, seq, LANES), F32) for _ in range(3)],
        compiler_params=_params(("parallel", "parallel")),
        name="dilated_attention",
    )(p, p, p, bias)


def _ret_kernel(q_ref, k_ref, v_ref, g_ref, cos_ref, sin_ref, dmat_ref, xi_ref, zeta_ref, gng_ref, gnb_ref,
                hsum_ref, o_ref, state_s):
    c = RET_CHUNK
    tb = q_ref.shape[0]

    @pl.when(pl.program_id(1) == 0)
    def _():
        state_s[...] = jnp.zeros_like(state_s)

    lane = lax.broadcasted_iota(jnp.int32, (c, RET_DIM), 1)
    first_half = (lane % HEAD_DIM) < (HEAD_DIM // 2)

    def rotate(x, cos, sin):
        swapped = jnp.where(first_half, pltpu.roll(x, RET_DIM - HEAD_DIM // 2, axis=1),
                            pltpu.roll(x, HEAD_DIM // 2, axis=1))
        return x * cos + swapped * sin

    heads = range(RET_HEADS)
    hsl = [slice(h * HEAD_DIM, (h + 1) * HEAD_DIM) for h in heads]
    chunk_decay = [(1.0 - 2.0 ** (-5.0 - h)) ** c for h in heads]

    def group_body(g, carry):
        rows_l, qb, kb, qx, kz, vb = [], [], [], [], [], []
        for u in range(RET_UNROLL):
            rows = pl.ds(pl.multiple_of((g * RET_UNROLL + u) * c, c), c)
            cos, sin = cos_ref[rows, :], sin_ref[rows, :]
            q = rotate(q_ref[rows, :], cos, sin)
            k = rotate(k_ref[rows, :], cos, sin) * (HEAD_DIM ** -0.5)
            rows_l.append(rows)
            qb.append(q.astype(BF16))
            kb.append(k.astype(BF16))
            qx.append((q * xi_ref[...]).astype(BF16))
            kz.append((k * zeta_ref[...]).astype(BF16))
            vb.append(v_ref[rows, :].astype(BF16))
        sc = [[(_dot_nt(qb[u][:, sl], kb[u][:, sl]) * dmat_ref[h]).astype(BF16) for h, sl in zip(heads, hsl)]
              for u in range(RET_UNROLL)]
        intra = [[_dot(sc[u][h], vb[u][:, hsl[h]]) for h in heads] for u in range(RET_UNROLL)]
        kv = [[_dot_tn(kz[u][:, sl], vb[u][:, sl]) for sl in hsl] for u in range(RET_UNROLL)]
        states = [state_s[h] for h in heads]
        ys = []
        for u in range(RET_UNROLL):
            ys.append(jnp.concatenate(
                [intra[u][h] + _dot(qx[u][:, hsl[h]], states[h].astype(BF16)) for h in heads], axis=1))
            states = [states[h] * chunk_decay[h] + kv[u][h] for h in heads]
        for h in heads:
            state_s[h] = states[h]
        hsum = hsum_ref[...]
        for rows, y in zip(rows_l, ys):
            mean = _mm_ones(y, hsum, terms=2) * (1.0 / HEAD_DIM)
            dy = y - mean
            var = _mm_ones(dy * dy, hsum, terms=2) * (1.0 / HEAD_DIM)
            yn = dy * lax.rsqrt(var + LN_EPS) * gng_ref[...] + gnb_ref[...]
            gate = g_ref[rows, :]
            o_ref[rows, :] = (gate * _sigmoid(gate) * yn).astype(o_ref.dtype)
        return carry

    lax.fori_loop(0, tb // (c * RET_UNROLL), group_body, 0)


def _ret_tables(seq):
    c = RET_CHUNK
    half = HEAD_DIM // 2
    inv = ROPE_BASE ** (-jnp.arange(half, dtype=F32) / half)
    ang = jnp.arange(seq, dtype=F32)[:, None] * inv
    cos, sin = jnp.cos(ang), jnp.sin(ang)
    cos_t = jnp.tile(jnp.concatenate([cos, cos], axis=1), (1, RET_HEADS))
    sin_t = jnp.tile(jnp.concatenate([-sin, sin], axis=1), (1, RET_HEADS))
    log_g = jnp.log1p(-jnp.exp2(-5.0 - jnp.arange(RET_HEADS, dtype=F32)))
    n = jnp.arange(c, dtype=F32)
    diff = n[:, None] - n[None, :]
    dmat = jnp.where(diff >= 0, jnp.exp(log_g[:, None, None] * jnp.maximum(diff, 0.0)), 0.0)
    zeta = jnp.exp(log_g[:, None] * (c - 1 - n))
    xi = jnp.exp(log_g[:, None] * (n + 1))
    widen = lambda tab: jnp.repeat(tab.T, HEAD_DIM, axis=1)
    return cos_t, sin_t, dmat, widen(xi), widen(zeta)


def retention(p, batch, seq, gn_g, gn_b):
    t = batch * seq
    tb = min(RET_BLOCK, seq)
    nblk = seq // tb
    cos_t, sin_t, dmat, xi, zeta = _ret_tables(seq)
    hsum = _head_sum_matrix(RET_DIM)
    col0 = (RWKV_IN + ATTN_IN) // RET_DIM
    spec = lambda off: pl.BlockSpec((tb, RET_DIM), lambda b, j: (b * nblk + j, col0 + off))
    tab = pl.BlockSpec((tb, RET_DIM), lambda b, j: (j, 0))
    consts = [dmat, xi, zeta, gn_g.reshape(1, -1), gn_b.reshape(1, -1), hsum]
    return pl.pallas_call(
        _ret_kernel,
        grid=(batch, nblk),
        in_specs=[spec(0), spec(1), spec(2), spec(3), tab, tab] + [_const_spec(a.shape) for a in consts],
        out_specs=pl.BlockSpec((tb, RET_DIM), lambda b, j: (b * nblk + j, 0)),
        out_shape=jax.ShapeDtypeStruct((t, RET_DIM), BF16),
        scratch_shapes=[pltpu.VMEM((RET_HEADS, HEAD_DIM, HEAD_DIM), F32)],
        compiler_params=_params(("parallel", "arbitrary")),
        name="retention",
    )(p, p, p, p, cos_t, sin_t, *consts)


def _outproj_kernel(ya_ref, yb_ref, yc_ref, x_ref, w_ref, g_ref, b_ref, *rest, alpha):
    acc = _dot(ya_ref[...], w_ref[0:RWKV_DIM, :])
    acc += _dot(yb_ref[...], w_ref[RWKV_DIM:RWKV_DIM + ATTN_DIM, :])
    acc += _dot(yc_ref[...], w_ref[RWKV_DIM + ATTN_DIM:, :])
    h = _layer_norm(alpha * x_ref[...] + acc, g_ref[...], b_ref[...])
    if len(rest) == 1:
        (o_ref,) = rest
    else:
        router_ref, o_ref, logit_ref = rest
        logit_ref[...] = _dot_nt(router_ref[...].astype(BF16), h.astype(BF16))
    o_ref[...] = h


def out_projection_ln(ya, yb, yc, x2d, w_bf16, layer, g, b, alpha, router=None):
    t, d = x2d.shape
    tm = min(ROW_TILE, t)
    rows = lambda width: pl.BlockSpec((tm, width), lambda i: (i, 0))
    in_specs = [rows(RWKV_DIM), rows(ATTN_DIM), rows(RET_DIM), rows(d), _layer_spec(w_bf16, layer),
                _const_spec((1, d)), _const_spec((1, d))]
    args = [ya, yb, yc, x2d, w_bf16, g.reshape(1, -1), b.reshape(1, -1)]
    out_specs, out_shape = rows(d), jax.ShapeDtypeStruct((t, d), F32)
    if router is not None:
        in_specs.append(_const_spec((N_EXPERTS, d)))
        args.append(router.T)
        out_specs = [out_specs, pl.BlockSpec((N_EXPERTS, tm), lambda i: (0, i))]
        out_shape = [out_shape, jax.ShapeDtypeStruct((N_EXPERTS, t), F32)]
    return pl.pallas_call(
        functools.partial(_outproj_kernel, alpha=alpha),
        grid=(t // tm,),
        in_specs=in_specs,
        out_specs=out_specs,
        out_shape=out_shape,
        compiler_params=_params(("parallel",)),
        name="out_projection_ln",
    )(*args)


def _swiglu(xb, wg, wu, wd, acc, f_chunk):
    ff = wg.shape[-1]
    starts = list(range(0, ff, f_chunk))
    gate_up = lambda f0: (_dot(xb, wg[:, f0:f0 + f_chunk].astype(BF16)),
                          _dot(xb, wu[:, f0:f0 + f_chunk].astype(BF16)))
    nxt = gate_up(starts[0])
    for n, f0 in enumerate(starts):
        gate, up = nxt
        if n + 1 < len(starts):
            nxt = gate_up(starts[n + 1])
        hid = (gate * _sigmoid(gate) * up).astype(BF16)
        part = _dot(hid, wd[f0:f0 + f_chunk, :].astype(BF16))
        acc = part if acc is None else acc + part
    return acc


def _ffn_kernel(x_ref, wg_ref, wu_ref, wd_ref, g_ref, b_ref, o_ref, *, alpha, f_chunk):
    x = x_ref[...]
    acc = _swiglu(x.astype(BF16), wg_ref, wu_ref, wd_ref, alpha * x, f_chunk)
    o_ref[...] = _layer_norm(acc, g_ref[...], b_ref[...])


def dense_ffn_ln(x2d, wg, wu, wd, layer, g, b, alpha):
    t, d = x2d.shape
    tm = min(ROW_TILE, t)
    return pl.pallas_call(
        functools.partial(_ffn_kernel, alpha=alpha, f_chunk=FFN_SUBCHUNK),
        grid=(t // tm,),
        in_specs=[pl.BlockSpec((tm, d), lambda i: (i, 0)), _layer_spec(wg, layer), _layer_spec(wu, layer),
                  _layer_spec(wd, layer), _const_spec((1, d)), _const_spec((1, d))],
        out_specs=pl.BlockSpec((tm, d), lambda i: (i, 0)),
        out_shape=jax.ShapeDtypeStruct((t, d), F32),
        compiler_params=_params(("parallel",)),
        name="dense_ffn_ln",
    )(x2d, wg, wu, wd, g.reshape(1, -1), b.reshape(1, -1))


def _expert_kernel(blk_e_ref, used_ref, x_ref, wg_ref, wu_ref, wd_ref, *rest, blk_off):
    o_ref, acc_s = rest[-2:]
    i, j = pl.program_id(0) + blk_off, pl.program_id(1)
    last = pl.num_programs(1) - 1

    @pl.when(i < used_ref[0])
    def _():
        part = _swiglu(x_ref[...].astype(BF16), wg_ref.at[0], wu_ref.at[0], wd_ref.at[0], None, FFN_SUBCHUNK)

        @pl.when(j == 0)
        def _():
            acc_s[...] = part

        @pl.when((j > 0) & (j < last))
        def _():
            acc_s[...] += part

        @pl.when(j == last)
        def _():
            o_ref[...] = (acc_s[...] + part).astype(o_ref.dtype)


def expert_ffn(xs, blk_e, n_used, wg, wu, wd, layer, blk_off, out_rows, earlier=None, f_chunk=MOE_F_BLOCK):
    rows, d = xs.shape
    ff = wg.shape[3]
    tm = MOE_TILE
    in_specs = [pl.BlockSpec((tm, d), lambda i, j, be, nu: (i, 0)),
                pl.BlockSpec((None, 1, d, f_chunk), lambda i, j, be, nu: (layer, be[i + blk_off], 0, j)),
                pl.BlockSpec((None, 1, d, f_chunk), lambda i, j, be, nu: (layer, be[i + blk_off], 0, j)),
                pl.BlockSpec((None, 1, f_chunk, d), lambda i, j, be, nu: (layer, be[i + blk_off], j, 0))]
    args = [blk_e, n_used, xs, wg, wu, wd]
    aliases = {}
    if earlier is not None:
        in_specs.append(pl.BlockSpec(memory_space=pl.ANY))
        args.append(earlier)
        aliases = {len(args) - 1: 0}
    assert ff // f_chunk >= 2
    grid_spec = pltpu.PrefetchScalarGridSpec(
        num_scalar_prefetch=2,
        grid=(rows // tm, ff // f_chunk),
        in_specs=in_specs,
        out_specs=pl.BlockSpec((tm, d), lambda i, j, be, nu: (i + blk_off, 0)),
        scratch_shapes=[pltpu.VMEM((tm, d), F32)],
    )
    return pl.pallas_call(
        functools.partial(_expert_kernel, blk_off=blk_off),
        grid_spec=grid_spec,
        out_shape=jax.ShapeDtypeStruct((out_rows, d), BF16),
        input_output_aliases=aliases,
        compiler_params=_params(("parallel", "arbitrary")),
        name="expert_ffn",
    )(*args)


def _combine_ln_kernel(x_ref, y0_ref, y1_ref, gate_ref, g_ref, b_ref, o_ref, *, alpha):
    gates = gate_ref[...]
    f = y0_ref[...] * gates[:, 0:1] + y1_ref[...] * gates[:, 1:2]
    o_ref[...] = _layer_norm(alpha * x_ref[...] + f, g_ref[...], b_ref[...])


def combine_ln(x2d, y0, y1, gates, g, b, alpha):
    t, d = x2d.shape
    tm = min(ROW_TILE, t)
    rows = pl.BlockSpec((tm, d), lambda i: (i, 0))
    return pl.pallas_call(
        functools.partial(_combine_ln_kernel, alpha=alpha),
        grid=(t // tm,),
        in_specs=[rows, rows, rows, pl.BlockSpec((tm, TOP_K), lambda i: (i, 0)),
                  _const_spec((1, d)), _const_spec((1, d))],
        out_specs=rows,
        out_shape=jax.ShapeDtypeStruct((t, d), F32),
        compiler_params=_params(("parallel",)),
        name="combine_ln",
    )(x2d, y0, y1, gates, g.reshape(1, -1), b.reshape(1, -1))


def moe_ffn_ln(x2d, logits_t, wg, wu, wd, layer, g, b, alpha):
    t, d = x2d.shape
    tm = MOE_TILE
    logits = logits_t.T
    top_val, top_idx = lax.top_k(logits, TOP_K)
    gates = jax.nn.softmax(top_val, axis=-1)
    member = (top_idx[:, :, None] == jnp.arange(N_EXPERTS)[None, None, :]).any(axis=1)
    counts = member.sum(axis=0).astype(jnp.int32)
    rank = jnp.cumsum(member.astype(jnp.int32), axis=0) - member.astype(jnp.int32)
    padded = (counts + tm - 1) // tm * tm
    pad_ends = jnp.cumsum(padded)
    pad_starts = pad_ends - padded
    pos = jnp.take_along_axis(pad_starts[None, :] + rank, top_idx, axis=1)
    rows = t * TOP_K + N_EXPERTS * tm
    n_blk = rows // tm
    order = jnp.argsort(top_idx.reshape(-1), stable=True).astype(jnp.int32)
    tok_sorted = order // TOP_K
    starts = jnp.cumsum(counts) - counts
    r = jnp.arange(rows, dtype=jnp.int32)
    row_e = jnp.minimum(jnp.searchsorted(pad_ends, r, side='right'), N_EXPERTS - 1).astype(jnp.int32)
    within = r - pad_starts[row_e]
    src = jnp.where(within < counts[row_e], tok_sorted[jnp.minimum(starts[row_e] + within, t * TOP_K - 1)], 0)
    blk_e = row_e[::tm]
    n_used = (pad_ends[-1:] // tm).astype(jnp.int32)
    half = (n_blk // 2) * tm
    ys = expert_ffn(x2d[src[:half]], blk_e, n_used, wg, wu, wd, layer, 0, rows)
    ys = expert_ffn(x2d[src[half:]], blk_e, n_used, wg, wu, wd, layer, half // tm, rows, earlier=ys)
    return combine_ln(x2d, ys[pos[:, 0]], ys[pos[:, 1]], gates, g, b, alpha)


def kernel(x, w_in, w_out, rwkv_mu, rwkv_w0, rwkv_w_up, rwkv_a0, rwkv_a_up, rwkv_g_up, rwkv_k_k, rwkv_k_a,
           rwkv_r_k, rwkv_ln_g, rwkv_ln_b, ret_gn_g, ret_gn_b, rel_bias, ln_g, ln_b, ffn_w_gate, ffn_w_up,
           ffn_w_down, moe_router, moe_w_gate, moe_w_up, moe_w_down):
    batch, seq, d = x.shape
    depth = w_in.shape[0]
    alpha = (2 * depth) ** 0.25
    h = x.reshape(batch * seq, d)
    w_in, w_out = w_in.astype(BF16), w_out.astype(BF16)
    ffn_w_gate, ffn_w_up, ffn_w_down = ffn_w_gate.astype(BF16), ffn_w_up.astype(BF16), ffn_w_down.astype(BF16)
    for layer in range(depth):
        p = in_projection(h, w_in, layer)
        ya = rwkv_time_mix(p, batch, seq, rwkv_mu[layer], rwkv_w0[layer], rwkv_w_up[layer], rwkv_a0[layer],
                           rwkv_a_up[layer], rwkv_g_up[layer], rwkv_k_k[layer], rwkv_k_a[layer],
                           rwkv_r_k[layer], rwkv_ln_g[layer], rwkv_ln_b[layer])
        yb = dilated_attention(p, batch, seq, rel_bias)
        yc = retention(p, batch, seq, ret_gn_g[layer], ret_gn_b[layer])
        j = layer // 2
        if layer % 2 == 0:
            h = out_projection_ln(ya, yb, yc, h, w_out, layer, ln_g[layer, 0], ln_b[layer, 0], alpha)
            h = dense_ffn_ln(h, ffn_w_gate, ffn_w_up, ffn_w_down, j, ln_g[layer, 1], ln_b[layer, 1], alpha)
        else:
            h, logits_t = out_projection_ln(ya, yb, yc, h, w_out, layer, ln_g[layer, 0], ln_b[layer, 0], alpha,
                                            router=moe_router[j])
            h = moe_ffn_ln(h, logits_t, moe_w_gate, moe_w_up, moe_w_down, j,
                           ln_g[layer, 1], ln_b[layer, 1], alpha)
    return h.reshape(batch, seq, d)
```

```python
import functools
import math

import numpy as np
import jax
import jax.numpy as jnp
from jax import lax
from jax.experimental import pallas as pl
from jax.experimental.pallas import tpu as pltpu

F32 = jnp.float32
BF16 = jnp.bfloat16
HI = lax.Precision.HIGHEST

HEAD_DIM = 64
RWKV_HEADS = 4
ATTN_HEADS = 8
RET_HEADS = 4
RWKV_DIM = RWKV_HEADS * HEAD_DIM
ATTN_DIM = ATTN_HEADS * HEAD_DIM
RET_DIM = RET_HEADS * HEAD_DIM
DECAY_LORA = 64
ICL_LORA = 64
GATE_LORA = 128
RWKV_IN = 3 * RWKV_DIM + DECAY_LORA + ICL_LORA + GATE_LORA
ATTN_IN = 3 * ATTN_DIM
RET_IN = 4 * RET_DIM
RWKV_GN_EPS = 64e-5
DECAY_SCALE = math.exp(-0.5)
DILATED_PATTERNS = ((128, 1), (512, 4), (2048, 16))
NUM_BUCKETS = 32
MAX_DISTANCE = 2048
ROPE_BASE = 10000.0
N_EXPERTS = 8
TOP_K = 2
LN_EPS = 1e-5

LANES = 128
WKV_CHUNK = 64
WKV_BLOCK = 1024
WKV_MASK_ROWS = 256
WKV_GROUP = 4
ATTN_W = 128
ATTN_UNROLL = 4
ATTN_MERGE_ROWS = 256
INPROJ_N_CHUNK = 512
RET_CHUNK = 128
RET_BLOCK = 1024
RET_UNROLL = 4
ROW_TILE = 512
MOE_TILE = 512
MOE_F_BLOCK = 1792
FFN_SUBCHUNK = 256
MASK_VALUE = -1e30
VMEM_LIMIT = 56 * 1024 * 1024


_DIMS = {"nn": (((1,), (0,)), ((), ())), "nt": (((1,), (1,)), ((), ())), "tn": (((0,), (0,)), ((), ()))}


def _dot(a, b):
    return jnp.dot(a, b, preferred_element_type=F32)


def _dot_nt(a, b):
    return lax.dot_general(a, b, _DIMS["nt"], preferred_element_type=F32)


def _dot_tn(a, b):
    return lax.dot_general(a, b, _DIMS["tn"], preferred_element_type=F32)


def _split(x, terms):
    parts = []
    for _ in range(terms - 1):
        hi = x.astype(BF16)
        parts.append(hi)
        x = x - hi.astype(F32)
    parts.append(x.astype(BF16))
    return parts


def _mm(a, b, kind="nn", passes=3):
    dg = lambda p, q: lax.dot_general(p, q, _DIMS[kind], preferred_element_type=F32)
    if passes == 1:
        return dg(a.astype(BF16), b.astype(BF16))
    ah, al = _split(a, 2)
    bh, bl = _split(b, 2)
    return dg(ah, bh) + (dg(al, bh) + dg(ah, bl))


def _mm_ones(x, ones_bf16, ones_first=False, terms=3):
    parts = _split(x, terms)
    m, n = x.shape
    if ones_first:
        full = _dot(ones_bf16, jnp.concatenate(parts, axis=1))
        out = [full[:, i * n:(i + 1) * n] for i in range(terms)]
    else:
        full = _dot(jnp.concatenate(parts, axis=0), ones_bf16)
        out = [full[i * m:(i + 1) * m] for i in range(terms)]
    acc = out[-1]
    for o in reversed(out[:-1]):
        acc = acc + o
    return acc


def _sigmoid(x):
    return 1.0 / (1.0 + jnp.exp(-x))


def _layer_norm(z, g, b):
    mu = jnp.mean(z, axis=-1, keepdims=True)
    d = z - mu
    var = jnp.mean(d * d, axis=-1, keepdims=True)
    return d * lax.rsqrt(var + LN_EPS) * g + b


def _params(sem, vmem=VMEM_LIMIT):
    return pltpu.CompilerParams(dimension_semantics=sem, vmem_limit_bytes=vmem)


def _const_spec(shape):
    nd = len(shape)
    return pl.BlockSpec(shape, lambda *_: (0,) * nd)


def _layer_spec(stacked, layer):
    nd = stacked.ndim - 1
    return pl.BlockSpec((None,) + stacked.shape[1:], lambda *_: (layer,) + (0,) * nd)


CAST_BLOCK_BYTES = 8 * 1024 * 1024


def _cast_kernel(x_ref, o_ref):
    o_ref[...] = x_ref[...].astype(o_ref.dtype)


def layer_to_bf16(w, layer):
    shape = w.shape
    cols = shape[-1]
    rows = w[0].size // cols
    w3 = w.reshape(shape[0], rows, cols)
    tr = 1 << int(math.log2(max(16, min(rows, CAST_BLOCK_BYTES // (4 * cols)))))
    while rows % tr:
        tr //= 2
    assert tr % 16 == 0
    out = pl.pallas_call(
        _cast_kernel,
        grid=(rows // tr,),
        in_specs=[pl.BlockSpec((None, tr, cols), lambda i: (layer, i, 0))],
        out_specs=pl.BlockSpec((tr, cols), lambda i: (i, 0)),
        out_shape=jax.ShapeDtypeStruct((rows, cols), BF16),
        compiler_params=_params(("parallel",)),
        name="layer_to_bf16",
    )(w3)
    return out.reshape((1,) + shape[1:])


def _inproj_kernel(x_ref, w_ref, o_ref, *, n_chunk):
    xb = x_ref[...].astype(BF16)
    for n0 in range(0, o_ref.shape[1], n_chunk):
        o_ref[:, n0:n0 + n_chunk] = _dot(xb, w_ref[:, n0:n0 + n_chunk])


def in_projection(x2d, w_bf16, layer):
    t, d = x2d.shape
    n = w_bf16.shape[2]
    tm = min(ROW_TILE, t)
    return pl.pallas_call(
        functools.partial(_inproj_kernel, n_chunk=INPROJ_N_CHUNK),
        grid=(t // tm,),
        in_specs=[pl.BlockSpec((tm, d), lambda i: (i, 0)), _layer_spec(w_bf16, layer)],
        out_specs=pl.BlockSpec((tm, n), lambda i: (i, 0)),
        out_shape=jax.ShapeDtypeStruct((t, n), F32),
        compiler_params=_params(("parallel",)),
        name="in_projection",
    )(x2d, w_bf16)


def _rwkv_kernel(p_ref, mu_ref, w0_ref, wup_ref, a0_ref, aup_ref, gup_ref, kk_ref, ka_ref, rk_ref,
                 lng_ref, lnb_ref, ltri_ref, same_ref, hsum_ref, o_ref,
                 state_s, prev_s, kt_s, rt_s, bt_s, kn_s, v_s, btg_s, kng_s, etot_s, y_s, rp_s, y0_s, gt_s, zt_s):
    c = WKV_CHUNK
    tb = p_ref.shape[0]
    d = RWKV_DIM
    assert c == HEAD_DIM

    @pl.when(pl.program_id(1) == 0)
    def _():
        state_s[...] = jnp.zeros_like(state_s)
        prev_s[...] = jnp.zeros_like(prev_s)

    p = p_ref[...]
    row = lax.broadcasted_iota(jnp.int32, p.shape, 0)
    shifted = jnp.where(row == 0, prev_s[...], pltpu.roll(p, 1, axis=0))
    prev_s[...] = p[tb - 1:tb, :]
    ps = p + (shifted - p) * mu_ref[...]
    r = ps[:, 0:d]
    k = ps[:, d:2 * d]
    v = ps[:, 2 * d:3 * d]
    xw = ps[:, 3 * d:3 * d + DECAY_LORA]
    xa = ps[:, 3 * d + DECAY_LORA:3 * d + DECAY_LORA + ICL_LORA]
    xg = ps[:, 3 * d + DECAY_LORA + ICL_LORA:]

    hsum = hsum_ref[...]
    logw = -DECAY_SCALE * _sigmoid(w0_ref[...] + _mm(jnp.tanh(xw), wup_ref[...]))
    a = _sigmoid(a0_ref[...] + _mm(xa, aup_ref[...]))
    g = _dot(_sigmoid(xg).astype(BF16), gup_ref[...].astype(BF16))
    kap = k * kk_ref[...]
    kap = kap * lax.rsqrt(jnp.maximum(_mm_ones(kap * kap, hsum), 1e-24))
    kn = k * (1.0 + (a - 1.0) * ka_ref[...])
    mb = ltri_ref.shape[0]
    ones2 = jnp.concatenate([ltri_ref[...], same_ref[...]], axis=0)
    sums = [_mm_ones(logw[s0:s0 + mb], ones2, ones_first=True) for s0 in range(0, tb, mb)]
    cum = jnp.concatenate([s[:mb] for s in sums], axis=0)
    tot = jnp.concatenate([s[mb:] for s in sums], axis=0)
    e_neg = jnp.exp(-cum)
    e_rem = jnp.exp(tot - cum)
    nb = -(a * kap)
    kt_s[...] = kap * jnp.exp(cum - logw)
    rt_s[...] = r * jnp.exp(cum)
    bt_s[...] = nb * e_neg
    kn_s[...] = kn * e_neg
    btg_s[...] = nb * e_rem
    kng_s[...] = kn * e_rem
    etot_s[...] = jnp.exp(tot)
    v_s[...] = v

    nh = RWKV_HEADS
    ri = lax.broadcasted_iota(jnp.int32, (c, d), 0)
    ci = lax.broadcasted_iota(jnp.int32, (c, d), 1) % HEAD_DIM
    strict = ci < ri
    incl = ci <= ri
    diag = ci == ri
    eye = diag.astype(F32)
    bi = lax.broadcasted_iota(jnp.int32, (d, d), 0) // HEAD_DIM
    bj = lax.broadcasted_iota(jnp.int32, (d, d), 1) // HEAD_DIM
    blocks = bi == bj
    zero16 = jnp.zeros((), BF16)

    def expand(x16):
        return jnp.where(blocks, jnp.concatenate([x16] * nh, axis=0), zero16)

    def bdmm(a, y, kind="nn", passes=3):
        dg = lambda p_, q_: lax.dot_general(p_, q_, _DIMS[kind], preferred_element_type=F32)
        if passes == 1:
            return dg(a.astype(BF16), expand(y.astype(BF16)))
        ah, al = _split(a, 2)
        yh, yl = _split(y, 2)
        m = a.shape[0]
        both = dg(jnp.concatenate([ah, al], axis=0), expand(yh))
        return both[:m] + (both[m:] + dg(ah, expand(yl)))

    def block_diagonal_of(full):
        outs = []
        for n0 in range(0, full.shape[1], d):
            m = jnp.where(blocks, full[:, n0:n0 + d], 0.0)
            acc = m[0:c]
            for h in range(1, nh):
                acc = acc + m[h * c:(h + 1) * c]
            outs.append(acc)
        return outs

    nchunk = tb // c
    cat0 = lambda x, y: jnp.concatenate([x, y], axis=0)
    cat1 = lambda x, y: jnp.concatenate([x, y], axis=1)
    levels = int(math.log2(c)) - 1

    def independent_part(chunks):
        get = lambda ref: [ref[j * c:(j + 1) * c, :] for j in chunks]
        kt, rt, vv, btg = get(kt_s), get(rt_s), get(v_s), get(btg_s)
        lhs = [cat0(k_, r_) for k_, r_ in zip(kt, rt)]
        a_b = [bdmm(l_, b_, "nt") for l_, b_ in zip(lhs, get(bt_s))]
        yield
        a_k = [bdmm(l_, n_, "nt") for l_, n_ in zip(lhs, get(kn_s))]
        yield
        a_ab = [jnp.where(strict, m[:c], 0.0) for m in a_b]
        a_rb = [jnp.where(incl, m[c:], 0.0) for m in a_b]
        a_kr = [cat0(jnp.where(strict, m[:c], 0.0), jnp.where(incl, m[c:], 0.0)) for m in a_k]
        inv = [eye + m for m in a_ab]
        pw = [bdmm(m, m, passes=1) for m in a_ab]
        yield
        for lvl in range(levels):
            if lvl < levels - 1:
                both = [bdmm(cat0(x_, p_), p_, passes=1) for x_, p_ in zip(inv, pw)]
                inv = [x_ + b_[:c] for x_, b_ in zip(inv, both)]
                pw = [b_[c:] for b_ in both]
            else:
                inv = [x_ + bdmm(x_, p_, passes=1) for x_, p_ in zip(inv, pw)]
            yield
        av = [bdmm(m, v_) for m, v_ in zip(a_kr, vv)]
        yield
        wmat = [bdmm(x_, k_, passes=1) for x_, k_ in zip(inv, kt)]
        umat = [bdmm(x_, a_[:c], passes=1) for x_, a_ in zip(inv, av)]
        yield
        rw = [bdmm(m, w_, passes=1) for m, w_ in zip(a_rb, wmat)]
        ru = [bdmm(m, u_, passes=1) for m, u_ in zip(a_rb, umat)]
        yield
        gz = [block_diagonal_of(_mm(b_, cat1(w_, u_), "tn", passes=1))
              for b_, w_, u_ in zip(btg, wmat, umat)]
        kv = [block_diagonal_of(_mm(n_, v_, "tn"))[0] for n_, v_ in zip(get(kng_s), vv)]
        for i, j in enumerate(chunks):
            rows = slice(j * c, (j + 1) * c)
            rp_s[rows, :] = rt[i] + rw[i]
            y0_s[rows, :] = ru[i] + av[i][c:]
            g_diag = jnp.where(diag, jnp.broadcast_to(etot_s[j * c:j * c + 1, :], (c, d)), 0.0)
            gt_s[rows, :] = g_diag + gz[i][0]
            zt_s[rows, :] = gz[i][1] + kv[i]

    state = [state_s[...]]

    def sequential_part(chunks):
        for j in chunks:
            rows = slice(j * c, (j + 1) * c)
            ry = bdmm(cat0(rp_s[rows, :], gt_s[rows, :]), state[0])
            y_s[rows, :] = ry[:c] + y0_s[rows, :]
            state[0] = ry[c:] + zt_s[rows, :]
            yield

    def output_part(chunks):
        rows = slice(chunks[0] * c, (chunks[-1] + 1) * c)
        y = y_s[rows, :]
        mean = _mm_ones(y, hsum, terms=2) * (1.0 / HEAD_DIM)
        yield
        dy = y - mean
        var = _mm_ones(dy * dy, hsum, terms=2) * (1.0 / HEAD_DIM)
        yield
        yn = dy * lax.rsqrt(var + RWKV_GN_EPS) * lng_ref[...] + lnb_ref[...]
        bonus = _mm_ones(r[rows] * kn[rows] * rk_ref[...], hsum, terms=2) * v[rows]
        yield
        o_ref[rows, :] = ((yn + bonus) * g[rows]).astype(o_ref.dtype)

    groups = [range(g0, min(g0 + WKV_GROUP, nchunk)) for g0 in range(0, nchunk, WKV_GROUP)]
    pending = iter(())
    for chunks in groups:
        for _ in independent_part(chunks):
            next(pending, None)
        for _ in pending:
            pass
        pending = sequential_part(chunks)
    for chunks in groups[:-1]:
        for _ in output_part(chunks):
            next(pending, None)
    for _ in pending:
        pass
    state_s[...] = state[0]
    for _ in output_part(groups[-1]):
        pass


def _chunk_masks(tb, c):
    i = np.arange(tb)
    same = (i[:, None] // c) == (i[None, :] // c)
    ltri = same & (i[None, :] <= i[:, None])
    return jnp.asarray(ltri, BF16), jnp.asarray(same, BF16)


def _head_sum_matrix(width):
    i = np.arange(width)
    return jnp.asarray((i[:, None] // HEAD_DIM) == (i[None, :] // HEAD_DIM), BF16)


def rwkv_time_mix(p, batch, seq, mu, w0, w_up, a0, a_up, g_up, k_k, k_a, r_k, ln_g, ln_b):
    t = batch * seq
    tb = min(WKV_BLOCK, seq)
    nblk = seq // tb
    ltri, same = _chunk_masks(min(WKV_MASK_ROWS, tb), WKV_CHUNK)
    hsum = _head_sum_matrix(RWKV_DIM)
    row = lambda a: a.reshape(1, -1)
    consts = [row(mu), row(w0), w_up, row(a0), a_up, g_up, row(k_k), row(k_a), row(r_k), row(ln_g), row(ln_b),
              ltri, same, hsum]
    buf = lambda: pltpu.VMEM((tb, RWKV_DIM), F32)
    return pl.pallas_call(
        _rwkv_kernel,
        grid=(batch, nblk),
        in_specs=[pl.BlockSpec((tb, RWKV_IN), lambda b, j: (b * nblk + j, 0))]
                 + [_const_spec(a.shape) for a in consts],
        out_specs=pl.BlockSpec((tb, RWKV_DIM), lambda b, j: (b * nblk + j, 0)),
        out_shape=jax.ShapeDtypeStruct((t, RWKV_DIM), BF16),
        scratch_shapes=[pltpu.VMEM((HEAD_DIM, RWKV_DIM), F32), pltpu.VMEM((1, RWKV_IN), F32)]
                       + [buf() for _ in range(13)],
        compiler_params=_params(("parallel", "arbitrary")),
        name="rwkv_time_mix",
    )(p, *consts)


def _attn_kernel(q_ref, k_ref, v_ref, bias_ref, o_ref, acc_s, m_s, l_s):
    seq = q_ref.shape[0]
    w = ATTN_W
    scale = HEAD_DIM ** -0.5

    def rows_of(start, dil):
        return pl.ds(start, w) if dil == 1 else pl.ds(start, w, stride=dil)

    lane = lax.broadcasted_iota(jnp.int32, (w, LANES), 1)
    head0 = lane < HEAD_DIM
    zero = jnp.zeros((), BF16)
    one = jnp.ones((), BF16)

    def group(pi, dil, g, firsts):
        rows_l, q_l, k_l, v_l = [], [], [], []
        for u, first in enumerate(firsts):
            b = g * len(firsts) + u
            start = (b % dil) + (b // dil) * (dil * w)
            rows = rows_of(start, dil)
            q = (q_ref[rows, :] * scale).astype(BF16)
            kk = k_ref[rows, :].astype(BF16)
            vv = v_ref[rows, :].astype(BF16)
            if not first:
                prev = rows_of(start - dil * w, dil)
                kk = jnp.concatenate([k_ref[prev, :].astype(BF16), kk], axis=0)
                vv = jnp.concatenate([v_ref[prev, :].astype(BF16), vv], axis=0)
            rows_l.append(rows)
            q_l.append(q)
            k_l.append(kk)
            v_l.append(vv)
        s = [[_dot_nt(jnp.where(head0 if h == 0 else ~head0, q, zero), kk)
              + (bias_ref[pi, h, :, w:] if first else bias_ref[pi, h])
              for h in range(2)] for q, kk, first in zip(q_l, k_l, firsts)]
        m = [[jnp.max(sh, axis=-1, keepdims=True) for sh in su] for su in s]
        pr = [[jnp.exp(sh - mh).astype(BF16) for sh, mh in zip(su, mu)] for su, mu in zip(s, m)]
        kmask = lambda vv: lax.broadcasted_iota(jnp.int32, vv.shape, 1) < HEAD_DIM
        res = [[_dot(pu[0], jnp.where(kmask(vv), vv, one)), _dot(pu[1], jnp.where(kmask(vv), one, vv))]
               for pu, vv in zip(pr, v_l)]
        for rows, ru, mu in zip(rows_l, res, m):
            acc_s[pi, rows, :] = jnp.where(head0, ru[0], ru[1])
            l_s[pi, rows, :] = jnp.where(head0, ru[1], ru[0])
            m_s[pi, rows, :] = jnp.where(head0, mu[0], mu[1])

    n_groups = (seq // w) // ATTN_UNROLL
    for pi, (window, dil) in enumerate(DILATED_PATTERNS):
        flags = [tuple((g * ATTN_UNROLL + u) < dil for u in range(ATTN_UNROLL)) for g in range(n_groups)]
        g0 = 0
        while g0 < n_groups:
            g1 = g0
            while g1 < n_groups and flags[g1] == flags[g0]:
                g1 += 1
            if g1 - g0 == 1:
                group(pi, dil, g0, flags[g0])
            else:
                def body(g, carry, pi=pi, dil=dil, firsts=flags[g0]):
                    group(pi, dil, g, firsts)
                    return carry
                lax.fori_loop(g0, g1, body, 0)
            g0 = g1

    mt = ATTN_MERGE_ROWS

    def merge_body(i, carry):
        rows = pl.ds(pl.multiple_of(i * mt, mt), mt)
        m0, m1, m2 = m_s[0, rows, :], m_s[1, rows, :], m_s[2, rows, :]
        mx = jnp.maximum(jnp.maximum(m0, m1), m2)
        w0, w1, w2 = jnp.exp(m0 - mx), jnp.exp(m1 - mx), jnp.exp(m2 - mx)
        num = w0 * acc_s[0, rows, :] + w1 * acc_s[1, rows, :] + w2 * acc_s[2, rows, :]
        swap = lambda x: pltpu.roll(x, HEAD_DIM, axis=1)
        den = w0 * swap(l_s[0, rows, :]) + w1 * swap(l_s[1, rows, :]) + w2 * swap(l_s[2, rows, :])
        o_ref[rows, :] = (num / den).astype(o_ref.dtype)
        return carry

    lax.fori_loop(0, seq // mt, merge_body, 0)


def _t5_bucket(dist):
    max_exact = NUM_BUCKETS // 2
    large = max_exact + (np.log(np.maximum(dist, max_exact) / max_exact)
                         / math.log(MAX_DISTANCE / max_exact) * (NUM_BUCKETS - max_exact)).astype(np.int32)
    return np.where(dist < max_exact, dist, np.minimum(large, NUM_BUCKETS - 1)).astype(np.int32)


def _attn_bias(rel_bias):
    w = ATTN_W
    i = np.arange(w)[:, None]
    j = np.arange(2 * w)[None, :]
    rel = i + w - j
    band = (rel >= 0) & (rel <= w)
    tabs = []
    for window, dil in DILATED_PATTERNS:
        bucket = _t5_bucket(np.clip(rel, 0, None) * dil)
        onehot = jnp.asarray(bucket[..., None] == np.arange(NUM_BUCKETS), F32)
        bias = jnp.einsum('ijb,bh->hij', onehot, rel_bias.astype(F32), precision=HI)
        tabs.append(jnp.where(band[None], bias, MASK_VALUE))
    return jnp.stack(tabs)


def dilated_attention(p, batch, seq, rel_bias):
    t = batch * seq
    bias = _attn_bias(rel_bias)
    col0 = RWKV_IN // LANES
    npair = ATTN_DIM // LANES
    spec = lambda off: pl.BlockSpec((seq, LANES), lambda b, hp: (b, col0 + off + hp))
    return pl.pallas_call(
        _attn_kernel,
        grid=(batch, npair),
        in_specs=[spec(0), spec(npair), spec(2 * npair),
                  pl.BlockSpec((3, 2, ATTN_W, 2 * ATTN_W), lambda b, hp: (0, hp, 0, 0))],
        out_specs=pl.BlockSpec((seq, LANES), lambda b, hp: (b, hp)),
        out_shape=jax.ShapeDtypeStruct((t, ATTN_DIM), BF16),
        scratch_shapes=[pltpu.VMEM((3, seq, LANES), F32) for _ in range(3)],
        compiler_params=_params(("parallel", "parallel")),
        name="dilated_attention",
    )(p, p, p, bias)


def _ret_kernel(q_ref, k_ref, v_ref, g_ref, cos_ref, sin_ref, dmat_ref, xi_ref, zeta_ref, gng_ref, gnb_ref,
                hsum_ref, o_ref, state_s):
    c = RET_CHUNK
    tb = q_ref.shape[0]

    @pl.when(pl.program_id(1) == 0)
    def _():
        state_s[...] = jnp.zeros_like(state_s)

    lane = lax.broadcasted_iota(jnp.int32, (c, RET_DIM), 1)
    first_half = (lane % HEAD_DIM) < (HEAD_DIM // 2)

    def rotate(x, cos, sin):
        swapped = jnp.where(first_half, pltpu.roll(x, RET_DIM - HEAD_DIM // 2, axis=1),
                            pltpu.roll(x, HEAD_DIM // 2, axis=1))
        return x * cos + swapped * sin

    heads = range(RET_HEADS)
    hsl = [slice(h * HEAD_DIM, (h + 1) * HEAD_DIM) for h in heads]
    chunk_decay = [(1.0 - 2.0 ** (-5.0 - h)) ** c for h in heads]

    def group_body(g, carry):
        rows_l, qb, kb, qx, kz, vb = [], [], [], [], [], []
        for u in range(RET_UNROLL):
            rows = pl.ds(pl.multiple_of((g * RET_UNROLL + u) * c, c), c)
            cos, sin = cos_ref[rows, :], sin_ref[rows, :]
            q = rotate(q_ref[rows, :], cos, sin)
            k = rotate(k_ref[rows, :], cos, sin) * (HEAD_DIM ** -0.5)
            rows_l.append(rows)
            qb.append(q.astype(BF16))
            kb.append(k.astype(BF16))
            qx.append((q * xi_ref[...]).astype(BF16))
            kz.append((k * zeta_ref[...]).astype(BF16))
            vb.append(v_ref[rows, :].astype(BF16))
        sc = [[(_dot_nt(qb[u][:, sl], kb[u][:, sl]) * dmat_ref[h]).astype(BF16) for h, sl in zip(heads, hsl)]
              for u in range(RET_UNROLL)]
        intra = [[_dot(sc[u][h], vb[u][:, hsl[h]]) for h in heads] for u in range(RET_UNROLL)]
        kv = [[_dot_tn(kz[u][:, sl], vb[u][:, sl]) for sl in hsl] for u in range(RET_UNROLL)]
        states = [state_s[h] for h in heads]
        ys = []
        for u in range(RET_UNROLL):
            ys.append(jnp.concatenate(
                [intra[u][h] + _dot(qx[u][:, hsl[h]], states[h].astype(BF16)) for h in heads], axis=1))
            states = [states[h] * chunk_decay[h] + kv[u][h] for h in heads]
        for h in heads:
            state_s[h] = states[h]
        hsum = hsum_ref[...]
        for rows, y in zip(rows_l, ys):
            mean = _mm_ones(y, hsum, terms=2) * (1.0 / HEAD_DIM)
            dy = y - mean
            var = _mm_ones(dy * dy, hsum, terms=2) * (1.0 / HEAD_DIM)
            yn = dy * lax.rsqrt(var + LN_EPS) * gng_ref[...] + gnb_ref[...]
            gate = g_ref[rows, :]
            o_ref[rows, :] = (gate * _sigmoid(gate) * yn).astype(o_ref.dtype)
        return carry

    lax.fori_loop(0, tb // (c * RET_UNROLL), group_body, 0)


def _ret_tables(seq):
    c = RET_CHUNK
    half = HEAD_DIM // 2
    inv = ROPE_BASE ** (-jnp.arange(half, dtype=F32) / half)
    ang = jnp.arange(seq, dtype=F32)[:, None] * inv
    cos, sin = jnp.cos(ang), jnp.sin(ang)
    cos_t = jnp.tile(jnp.concatenate([cos, cos], axis=1), (1, RET_HEADS))
    sin_t = jnp.tile(jnp.concatenate([-sin, sin], axis=1), (1, RET_HEADS))
    log_g = jnp.log1p(-jnp.exp2(-5.0 - jnp.arange(RET_HEADS, dtype=F32)))
    n = jnp.arange(c, dtype=F32)
    diff = n[:, None] - n[None, :]
    dmat = jnp.where(diff >= 0, jnp.exp(log_g[:, None, None] * jnp.maximum(diff, 0.0)), 0.0)
    zeta = jnp.exp(log_g[:, None] * (c - 1 - n))
    xi = jnp.exp(log_g[:, None] * (n + 1))
    widen = lambda tab: jnp.repeat(tab.T, HEAD_DIM, axis=1)
    return cos_t, sin_t, dmat, widen(xi), widen(zeta)


def retention(p, batch, seq, gn_g, gn_b):
    t = batch * seq
    tb = min(RET_BLOCK, seq)
    nblk = seq // tb
    cos_t, sin_t, dmat, xi, zeta = _ret_tables(seq)
    hsum = _head_sum_matrix(RET_DIM)
    col0 = (RWKV_IN + ATTN_IN) // RET_DIM
    spec = lambda off: pl.BlockSpec((tb, RET_DIM), lambda b, j: (b * nblk + j, col0 + off))
    tab = pl.BlockSpec((tb, RET_DIM), lambda b, j: (j, 0))
    consts = [dmat, xi, zeta, gn_g.reshape(1, -1), gn_b.reshape(1, -1), hsum]
    return pl.pallas_call(
        _ret_kernel,
        grid=(batch, nblk),
        in_specs=[spec(0), spec(1), spec(2), spec(3), tab, tab] + [_const_spec(a.shape) for a in consts],
        out_specs=pl.BlockSpec((tb, RET_DIM), lambda b, j: (b * nblk + j, 0)),
        out_shape=jax.ShapeDtypeStruct((t, RET_DIM), BF16),
        scratch_shapes=[pltpu.VMEM((RET_HEADS, HEAD_DIM, HEAD_DIM), F32)],
        compiler_params=_params(("parallel", "arbitrary")),
        name="retention",
    )(p, p, p, p, cos_t, sin_t, *consts)


def _outproj_kernel(ya_ref, yb_ref, yc_ref, x_ref, w_ref, g_ref, b_ref, *rest, alpha):
    acc = _dot(ya_ref[...], w_ref[0:RWKV_DIM, :])
    acc += _dot(yb_ref[...], w_ref[RWKV_DIM:RWKV_DIM + ATTN_DIM, :])
    acc += _dot(yc_ref[...], w_ref[RWKV_DIM + ATTN_DIM:, :])
    h = _layer_norm(alpha * x_ref[...] + acc, g_ref[...], b_ref[...])
    if len(rest) == 1:
        (o_ref,) = rest
    else:
        router_ref, o_ref, logit_ref = rest
        logit_ref[...] = _dot_nt(router_ref[...].astype(BF16), h.astype(BF16))
    o_ref[...] = h


def out_projection_ln(ya, yb, yc, x2d, w_bf16, layer, g, b, alpha, router=None):
    t, d = x2d.shape
    tm = min(ROW_TILE, t)
    rows = lambda width: pl.BlockSpec((tm, width), lambda i: (i, 0))
    in_specs = [rows(RWKV_DIM), rows(ATTN_DIM), rows(RET_DIM), rows(d), _layer_spec(w_bf16, layer),
                _const_spec((1, d)), _const_spec((1, d))]
    args = [ya, yb, yc, x2d, w_bf16, g.reshape(1, -1), b.reshape(1, -1)]
    out_specs, out_shape = rows(d), jax.ShapeDtypeStruct((t, d), F32)
    if router is not None:
        in_specs.append(_const_spec((N_EXPERTS, d)))
        args.append(router.T)
        out_specs = [out_specs, pl.BlockSpec((N_EXPERTS, tm), lambda i: (0, i))]
        out_shape = [out_shape, jax.ShapeDtypeStruct((N_EXPERTS, t), F32)]
    return pl.pallas_call(
        functools.partial(_outproj_kernel, alpha=alpha),
        grid=(t // tm,),
        in_specs=in_specs,
        out_specs=out_specs,
        out_shape=out_shape,
        compiler_params=_params(("parallel",)),
        name="out_projection_ln",
    )(*args)


def _swiglu(xb, wg, wu, wd, acc, f_chunk):
    ff = wg.shape[-1]
    starts = list(range(0, ff, f_chunk))
    gate_up = lambda f0: (_dot(xb, wg[:, f0:f0 + f_chunk]), _dot(xb, wu[:, f0:f0 + f_chunk]))
    nxt = gate_up(starts[0])
    for n, f0 in enumerate(starts):
        gate, up = nxt
        if n + 1 < len(starts):
            nxt = gate_up(starts[n + 1])
        hid = (gate * _sigmoid(gate) * up).astype(BF16)
        part = _dot(hid, wd[f0:f0 + f_chunk, :])
        acc = part if acc is None else acc + part
    return acc


def _ffn_kernel(x_ref, wg_ref, wu_ref, wd_ref, g_ref, b_ref, o_ref, *, alpha, f_chunk):
    x = x_ref[...]
    acc = _swiglu(x.astype(BF16), wg_ref, wu_ref, wd_ref, alpha * x, f_chunk)
    o_ref[...] = _layer_norm(acc, g_ref[...], b_ref[...])


def dense_ffn_ln(x2d, wg, wu, wd, layer, g, b, alpha):
    t, d = x2d.shape
    tm = min(ROW_TILE, t)
    return pl.pallas_call(
        functools.partial(_ffn_kernel, alpha=alpha, f_chunk=FFN_SUBCHUNK),
        grid=(t // tm,),
        in_specs=[pl.BlockSpec((tm, d), lambda i: (i, 0)), _layer_spec(wg, layer), _layer_spec(wu, layer),
                  _layer_spec(wd, layer), _const_spec((1, d)), _const_spec((1, d))],
        out_specs=pl.BlockSpec((tm, d), lambda i: (i, 0)),
        out_shape=jax.ShapeDtypeStruct((t, d), F32),
        compiler_params=_params(("parallel",)),
        name="dense_ffn_ln",
    )(x2d, wg, wu, wd, g.reshape(1, -1), b.reshape(1, -1))


def _expert_kernel(blk_e_ref, used_ref, x_ref, wg_ref, wu_ref, wd_ref, o_ref, acc_s):
    i, j = pl.program_id(0), pl.program_id(1)
    last = pl.num_programs(1) - 1

    @pl.when(i < used_ref[0])
    def _():
        part = _swiglu(x_ref[...].astype(BF16), wg_ref.at[0], wu_ref.at[0], wd_ref.at[0], None, FFN_SUBCHUNK)

        @pl.when(j == 0)
        def _():
            acc_s[...] = part

        @pl.when((j > 0) & (j < last))
        def _():
            acc_s[...] += part

        @pl.when(j == last)
        def _():
            o_ref[...] = (acc_s[...] + part).astype(o_ref.dtype)


def expert_ffn(xs, blk_e, n_used, wg, wu, wd, layer, f_chunk=MOE_F_BLOCK):
    rows, d = xs.shape
    ff = wg.shape[3]
    tm = MOE_TILE
    assert ff // f_chunk >= 2
    grid_spec = pltpu.PrefetchScalarGridSpec(
        num_scalar_prefetch=2,
        grid=(rows // tm, ff // f_chunk),
        in_specs=[pl.BlockSpec((tm, d), lambda i, j, be, nu: (i, 0)),
                  pl.BlockSpec((None, 1, d, f_chunk), lambda i, j, be, nu: (layer, be[i], 0, j)),
                  pl.BlockSpec((None, 1, d, f_chunk), lambda i, j, be, nu: (layer, be[i], 0, j)),
                  pl.BlockSpec((None, 1, f_chunk, d), lambda i, j, be, nu: (layer, be[i], j, 0))],
        out_specs=pl.BlockSpec((tm, d), lambda i, j, be, nu: (i, 0)),
        scratch_shapes=[pltpu.VMEM((tm, d), F32)],
    )
    return pl.pallas_call(
        _expert_kernel,
        grid_spec=grid_spec,
        out_shape=jax.ShapeDtypeStruct((rows, d), BF16),
        compiler_params=_params(("parallel", "arbitrary")),
        name="expert_ffn",
    )(blk_e, n_used, xs, wg, wu, wd)


def _combine_ln_kernel(x_ref, y0_ref, y1_ref, gate_ref, g_ref, b_ref, o_ref, *, alpha):
    gates = gate_ref[...]
    f = y0_ref[...] * gates[:, 0:1] + y1_ref[...] * gates[:, 1:2]
    o_ref[...] = _layer_norm(alpha * x_ref[...] + f, g_ref[...], b_ref[...])


def combine_ln(x2d, y0, y1, gates, g, b, alpha):
    t, d = x2d.shape
    tm = min(ROW_TILE, t)
    rows = pl.BlockSpec((tm, d), lambda i: (i, 0))
    return pl.pallas_call(
        functools.partial(_combine_ln_kernel, alpha=alpha),
        grid=(t // tm,),
        in_specs=[rows, rows, rows, pl.BlockSpec((tm, TOP_K), lambda i: (i, 0)),
                  _const_spec((1, d)), _const_spec((1, d))],
        out_specs=rows,
        out_shape=jax.ShapeDtypeStruct((t, d), F32),
        compiler_params=_params(("parallel",)),
        name="combine_ln",
    )(x2d, y0, y1, gates, g.reshape(1, -1), b.reshape(1, -1))


def moe_ffn_ln(x2d, logits_t, wg, wu, wd, layer, g, b, alpha):
    t, d = x2d.shape
    tm = MOE_TILE
    logits = logits_t.T
    top_val, top_idx = lax.top_k(logits, TOP_K)
    gates = jax.nn.softmax(top_val, axis=-1)
    member = (top_idx[:, :, None] == jnp.arange(N_EXPERTS)[None, None, :]).any(axis=1)
    counts = member.sum(axis=0).astype(jnp.int32)
    rank = jnp.cumsum(member.astype(jnp.int32), axis=0) - member.astype(jnp.int32)
    padded = (counts + tm - 1) // tm * tm
    pad_ends = jnp.cumsum(padded)
    pad_starts = pad_ends - padded
    pos = jnp.take_along_axis(pad_starts[None, :] + rank, top_idx, axis=1)
    rows = t * TOP_K + N_EXPERTS * tm
    order = jnp.argsort(top_idx.reshape(-1), stable=True).astype(jnp.int32)
    tok_sorted = order // TOP_K
    starts = jnp.cumsum(counts) - counts
    r = jnp.arange(rows, dtype=jnp.int32)
    row_e = jnp.minimum(jnp.searchsorted(pad_ends, r, side='right'), N_EXPERTS - 1).astype(jnp.int32)
    within = r - pad_starts[row_e]
    src = jnp.where(within < counts[row_e], tok_sorted[jnp.minimum(starts[row_e] + within, t * TOP_K - 1)], 0)
    blk_e = row_e[::tm]
    n_used = (pad_ends[-1:] // tm).astype(jnp.int32)
    ys = expert_ffn(x2d[src], blk_e, n_used, wg, wu, wd, layer)
    return combine_ln(x2d, ys[pos[:, 0]], ys[pos[:, 1]], gates, g, b, alpha)


def kernel(x, w_in, w_out, rwkv_mu, rwkv_w0, rwkv_w_up, rwkv_a0, rwkv_a_up, rwkv_g_up, rwkv_k_k, rwkv_k_a,
           rwkv_r_k, rwkv_ln_g, rwkv_ln_b, ret_gn_g, ret_gn_b, rel_bias, ln_g, ln_b, ffn_w_gate, ffn_w_up,
           ffn_w_down, moe_router, moe_w_gate, moe_w_up, moe_w_down):
    batch, seq, d = x.shape
    depth = w_in.shape[0]
    alpha = (2 * depth) ** 0.25
    h = x.reshape(batch * seq, d)
    w_in, w_out = w_in.astype(BF16), w_out.astype(BF16)
    ffn_w_gate, ffn_w_up, ffn_w_down = ffn_w_gate.astype(BF16), ffn_w_up.astype(BF16), ffn_w_down.astype(BF16)
    for layer in range(depth):
        p = in_projection(h, w_in, layer)
        ya = rwkv_time_mix(p, batch, seq, rwkv_mu[layer], rwkv_w0[layer], rwkv_w_up[layer], rwkv_a0[layer],
                           rwkv_a_up[layer], rwkv_g_up[layer], rwkv_k_k[layer], rwkv_k_a[layer],
                           rwkv_r_k[layer], rwkv_ln_g[layer], rwkv_ln_b[layer])
        yb = dilated_attention(p, batch, seq, rel_bias)
        yc = retention(p, batch, seq, ret_gn_g[layer], ret_gn_b[layer])
        j = layer // 2
        if layer % 2 == 0:
            h = out_projection_ln(ya, yb, yc, h, w_out, layer, ln_g[layer, 0], ln_b[layer, 0], alpha)
            h = dense_ffn_ln(h, ffn_w_gate, ffn_w_up, ffn_w_down, j, ln_g[layer, 1], ln_b[layer, 1], alpha)
        else:
            h, logits_t = out_projection_ln(ya, yb, yc, h, w_out, layer, ln_g[layer, 0], ln_b[layer, 0], alpha,
                                            router=moe_router[j])
            h = moe_ffn_ln(h, logits_t, layer_to_bf16(moe_w_gate, j), layer_to_bf16(moe_w_up, j),
                           layer_to_bf16(moe_w_down, j), 0, ln_g[layer, 1], ln_b[layer, 1], alpha)
    return h.reshape(batch, seq, d)
```

```python
import functools
import math

import numpy as np
import jax
import jax.numpy as jnp
from jax import lax
from jax.experimental import pallas as pl
from jax.experimental.pallas import tpu as pltpu

F32 = jnp.float32
BF16 = jnp.bfloat16
HI = lax.Precision.HIGHEST

HEAD_DIM = 64
RWKV_HEADS = 4
ATTN_HEADS = 8
RET_HEADS = 4
RWKV_DIM = RWKV_HEADS * HEAD_DIM
ATTN_DIM = ATTN_HEADS * HEAD_DIM
RET_DIM = RET_HEADS * HEAD_DIM
DECAY_LORA = 64
ICL_LORA = 64
GATE_LORA = 128
RWKV_IN = 3 * RWKV_DIM + DECAY_LORA + ICL_LORA + GATE_LORA
ATTN_IN = 3 * ATTN_DIM
RET_IN = 4 * RET_DIM
RWKV_GN_EPS = 64e-5
DECAY_SCALE = math.exp(-0.5)
DILATED_PATTERNS = ((128, 1), (512, 4), (2048, 16))
NUM_BUCKETS = 32
MAX_DISTANCE = 2048
ROPE_BASE = 10000.0
N_EXPERTS = 8
TOP_K = 2
LN_EPS = 1e-5

LANES = 128
WKV_CHUNK = 64
WKV_BLOCK = 1024
WKV_MASK_ROWS = 256
WKV_GROUP = 4
ATTN_W = 128
ATTN_UNROLL = 4
ATTN_MERGE_ROWS = 256
INPROJ_N_CHUNK = 512
RET_CHUNK = 128
RET_BLOCK = 1024
RET_UNROLL = 4
ROW_TILE = 512
MOE_TILE = 512
MOE_F_BLOCK = 1792
FFN_SUBCHUNK = 256
MASK_VALUE = -1e30
VMEM_LIMIT = 56 * 1024 * 1024


_DIMS = {"nn": (((1,), (0,)), ((), ())), "nt": (((1,), (1,)), ((), ())), "tn": (((0,), (0,)), ((), ()))}


def _dot(a, b):
    return jnp.dot(a, b, preferred_element_type=F32)


def _dot_nt(a, b):
    return lax.dot_general(a, b, _DIMS["nt"], preferred_element_type=F32)


def _dot_tn(a, b):
    return lax.dot_general(a, b, _DIMS["tn"], preferred_element_type=F32)


def _split(x, terms):
    parts = []
    for _ in range(terms - 1):
        hi = x.astype(BF16)
        parts.append(hi)
        x = x - hi.astype(F32)
    parts.append(x.astype(BF16))
    return parts


def _mm(a, b, kind="nn", passes=3):
    dg = lambda p, q: lax.dot_general(p, q, _DIMS[kind], preferred_element_type=F32)
    if passes == 1:
        return dg(a.astype(BF16), b.astype(BF16))
    ah, al = _split(a, 2)
    bh, bl = _split(b, 2)
    return dg(ah, bh) + (dg(al, bh) + dg(ah, bl))


def _mm_ones(x, ones_bf16, ones_first=False, terms=3):
    parts = _split(x, terms)
    m, n = x.shape
    if ones_first:
        full = _dot(ones_bf16, jnp.concatenate(parts, axis=1))
        out = [full[:, i * n:(i + 1) * n] for i in range(terms)]
    else:
        full = _dot(jnp.concatenate(parts, axis=0), ones_bf16)
        out = [full[i * m:(i + 1) * m] for i in range(terms)]
    acc = out[-1]
    for o in reversed(out[:-1]):
        acc = acc + o
    return acc


def _sigmoid(x):
    return 1.0 / (1.0 + jnp.exp(-x))


def _layer_norm(z, g, b):
    mu = jnp.mean(z, axis=-1, keepdims=True)
    d = z - mu
    var = jnp.mean(d * d, axis=-1, keepdims=True)
    return d * lax.rsqrt(var + LN_EPS) * g + b


def _params(sem, vmem=VMEM_LIMIT):
    return pltpu.CompilerParams(dimension_semantics=sem, vmem_limit_bytes=vmem)


def _const_spec(shape):
    nd = len(shape)
    return pl.BlockSpec(shape, lambda *_: (0,) * nd)


def _layer_spec(stacked, layer):
    nd = stacked.ndim - 1
    return pl.BlockSpec((None,) + stacked.shape[1:], lambda *_: (layer,) + (0,) * nd)


CAST_BLOCK_BYTES = 8 * 1024 * 1024


def _cast_kernel(x_ref, o_ref):
    o_ref[...] = x_ref[...].astype(o_ref.dtype)


def layer_to_bf16(w, layer):
    shape = w.shape
    cols = shape[-1]
    rows = w[0].size // cols
    w3 = w.reshape(shape[0], rows, cols)
    tr = 1 << int(math.log2(max(16, min(rows, CAST_BLOCK_BYTES // (4 * cols)))))
    while rows % tr:
        tr //= 2
    assert tr % 16 == 0
    out = pl.pallas_call(
        _cast_kernel,
        grid=(rows // tr,),
        in_specs=[pl.BlockSpec((None, tr, cols), lambda i: (layer, i, 0))],
        out_specs=pl.BlockSpec((tr, cols), lambda i: (i, 0)),
        out_shape=jax.ShapeDtypeStruct((rows, cols), BF16),
        compiler_params=_params(("parallel",)),
        name="layer_to_bf16",
    )(w3)
    return out.reshape((1,) + shape[1:])


def _inproj_kernel(x_ref, w_ref, o_ref, *, n_chunk):
    xb = x_ref[...].astype(BF16)
    for n0 in range(0, o_ref.shape[1], n_chunk):
        o_ref[:, n0:n0 + n_chunk] = _dot(xb, w_ref[:, n0:n0 + n_chunk])


def in_projection(x2d, w_bf16, layer):
    t, d = x2d.shape
    n = w_bf16.shape[2]
    tm = min(ROW_TILE, t)
    return pl.pallas_call(
        functools.partial(_inproj_kernel, n_chunk=INPROJ_N_CHUNK),
        grid=(t // tm,),
        in_specs=[pl.BlockSpec((tm, d), lambda i: (i, 0)), _layer_spec(w_bf16, layer)],
        out_specs=pl.BlockSpec((tm, n), lambda i: (i, 0)),
        out_shape=jax.ShapeDtypeStruct((t, n), F32),
        compiler_params=_params(("parallel",)),
        name="in_projection",
    )(x2d, w_bf16)


def _rwkv_kernel(p_ref, mu_ref, w0_ref, wup_ref, a0_ref, aup_ref, gup_ref, kk_ref, ka_ref, rk_ref,
                 lng_ref, lnb_ref, ltri_ref, same_ref, hsum_ref, o_ref,
                 state_s, prev_s, kt_s, rt_s, bt_s, kn_s, v_s, btg_s, kng_s, etot_s, y_s, rp_s, y0_s, gt_s, zt_s):
    c = WKV_CHUNK
    tb = p_ref.shape[0]
    d = RWKV_DIM
    assert c == HEAD_DIM

    @pl.when(pl.program_id(1) == 0)
    def _():
        state_s[...] = jnp.zeros_like(state_s)
        prev_s[...] = jnp.zeros_like(prev_s)

    p = p_ref[...]
    row = lax.broadcasted_iota(jnp.int32, p.shape, 0)
    shifted = jnp.where(row == 0, prev_s[...], pltpu.roll(p, 1, axis=0))
    prev_s[...] = p[tb - 1:tb, :]
    ps = p + (shifted - p) * mu_ref[...]
    r = ps[:, 0:d]
    k = ps[:, d:2 * d]
    v = ps[:, 2 * d:3 * d]
    xw = ps[:, 3 * d:3 * d + DECAY_LORA]
    xa = ps[:, 3 * d + DECAY_LORA:3 * d + DECAY_LORA + ICL_LORA]
    xg = ps[:, 3 * d + DECAY_LORA + ICL_LORA:]

    hsum = hsum_ref[...]
    logw = -DECAY_SCALE * _sigmoid(w0_ref[...] + _mm(jnp.tanh(xw), wup_ref[...]))
    a = _sigmoid(a0_ref[...] + _mm(xa, aup_ref[...]))
    g = _dot(_sigmoid(xg).astype(BF16), gup_ref[...].astype(BF16))
    kap = k * kk_ref[...]
    kap = kap * lax.rsqrt(jnp.maximum(_mm_ones(kap * kap, hsum), 1e-24))
    kn = k * (1.0 + (a - 1.0) * ka_ref[...])
    mb = ltri_ref.shape[0]
    ones2 = jnp.concatenate([ltri_ref[...], same_ref[...]], axis=0)
    sums = [_mm_ones(logw[s0:s0 + mb], ones2, ones_first=True) for s0 in range(0, tb, mb)]
    cum = jnp.concatenate([s[:mb] for s in sums], axis=0)
    tot = jnp.concatenate([s[mb:] for s in sums], axis=0)
    e_neg = jnp.exp(-cum)
    e_rem = jnp.exp(tot - cum)
    nb = -(a * kap)
    kt_s[...] = kap * jnp.exp(cum - logw)
    rt_s[...] = r * jnp.exp(cum)
    bt_s[...] = nb * e_neg
    kn_s[...] = kn * e_neg
    btg_s[...] = nb * e_rem
    kng_s[...] = kn * e_rem
    etot_s[...] = jnp.exp(tot)
    v_s[...] = v

    nh = RWKV_HEADS
    ri = lax.broadcasted_iota(jnp.int32, (c, d), 0)
    ci = lax.broadcasted_iota(jnp.int32, (c, d), 1) % HEAD_DIM
    strict = ci < ri
    incl = ci <= ri
    diag = ci == ri
    eye = diag.astype(F32)
    bi = lax.broadcasted_iota(jnp.int32, (d, d), 0) // HEAD_DIM
    bj = lax.broadcasted_iota(jnp.int32, (d, d), 1) // HEAD_DIM
    blocks = bi == bj
    zero16 = jnp.zeros((), BF16)

    def expand(x16):
        return jnp.where(blocks, jnp.concatenate([x16] * nh, axis=0), zero16)

    def bdmm(a, y, kind="nn", passes=3):
        dg = lambda p_, q_: lax.dot_general(p_, q_, _DIMS[kind], preferred_element_type=F32)
        if passes == 1:
            return dg(a.astype(BF16), expand(y.astype(BF16)))
        ah, al = _split(a, 2)
        yh, yl = _split(y, 2)
        m = a.shape[0]
        both = dg(jnp.concatenate([ah, al], axis=0), expand(yh))
        return both[:m] + (both[m:] + dg(ah, expand(yl)))

    def block_diagonal_of(full):
        outs = []
        for n0 in range(0, full.shape[1], d):
            m = jnp.where(blocks, full[:, n0:n0 + d], 0.0)
            acc = m[0:c]
            for h in range(1, nh):
                acc = acc + m[h * c:(h + 1) * c]
            outs.append(acc)
        return outs

    nchunk = tb // c
    cat0 = lambda x, y: jnp.concatenate([x, y], axis=0)
    cat1 = lambda x, y: jnp.concatenate([x, y], axis=1)
    levels = int(math.log2(c)) - 1

    def independent_part(chunks):
        get = lambda ref: [ref[j * c:(j + 1) * c, :] for j in chunks]
        kt, rt, vv, btg = get(kt_s), get(rt_s), get(v_s), get(btg_s)
        lhs = [cat0(k_, r_) for k_, r_ in zip(kt, rt)]
        a_b = [bdmm(l_, b_, "nt") for l_, b_ in zip(lhs, get(bt_s))]
        yield
        a_k = [bdmm(l_, n_, "nt") for l_, n_ in zip(lhs, get(kn_s))]
        yield
        a_ab = [jnp.where(strict, m[:c], 0.0) for m in a_b]
        a_rb = [jnp.where(incl, m[c:], 0.0) for m in a_b]
        a_kr = [cat0(jnp.where(strict, m[:c], 0.0), jnp.where(incl, m[c:], 0.0)) for m in a_k]
        inv = [eye + m for m in a_ab]
        pw = [bdmm(m, m, passes=1) for m in a_ab]
        yield
        for lvl in range(levels):
            if lvl < levels - 1:
                both = [bdmm(cat0(x_, p_), p_, passes=1) for x_, p_ in zip(inv, pw)]
                inv = [x_ + b_[:c] for x_, b_ in zip(inv, both)]
                pw = [b_[c:] for b_ in both]
            else:
                inv = [x_ + bdmm(x_, p_, passes=1) for x_, p_ in zip(inv, pw)]
            yield
        av = [bdmm(m, v_) for m, v_ in zip(a_kr, vv)]
        yield
        wmat = [bdmm(x_, k_, passes=1) for x_, k_ in zip(inv, kt)]
        umat = [bdmm(x_, a_[:c], passes=1) for x_, a_ in zip(inv, av)]
        yield
        rw = [bdmm(m, w_, passes=1) for m, w_ in zip(a_rb, wmat)]
        ru = [bdmm(m, u_, passes=1) for m, u_ in zip(a_rb, umat)]
        yield
        gz = [block_diagonal_of(_mm(b_, cat1(w_, u_), "tn", passes=1))
              for b_, w_, u_ in zip(btg, wmat, umat)]
        kv = [block_diagonal_of(_mm(n_, v_, "tn"))[0] for n_, v_ in zip(get(kng_s), vv)]
        for i, j in enumerate(chunks):
            rows = slice(j * c, (j + 1) * c)
            rp_s[rows, :] = rt[i] + rw[i]
            y0_s[rows, :] = ru[i] + av[i][c:]
            g_diag = jnp.where(diag, jnp.broadcast_to(etot_s[j * c:j * c + 1, :], (c, d)), 0.0)
            gt_s[rows, :] = g_diag + gz[i][0]
            zt_s[rows, :] = gz[i][1] + kv[i]

    state = [state_s[...]]

    def sequential_part(chunks):
        for j in chunks:
            rows = slice(j * c, (j + 1) * c)
            ry = bdmm(cat0(rp_s[rows, :], gt_s[rows, :]), state[0])
            y_s[rows, :] = ry[:c] + y0_s[rows, :]
            state[0] = ry[c:] + zt_s[rows, :]
            yield

    def output_part(chunks):
        rows = slice(chunks[0] * c, (chunks[-1] + 1) * c)
        y = y_s[rows, :]
        mean = _mm_ones(y, hsum, terms=2) * (1.0 / HEAD_DIM)
        yield
        dy = y - mean
        var = _mm_ones(dy * dy, hsum, terms=2) * (1.0 / HEAD_DIM)
        yield
        yn = dy * lax.rsqrt(var + RWKV_GN_EPS) * lng_ref[...] + lnb_ref[...]
        bonus = _mm_ones(r[rows] * kn[rows] * rk_ref[...], hsum, terms=2) * v[rows]
        yield
        o_ref[rows, :] = ((yn + bonus) * g[rows]).astype(o_ref.dtype)

    groups = [range(g0, min(g0 + WKV_GROUP, nchunk)) for g0 in range(0, nchunk, WKV_GROUP)]
    pending = iter(())
    for chunks in groups:
        for _ in independent_part(chunks):
            next(pending, None)
        for _ in pending:
            pass
        pending = sequential_part(chunks)
    for chunks in groups[:-1]:
        for _ in output_part(chunks):
            next(pending, None)
    for _ in pending:
        pass
    state_s[...] = state[0]
    for _ in output_part(groups[-1]):
        pass


def _chunk_masks(tb, c):
    i = np.arange(tb)
    same = (i[:, None] // c) == (i[None, :] // c)
    ltri = same & (i[None, :] <= i[:, None])
    return jnp.asarray(ltri, BF16), jnp.asarray(same, BF16)


def _head_sum_matrix(width):
    i = np.arange(width)
    return jnp.asarray((i[:, None] // HEAD_DIM) == (i[None, :] // HEAD_DIM), BF16)


def rwkv_time_mix(p, batch, seq, mu, w0, w_up, a0, a_up, g_up, k_k, k_a, r_k, ln_g, ln_b):
    t = batch * seq
    tb = min(WKV_BLOCK, seq)
    nblk = seq // tb
    ltri, same = _chunk_masks(min(WKV_MASK_ROWS, tb), WKV_CHUNK)
    hsum = _head_sum_matrix(RWKV_DIM)
    row = lambda a: a.reshape(1, -1)
    consts = [row(mu), row(w0), w_up, row(a0), a_up, g_up, row(k_k), row(k_a), row(r_k), row(ln_g), row(ln_b),
              ltri, same, hsum]
    buf = lambda: pltpu.VMEM((tb, RWKV_DIM), F32)
    return pl.pallas_call(
        _rwkv_kernel,
        grid=(batch, nblk),
        in_specs=[pl.BlockSpec((tb, RWKV_IN), lambda b, j: (b * nblk + j, 0))]
                 + [_const_spec(a.shape) for a in consts],
        out_specs=pl.BlockSpec((tb, RWKV_DIM), lambda b, j: (b * nblk + j, 0)),
        out_shape=jax.ShapeDtypeStruct((t, RWKV_DIM), BF16),
        scratch_shapes=[pltpu.VMEM((HEAD_DIM, RWKV_DIM), F32), pltpu.VMEM((1, RWKV_IN), F32)]
                       + [buf() for _ in range(13)],
        compiler_params=_params(("parallel", "arbitrary")),
        name="rwkv_time_mix",
    )(p, *consts)


def _attn_kernel(q_ref, k_ref, v_ref, bias_ref, o_ref, acc_s, m_s, l_s):
    seq = q_ref.shape[0]
    w = ATTN_W
    scale = HEAD_DIM ** -0.5

    def rows_of(start, dil):
        return pl.ds(start, w) if dil == 1 else pl.ds(start, w, stride=dil)

    lane = lax.broadcasted_iota(jnp.int32, (w, LANES), 1)
    head0 = lane < HEAD_DIM
    zero = jnp.zeros((), BF16)
    one = jnp.ones((), BF16)

    def group(pi, dil, g, firsts):
        rows_l, q_l, k_l, v_l = [], [], [], []
        for u, first in enumerate(firsts):
            b = g * len(firsts) + u
            start = (b % dil) + (b // dil) * (dil * w)
            rows = rows_of(start, dil)
            q = (q_ref[rows, :] * scale).astype(BF16)
            kk = k_ref[rows, :].astype(BF16)
            vv = v_ref[rows, :].astype(BF16)
            if not first:
                prev = rows_of(start - dil * w, dil)
                kk = jnp.concatenate([k_ref[prev, :].astype(BF16), kk], axis=0)
                vv = jnp.concatenate([v_ref[prev, :].astype(BF16), vv], axis=0)
            rows_l.append(rows)
            q_l.append(q)
            k_l.append(kk)
            v_l.append(vv)
        s = [[_dot_nt(jnp.where(head0 if h == 0 else ~head0, q, zero), kk)
              + (bias_ref[pi, h, :, w:] if first else bias_ref[pi, h])
              for h in range(2)] for q, kk, first in zip(q_l, k_l, firsts)]
        m = [[jnp.max(sh, axis=-1, keepdims=True) for sh in su] for su in s]
        pr = [[jnp.exp(sh - mh).astype(BF16) for sh, mh in zip(su, mu)] for su, mu in zip(s, m)]
        kmask = lambda vv: lax.broadcasted_iota(jnp.int32, vv.shape, 1) < HEAD_DIM
        res = [[_dot(pu[0], jnp.where(kmask(vv), vv, one)), _dot(pu[1], jnp.where(kmask(vv), one, vv))]
               for pu, vv in zip(pr, v_l)]
        for rows, ru, mu in zip(rows_l, res, m):
            acc_s[pi, rows, :] = jnp.where(head0, ru[0], ru[1])
            l_s[pi, rows, :] = jnp.where(head0, ru[1], ru[0])
            m_s[pi, rows, :] = jnp.where(head0, mu[0], mu[1])

    n_groups = (seq // w) // ATTN_UNROLL
    for pi, (window, dil) in enumerate(DILATED_PATTERNS):
        for g in range(n_groups):
            group(pi, dil, g, tuple((g * ATTN_UNROLL + u) < dil for u in range(ATTN_UNROLL)))

    mt = ATTN_MERGE_ROWS

    def merge_body(i, carry):
        rows = pl.ds(pl.multiple_of(i * mt, mt), mt)
        m0, m1, m2 = m_s[0, rows, :], m_s[1, rows, :], m_s[2, rows, :]
        mx = jnp.maximum(jnp.maximum(m0, m1), m2)
        w0, w1, w2 = jnp.exp(m0 - mx), jnp.exp(m1 - mx), jnp.exp(m2 - mx)
        num = w0 * acc_s[0, rows, :] + w1 * acc_s[1, rows, :] + w2 * acc_s[2, rows, :]
        swap = lambda x: pltpu.roll(x, HEAD_DIM, axis=1)
        den = w0 * swap(l_s[0, rows, :]) + w1 * swap(l_s[1, rows, :]) + w2 * swap(l_s[2, rows, :])
        o_ref[rows, :] = (num / den).astype(o_ref.dtype)
        return carry

    lax.fori_loop(0, seq // mt, merge_body, 0)


def _t5_bucket(dist):
    max_exact = NUM_BUCKETS // 2
    large = max_exact + (np.log(np.maximum(dist, max_exact) / max_exact)
                         / math.log(MAX_DISTANCE / max_exact) * (NUM_BUCKETS - max_exact)).astype(np.int32)
    return np.where(dist < max_exact, dist, np.minimum(large, NUM_BUCKETS - 1)).astype(np.int32)


def _attn_bias(rel_bias):
    w = ATTN_W
    i = np.arange(w)[:, None]
    j = np.arange(2 * w)[None, :]
    rel = i + w - j
    band = (rel >= 0) & (rel <= w)
    tabs = []
    for window, dil in DILATED_PATTERNS:
        bucket = _t5_bucket(np.clip(rel, 0, None) * dil)
        onehot = jnp.asarray(bucket[..., None] == np.arange(NUM_BUCKETS), F32)
        bias = jnp.einsum('ijb,bh->hij', onehot, rel_bias.astype(F32), precision=HI)
        tabs.append(jnp.where(band[None], bias, MASK_VALUE))
    return jnp.stack(tabs)


def dilated_attention(p, batch, seq, rel_bias):
    t = batch * seq
    bias = _attn_bias(rel_bias)
    col0 = RWKV_IN // LANES
    npair = ATTN_DIM // LANES
    spec = lambda off: pl.BlockSpec((seq, LANES), lambda b, hp: (b, col0 + off + hp))
    return pl.pallas_call(
        _attn_kernel,
        grid=(batch, npair),
        in_specs=[spec(0), spec(npair), spec(2 * npair),
                  pl.BlockSpec((3, 2, ATTN_W, 2 * ATTN_W), lambda b, hp: (0, hp, 0, 0))],
        out_specs=pl.BlockSpec((seq, LANES), lambda b, hp: (b, hp)),
        out_shape=jax.ShapeDtypeStruct((t, ATTN_DIM), BF16),
        scratch_shapes=[pltpu.VMEM((3, seq, LANES), F32) for _ in range(3)],
        compiler_params=_params(("parallel", "parallel")),
        name="dilated_attention",
    )(p, p, p, bias)


def _ret_kernel(q_ref, k_ref, v_ref, g_ref, cos_ref, sin_ref, dmat_ref, xi_ref, zeta_ref, gng_ref, gnb_ref,
                hsum_ref, o_ref, state_s):
    c = RET_CHUNK
    tb = q_ref.shape[0]

    @pl.when(pl.program_id(1) == 0)
    def _():
        state_s[...] = jnp.zeros_like(state_s)

    lane = lax.broadcasted_iota(jnp.int32, (c, RET_DIM), 1)
    first_half = (lane % HEAD_DIM) < (HEAD_DIM // 2)

    def rotate(x, cos, sin):
        swapped = jnp.where(first_half, pltpu.roll(x, RET_DIM - HEAD_DIM // 2, axis=1),
                            pltpu.roll(x, HEAD_DIM // 2, axis=1))
        return x * cos + swapped * sin

    heads = range(RET_HEADS)
    hsl = [slice(h * HEAD_DIM, (h + 1) * HEAD_DIM) for h in heads]
    chunk_decay = [(1.0 - 2.0 ** (-5.0 - h)) ** c for h in heads]

    def group_body(g, carry):
        rows_l, qb, kb, qx, kz, vb = [], [], [], [], [], []
        for u in range(RET_UNROLL):
            rows = pl.ds(pl.multiple_of((g * RET_UNROLL + u) * c, c), c)
            cos, sin = cos_ref[rows, :], sin_ref[rows, :]
            q = rotate(q_ref[rows, :], cos, sin)
            k = rotate(k_ref[rows, :], cos, sin) * (HEAD_DIM ** -0.5)
            rows_l.append(rows)
            qb.append(q.astype(BF16))
            kb.append(k.astype(BF16))
            qx.append((q * xi_ref[...]).astype(BF16))
            kz.append((k * zeta_ref[...]).astype(BF16))
            vb.append(v_ref[rows, :].astype(BF16))
        sc = [[(_dot_nt(qb[u][:, sl], kb[u][:, sl]) * dmat_ref[h]).astype(BF16) for h, sl in zip(heads, hsl)]
              for u in range(RET_UNROLL)]
        intra = [[_dot(sc[u][h], vb[u][:, hsl[h]]) for h in heads] for u in range(RET_UNROLL)]
        kv = [[_dot_tn(kz[u][:, sl], vb[u][:, sl]) for sl in hsl] for u in range(RET_UNROLL)]
        states = [state_s[h] for h in heads]
        ys = []
        for u in range(RET_UNROLL):
            ys.append(jnp.concatenate(
                [intra[u][h] + _dot(qx[u][:, hsl[h]], states[h].astype(BF16)) for h in heads], axis=1))
            states = [states[h] * chunk_decay[h] + kv[u][h] for h in heads]
        for h in heads:
            state_s[h] = states[h]
        hsum = hsum_ref[...]
        for rows, y in zip(rows_l, ys):
            mean = _mm_ones(y, hsum, terms=2) * (1.0 / HEAD_DIM)
            dy = y - mean
            var = _mm_ones(dy * dy, hsum, terms=2) * (1.0 / HEAD_DIM)
            yn = dy * lax.rsqrt(var + LN_EPS) * gng_ref[...] + gnb_ref[...]
            gate = g_ref[rows, :]
            o_ref[rows, :] = (gate * _sigmoid(gate) * yn).astype(o_ref.dtype)
        return carry

    lax.fori_loop(0, tb // (c * RET_UNROLL), group_body, 0)


def _ret_tables(seq):
    c = RET_CHUNK
    half = HEAD_DIM // 2
    inv = ROPE_BASE ** (-jnp.arange(half, dtype=F32) / half)
    ang = jnp.arange(seq, dtype=F32)[:, None] * inv
    cos, sin = jnp.cos(ang), jnp.sin(ang)
    cos_t = jnp.tile(jnp.concatenate([cos, cos], axis=1), (1, RET_HEADS))
    sin_t = jnp.tile(jnp.concatenate([-sin, sin], axis=1), (1, RET_HEADS))
    log_g = jnp.log1p(-jnp.exp2(-5.0 - jnp.arange(RET_HEADS, dtype=F32)))
    n = jnp.arange(c, dtype=F32)
    diff = n[:, None] - n[None, :]
    dmat = jnp.where(diff >= 0, jnp.exp(log_g[:, None, None] * jnp.maximum(diff, 0.0)), 0.0)
    zeta = jnp.exp(log_g[:, None] * (c - 1 - n))
    xi = jnp.exp(log_g[:, None] * (n + 1))
    widen = lambda tab: jnp.repeat(tab.T, HEAD_DIM, axis=1)
    return cos_t, sin_t, dmat, widen(xi), widen(zeta)


def retention(p, batch, seq, gn_g, gn_b):
    t = batch * seq
    tb = min(RET_BLOCK, seq)
    nblk = seq // tb
    cos_t, sin_t, dmat, xi, zeta = _ret_tables(seq)
    hsum = _head_sum_matrix(RET_DIM)
    col0 = (RWKV_IN + ATTN_IN) // RET_DIM
    spec = lambda off: pl.BlockSpec((tb, RET_DIM), lambda b, j: (b * nblk + j, col0 + off))
    tab = pl.BlockSpec((tb, RET_DIM), lambda b, j: (j, 0))
    consts = [dmat, xi, zeta, gn_g.reshape(1, -1), gn_b.reshape(1, -1), hsum]
    return pl.pallas_call(
        _ret_kernel,
        grid=(batch, nblk),
        in_specs=[spec(0), spec(1), spec(2), spec(3), tab, tab] + [_const_spec(a.shape) for a in consts],
        out_specs=pl.BlockSpec((tb, RET_DIM), lambda b, j: (b * nblk + j, 0)),
        out_shape=jax.ShapeDtypeStruct((t, RET_DIM), BF16),
        scratch_shapes=[pltpu.VMEM((RET_HEADS, HEAD_DIM, HEAD_DIM), F32)],
        compiler_params=_params(("parallel", "arbitrary")),
        name="retention",
    )(p, p, p, p, cos_t, sin_t, *consts)


def _outproj_kernel(ya_ref, yb_ref, yc_ref, x_ref, w_ref, g_ref, b_ref, *rest, alpha):
    acc = _dot(ya_ref[...], w_ref[0:RWKV_DIM, :])
    acc += _dot(yb_ref[...], w_ref[RWKV_DIM:RWKV_DIM + ATTN_DIM, :])
    acc += _dot(yc_ref[...], w_ref[RWKV_DIM + ATTN_DIM:, :])
    h = _layer_norm(alpha * x_ref[...] + acc, g_ref[...], b_ref[...])
    if len(rest) == 1:
        (o_ref,) = rest
    else:
        router_ref, o_ref, logit_ref = rest
        logit_ref[...] = _dot_nt(router_ref[...].astype(BF16), h.astype(BF16))
    o_ref[...] = h


def out_projection_ln(ya, yb, yc, x2d, w_bf16, layer, g, b, alpha, router=None):
    t, d = x2d.shape
    tm = min(ROW_TILE, t)
    rows = lambda width: pl.BlockSpec((tm, width), lambda i: (i, 0))
    in_specs = [rows(RWKV_DIM), rows(ATTN_DIM), rows(RET_DIM), rows(d), _layer_spec(w_bf16, layer),
                _const_spec((1, d)), _const_spec((1, d))]
    args = [ya, yb, yc, x2d, w_bf16, g.reshape(1, -1), b.reshape(1, -1)]
    out_specs, out_shape = rows(d), jax.ShapeDtypeStruct((t, d), F32)
    if router is not None:
        in_specs.append(_const_spec((N_EXPERTS, d)))
        args.append(router.T)
        out_specs = [out_specs, pl.BlockSpec((N_EXPERTS, tm), lambda i: (0, i))]
        out_shape = [out_shape, jax.ShapeDtypeStruct((N_EXPERTS, t), F32)]
    return pl.pallas_call(
        functools.partial(_outproj_kernel, alpha=alpha),
        grid=(t // tm,),
        in_specs=in_specs,
        out_specs=out_specs,
        out_shape=out_shape,
        compiler_params=_params(("parallel",)),
        name="out_projection_ln",
    )(*args)


def _swiglu(xb, wg, wu, wd, acc, f_chunk):
    ff = wg.shape[-1]
    starts = list(range(0, ff, f_chunk))
    gate_up = lambda f0: (_dot(xb, wg[:, f0:f0 + f_chunk]), _dot(xb, wu[:, f0:f0 + f_chunk]))
    nxt = gate_up(starts[0])
    for n, f0 in enumerate(starts):
        gate, up = nxt
        if n + 1 < len(starts):
            nxt = gate_up(starts[n + 1])
        hid = (gate * _sigmoid(gate) * up).astype(BF16)
        part = _dot(hid, wd[f0:f0 + f_chunk, :])
        acc = part if acc is None else acc + part
    return acc


def _ffn_kernel(x_ref, wg_ref, wu_ref, wd_ref, g_ref, b_ref, o_ref, *, alpha, f_chunk):
    x = x_ref[...]
    acc = _swiglu(x.astype(BF16), wg_ref, wu_ref, wd_ref, alpha * x, f_chunk)
    o_ref[...] = _layer_norm(acc, g_ref[...], b_ref[...])


def dense_ffn_ln(x2d, wg, wu, wd, layer, g, b, alpha):
    t, d = x2d.shape
    tm = min(ROW_TILE, t)
    return pl.pallas_call(
        functools.partial(_ffn_kernel, alpha=alpha, f_chunk=FFN_SUBCHUNK),
        grid=(t // tm,),
        in_specs=[pl.BlockSpec((tm, d), lambda i: (i, 0)), _layer_spec(wg, layer), _layer_spec(wu, layer),
                  _layer_spec(wd, layer), _const_spec((1, d)), _const_spec((1, d))],
        out_specs=pl.BlockSpec((tm, d), lambda i: (i, 0)),
        out_shape=jax.ShapeDtypeStruct((t, d), F32),
        compiler_params=_params(("parallel",)),
        name="dense_ffn_ln",
    )(x2d, wg, wu, wd, g.reshape(1, -1), b.reshape(1, -1))


def _expert_kernel(blk_e_ref, used_ref, x_ref, wg_ref, wu_ref, wd_ref, o_ref, acc_s):
    i, j = pl.program_id(0), pl.program_id(1)
    last = pl.num_programs(1) - 1

    @pl.when(i < used_ref[0])
    def _():
        part = _swiglu(x_ref[...].astype(BF16), wg_ref.at[0], wu_ref.at[0], wd_ref.at[0], None, FFN_SUBCHUNK)

        @pl.when(j == 0)
        def _():
            acc_s[...] = part

        @pl.when((j > 0) & (j < last))
        def _():
            acc_s[...] += part

        @pl.when(j == last)
        def _():
            o_ref[...] = (acc_s[...] + part).astype(o_ref.dtype)


def expert_ffn(xs, blk_e, n_used, wg, wu, wd, layer, f_chunk=MOE_F_BLOCK):
    rows, d = xs.shape
    ff = wg.shape[3]
    tm = MOE_TILE
    assert ff // f_chunk >= 2
    grid_spec = pltpu.PrefetchScalarGridSpec(
        num_scalar_prefetch=2,
        grid=(rows // tm, ff // f_chunk),
        in_specs=[pl.BlockSpec((tm, d), lambda i, j, be, nu: (i, 0)),
                  pl.BlockSpec((None, 1, d, f_chunk), lambda i, j, be, nu: (layer, be[i], 0, j)),
                  pl.BlockSpec((None, 1, d, f_chunk), lambda i, j, be, nu: (layer, be[i], 0, j)),
                  pl.BlockSpec((None, 1, f_chunk, d), lambda i, j, be, nu: (layer, be[i], j, 0))],
        out_specs=pl.BlockSpec((tm, d), lambda i, j, be, nu: (i, 0)),
        scratch_shapes=[pltpu.VMEM((tm, d), F32)],
    )
    return pl.pallas_call(
        _expert_kernel,
        grid_spec=grid_spec,
        out_shape=jax.ShapeDtypeStruct((rows, d), BF16),
        compiler_params=_params(("parallel", "arbitrary")),
        name="expert_ffn",
    )(blk_e, n_used, xs, wg, wu, wd)


def _combine_ln_kernel(x_ref, y0_ref, y1_ref, gate_ref, g_ref, b_ref, o_ref, *, alpha):
    gates = gate_ref[...]
    f = y0_ref[...] * gates[:, 0:1] + y1_ref[...] * gates[:, 1:2]
    o_ref[...] = _layer_norm(alpha * x_ref[...] + f, g_ref[...], b_ref[...])


def combine_ln(x2d, y0, y1, gates, g, b, alpha):
    t, d = x2d.shape
    tm = min(ROW_TILE, t)
    rows = pl.BlockSpec((tm, d), lambda i: (i, 0))
    return pl.pallas_call(
        functools.partial(_combine_ln_kernel, alpha=alpha),
        grid=(t // tm,),
        in_specs=[rows, rows, rows, pl.BlockSpec((tm, TOP_K), lambda i: (i, 0)),
                  _const_spec((1, d)), _const_spec((1, d))],
        out_specs=rows,
        out_shape=jax.ShapeDtypeStruct((t, d), F32),
        compiler_params=_params(("parallel",)),
        name="combine_ln",
    )(x2d, y0, y1, gates, g.reshape(1, -1), b.reshape(1, -1))


def moe_ffn_ln(x2d, logits_t, wg, wu, wd, layer, g, b, alpha):
    t, d = x2d.shape
    tm = MOE_TILE
    logits = logits_t.T
    top_val, top_idx = lax.top_k(logits, TOP_K)
    gates = jax.nn.softmax(top_val, axis=-1)
    member = (top_idx[:, :, None] == jnp.arange(N_EXPERTS)[None, None, :]).any(axis=1)
    counts = member.sum(axis=0).astype(jnp.int32)
    rank = jnp.cumsum(member.astype(jnp.int32), axis=0) - member.astype(jnp.int32)
    padded = (counts + tm - 1) // tm * tm
    pad_ends = jnp.cumsum(padded)
    pad_starts = pad_ends - padded
    pos = jnp.take_along_axis(pad_starts[None, :] + rank, top_idx, axis=1)
    rows = t * TOP_K + N_EXPERTS * tm
    order = jnp.argsort(top_idx.reshape(-1), stable=True).astype(jnp.int32)
    tok_sorted = order // TOP_K
    starts = jnp.cumsum(counts) - counts
    r = jnp.arange(rows, dtype=jnp.int32)
    row_e = jnp.minimum(jnp.searchsorted(pad_ends, r, side='right'), N_EXPERTS - 1).astype(jnp.int32)
    within = r - pad_starts[row_e]
    src = jnp.where(within < counts[row_e], tok_sorted[jnp.minimum(starts[row_e] + within, t * TOP_K - 1)], 0)
    blk_e = row_e[::tm]
    n_used = (pad_ends[-1:] // tm).astype(jnp.int32)
    ys = expert_ffn(x2d[src], blk_e, n_used, wg, wu, wd, layer)
    return combine_ln(x2d, ys[pos[:, 0]], ys[pos[:, 1]], gates, g, b, alpha)


def kernel(x, w_in, w_out, rwkv_mu, rwkv_w0, rwkv_w_up, rwkv_a0, rwkv_a_up, rwkv_g_up, rwkv_k_k, rwkv_k_a,
           rwkv_r_k, rwkv_ln_g, rwkv_ln_b, ret_gn_g, ret_gn_b, rel_bias, ln_g, ln_b, ffn_w_gate, ffn_w_up,
           ffn_w_down, moe_router, moe_w_gate, moe_w_up, moe_w_down):
    batch, seq, d = x.shape
    depth = w_in.shape[0]
    alpha = (2 * depth) ** 0.25
    h = x.reshape(batch * seq, d)
    w_in, w_out = w_in.astype(BF16), w_out.astype(BF16)
    ffn_w_gate, ffn_w_up, ffn_w_down = ffn_w_gate.astype(BF16), ffn_w_up.astype(BF16), ffn_w_down.astype(BF16)
    for layer in range(depth):
        p = in_projection(h, w_in, layer)
        ya = rwkv_time_mix(p, batch, seq, rwkv_mu[layer], rwkv_w0[layer], rwkv_w_up[layer], rwkv_a0[layer],
                           rwkv_a_up[layer], rwkv_g_up[layer], rwkv_k_k[layer], rwkv_k_a[layer],
                           rwkv_r_k[layer], rwkv_ln_g[layer], rwkv_ln_b[layer])
        yb = dilated_attention(p, batch, seq, rel_bias)
        yc = retention(p, batch, seq, ret_gn_g[layer], ret_gn_b[layer])
        j = layer // 2
        if layer % 2 == 0:
            h = out_projection_ln(ya, yb, yc, h, w_out, layer, ln_g[layer, 0], ln_b[layer, 0], alpha)
            h = dense_ffn_ln(h, ffn_w_gate, ffn_w_up, ffn_w_down, j, ln_g[layer, 1], ln_b[layer, 1], alpha)
        else:
            h, logits_t = out_projection_ln(ya, yb, yc, h, w_out, layer, ln_g[layer, 0], ln_b[layer, 0], alpha,
                                            router=moe_router[j])
            h = moe_ffn_ln(h, logits_t, layer_to_bf16(moe_w_gate, j), layer_to_bf16(moe_w_up, j),
                           layer_to_bf16(moe_w_down, j), 0, ln_g[layer, 1], ln_b[layer, 1], alpha)
    return h.reshape(batch, seq, d)
```

```python
import functools
import math

import numpy as np
import jax
import jax.numpy as jnp
from jax import lax
from jax.experimental import pallas as pl
from jax.experimental.pallas import tpu as pltpu

F32 = jnp.float32
BF16 = jnp.bfloat16
HI = lax.Precision.HIGHEST

HEAD_DIM = 64
RWKV_HEADS = 4
ATTN_HEADS = 8
RET_HEADS = 4
RWKV_DIM = RWKV_HEADS * HEAD_DIM
ATTN_DIM = ATTN_HEADS * HEAD_DIM
RET_DIM = RET_HEADS * HEAD_DIM
DECAY_LORA = 64
ICL_LORA = 64
GATE_LORA = 128
RWKV_IN = 3 * RWKV_DIM + DECAY_LORA + ICL_LORA + GATE_LORA
ATTN_IN = 3 * ATTN_DIM
RET_IN = 4 * RET_DIM
RWKV_GN_EPS = 64e-5
DECAY_SCALE = math.exp(-0.5)
DILATED_PATTERNS = ((128, 1), (512, 4), (2048, 16))
NUM_BUCKETS = 32
MAX_DISTANCE = 2048
ROPE_BASE = 10000.0
N_EXPERTS = 8
TOP_K = 2
LN_EPS = 1e-5

LANES = 128
WKV_CHUNK = 64
WKV_BLOCK = 1024
WKV_MASK_ROWS = 256
WKV_GROUP = 4
ATTN_W = 128
ATTN_UNROLL = 4
ATTN_MERGE_ROWS = 256
INPROJ_N_CHUNK = 512
RET_CHUNK = 128
RET_BLOCK = 1024
RET_UNROLL = 4
ROW_TILE = 512
MOE_TILE = 512
MOE_F_BLOCK = 1792
FFN_SUBCHUNK = 256
MASK_VALUE = -1e30
VMEM_LIMIT = 56 * 1024 * 1024


_DIMS = {"nn": (((1,), (0,)), ((), ())), "nt": (((1,), (1,)), ((), ())), "tn": (((0,), (0,)), ((), ()))}


def _dot(a, b):
    return jnp.dot(a, b, preferred_element_type=F32)


def _dot_nt(a, b):
    return lax.dot_general(a, b, _DIMS["nt"], preferred_element_type=F32)


def _dot_tn(a, b):
    return lax.dot_general(a, b, _DIMS["tn"], preferred_element_type=F32)


def _split(x, terms):
    parts = []
    for _ in range(terms - 1):
        hi = x.astype(BF16)
        parts.append(hi)
        x = x - hi.astype(F32)
    parts.append(x.astype(BF16))
    return parts


def _mm(a, b, kind="nn", passes=3):
    dg = lambda p, q: lax.dot_general(p, q, _DIMS[kind], preferred_element_type=F32)
    if passes == 1:
        return dg(a.astype(BF16), b.astype(BF16))
    ah, al = _split(a, 2)
    bh, bl = _split(b, 2)
    return dg(ah, bh) + (dg(al, bh) + dg(ah, bl))


def _mm_ones(x, ones_bf16, ones_first=False, terms=3):
    parts = _split(x, terms)
    m, n = x.shape
    if ones_first:
        full = _dot(ones_bf16, jnp.concatenate(parts, axis=1))
        out = [full[:, i * n:(i + 1) * n] for i in range(terms)]
    else:
        full = _dot(jnp.concatenate(parts, axis=0), ones_bf16)
        out = [full[i * m:(i + 1) * m] for i in range(terms)]
    acc = out[-1]
    for o in reversed(out[:-1]):
        acc = acc + o
    return acc


def _sigmoid(x):
    return 1.0 / (1.0 + jnp.exp(-x))


def _layer_norm(z, g, b):
    mu = jnp.mean(z, axis=-1, keepdims=True)
    d = z - mu
    var = jnp.mean(d * d, axis=-1, keepdims=True)
    return d * lax.rsqrt(var + LN_EPS) * g + b


def _params(sem, vmem=VMEM_LIMIT):
    return pltpu.CompilerParams(dimension_semantics=sem, vmem_limit_bytes=vmem)


def _const_spec(shape):
    nd = len(shape)
    return pl.BlockSpec(shape, lambda *_: (0,) * nd)


def _layer_spec(stacked, layer):
    nd = stacked.ndim - 1
    return pl.BlockSpec((None,) + stacked.shape[1:], lambda *_: (layer,) + (0,) * nd)


CAST_BLOCK_BYTES = 8 * 1024 * 1024


def _cast_kernel(x_ref, o_ref):
    o_ref[...] = x_ref[...].astype(o_ref.dtype)


def layer_to_bf16(w, layer):
    shape = w.shape
    cols = shape[-1]
    rows = w[0].size // cols
    w3 = w.reshape(shape[0], rows, cols)
    tr = 1 << int(math.log2(max(16, min(rows, CAST_BLOCK_BYTES // (4 * cols)))))
    while rows % tr:
        tr //= 2
    assert tr % 16 == 0
    out = pl.pallas_call(
        _cast_kernel,
        grid=(rows // tr,),
        in_specs=[pl.BlockSpec((None, tr, cols), lambda i: (layer, i, 0))],
        out_specs=pl.BlockSpec((tr, cols), lambda i: (i, 0)),
        out_shape=jax.ShapeDtypeStruct((rows, cols), BF16),
        compiler_params=_params(("parallel",)),
        name="layer_to_bf16",
    )(w3)
    return out.reshape((1,) + shape[1:])


def _inproj_kernel(x_ref, w_ref, o_ref, *, n_chunk):
    xb = x_ref[...].astype(BF16)
    for n0 in range(0, o_ref.shape[1], n_chunk):
        o_ref[:, n0:n0 + n_chunk] = _dot(xb, w_ref[:, n0:n0 + n_chunk])


def in_projection(x2d, w_bf16, layer):
    t, d = x2d.shape
    n = w_bf16.shape[2]
    tm = min(ROW_TILE, t)
    return pl.pallas_call(
        functools.partial(_inproj_kernel, n_chunk=INPROJ_N_CHUNK),
        grid=(t // tm,),
        in_specs=[pl.BlockSpec((tm, d), lambda i: (i, 0)), _layer_spec(w_bf16, layer)],
        out_specs=pl.BlockSpec((tm, n), lambda i: (i, 0)),
        out_shape=jax.ShapeDtypeStruct((t, n), F32),
        compiler_params=_params(("parallel",)),
        name="in_projection",
    )(x2d, w_bf16)


def _rwkv_kernel(p_ref, mu_ref, w0_ref, wup_ref, a0_ref, aup_ref, gup_ref, kk_ref, ka_ref, rk_ref,
                 lng_ref, lnb_ref, ltri_ref, same_ref, hsum_ref, o_ref,
                 state_s, prev_s, kt_s, rt_s, bt_s, kn_s, v_s, btg_s, kng_s, etot_s, y_s, rp_s, y0_s, gt_s, zt_s):
    c = WKV_CHUNK
    tb = p_ref.shape[0]
    d = RWKV_DIM
    assert c == HEAD_DIM

    @pl.when(pl.program_id(1) == 0)
    def _():
        state_s[...] = jnp.zeros_like(state_s)
        prev_s[...] = jnp.zeros_like(prev_s)

    p = p_ref[...]
    row = lax.broadcasted_iota(jnp.int32, p.shape, 0)
    shifted = jnp.where(row == 0, prev_s[...], pltpu.roll(p, 1, axis=0))
    prev_s[...] = p[tb - 1:tb, :]
    ps = p + (shifted - p) * mu_ref[...]
    r = ps[:, 0:d]
    k = ps[:, d:2 * d]
    v = ps[:, 2 * d:3 * d]
    xw = ps[:, 3 * d:3 * d + DECAY_LORA]
    xa = ps[:, 3 * d + DECAY_LORA:3 * d + DECAY_LORA + ICL_LORA]
    xg = ps[:, 3 * d + DECAY_LORA + ICL_LORA:]

    hsum = hsum_ref[...]
    logw = -DECAY_SCALE * _sigmoid(w0_ref[...] + _mm(jnp.tanh(xw), wup_ref[...]))
    a = _sigmoid(a0_ref[...] + _mm(xa, aup_ref[...]))
    g = _dot(_sigmoid(xg).astype(BF16), gup_ref[...].astype(BF16))
    kap = k * kk_ref[...]
    kap = kap * lax.rsqrt(jnp.maximum(_mm_ones(kap * kap, hsum), 1e-24))
    kn = k * (1.0 + (a - 1.0) * ka_ref[...])
    mb = ltri_ref.shape[0]
    ones2 = jnp.concatenate([ltri_ref[...], same_ref[...]], axis=0)
    sums = [_mm_ones(logw[s0:s0 + mb], ones2, ones_first=True) for s0 in range(0, tb, mb)]
    cum = jnp.concatenate([s[:mb] for s in sums], axis=0)
    tot = jnp.concatenate([s[mb:] for s in sums], axis=0)
    e_neg = jnp.exp(-cum)
    e_rem = jnp.exp(tot - cum)
    nb = -(a * kap)
    kt_s[...] = kap * jnp.exp(cum - logw)
    rt_s[...] = r * jnp.exp(cum)
    bt_s[...] = nb * e_neg
    kn_s[...] = kn * e_neg
    btg_s[...] = nb * e_rem
    kng_s[...] = kn * e_rem
    etot_s[...] = jnp.exp(tot)
    v_s[...] = v

    nh = RWKV_HEADS
    ri = lax.broadcasted_iota(jnp.int32, (c, d), 0)
    ci = lax.broadcasted_iota(jnp.int32, (c, d), 1) % HEAD_DIM
    strict = ci < ri
    incl = ci <= ri
    diag = ci == ri
    eye = diag.astype(F32)
    bi = lax.broadcasted_iota(jnp.int32, (d, d), 0) // HEAD_DIM
    bj = lax.broadcasted_iota(jnp.int32, (d, d), 1) // HEAD_DIM
    blocks = bi == bj
    zero16 = jnp.zeros((), BF16)

    def expand(x16):
        return jnp.where(blocks, jnp.concatenate([x16] * nh, axis=0), zero16)

    def bdmm(a, y, kind="nn", passes=3):
        dg = lambda p_, q_: lax.dot_general(p_, q_, _DIMS[kind], preferred_element_type=F32)
        if passes == 1:
            return dg(a.astype(BF16), expand(y.astype(BF16)))
        ah, al = _split(a, 2)
        yh, yl = _split(y, 2)
        m = a.shape[0]
        both = dg(jnp.concatenate([ah, al], axis=0), expand(yh))
        return both[:m] + (both[m:] + dg(ah, expand(yl)))

    def block_diagonal_of(full):
        outs = []
        for n0 in range(0, full.shape[1], d):
            m = jnp.where(blocks, full[:, n0:n0 + d], 0.0)
            acc = m[0:c]
            for h in range(1, nh):
                acc = acc + m[h * c:(h + 1) * c]
            outs.append(acc)
        return outs

    nchunk = tb // c
    cat0 = lambda x, y: jnp.concatenate([x, y], axis=0)
    cat1 = lambda x, y: jnp.concatenate([x, y], axis=1)
    levels = int(math.log2(c)) - 1

    def independent_part(chunks):
        get = lambda ref: [ref[j * c:(j + 1) * c, :] for j in chunks]
        kt, rt, vv, btg = get(kt_s), get(rt_s), get(v_s), get(btg_s)
        lhs = [cat0(k_, r_) for k_, r_ in zip(kt, rt)]
        a_b = [bdmm(l_, b_, "nt") for l_, b_ in zip(lhs, get(bt_s))]
        yield
        a_k = [bdmm(l_, n_, "nt") for l_, n_ in zip(lhs, get(kn_s))]
        yield
        a_ab = [jnp.where(strict, m[:c], 0.0) for m in a_b]
        a_rb = [jnp.where(incl, m[c:], 0.0) for m in a_b]
        a_kr = [cat0(jnp.where(strict, m[:c], 0.0), jnp.where(incl, m[c:], 0.0)) for m in a_k]
        inv = [eye + m for m in a_ab]
        pw = [bdmm(m, m, passes=1) for m in a_ab]
        yield
        for lvl in range(levels):
            if lvl < levels - 1:
                both = [bdmm(cat0(x_, p_), p_, passes=1) for x_, p_ in zip(inv, pw)]
                inv = [x_ + b_[:c] for x_, b_ in zip(inv, both)]
                pw = [b_[c:] for b_ in both]
            else:
                inv = [x_ + bdmm(x_, p_, passes=1) for x_, p_ in zip(inv, pw)]
            yield
        av = [bdmm(m, v_) for m, v_ in zip(a_kr, vv)]
        yield
        wmat = [bdmm(x_, k_, passes=1) for x_, k_ in zip(inv, kt)]
        umat = [bdmm(x_, a_[:c], passes=1) for x_, a_ in zip(inv, av)]
        yield
        rw = [bdmm(m, w_, passes=1) for m, w_ in zip(a_rb, wmat)]
        ru = [bdmm(m, u_, passes=1) for m, u_ in zip(a_rb, umat)]
        yield
        gz = [block_diagonal_of(_mm(b_, cat1(w_, u_), "tn", passes=1))
              for b_, w_, u_ in zip(btg, wmat, umat)]
        kv = [block_diagonal_of(_mm(n_, v_, "tn"))[0] for n_, v_ in zip(get(kng_s), vv)]
        for i, j in enumerate(chunks):
            rows = slice(j * c, (j + 1) * c)
            rp_s[rows, :] = rt[i] + rw[i]
            y0_s[rows, :] = ru[i] + av[i][c:]
            g_diag = jnp.where(diag, jnp.broadcast_to(etot_s[j * c:j * c + 1, :], (c, d)), 0.0)
            gt_s[rows, :] = g_diag + gz[i][0]
            zt_s[rows, :] = gz[i][1] + kv[i]

    state = [state_s[...]]

    def sequential_part(chunks):
        for j in chunks:
            rows = slice(j * c, (j + 1) * c)
            ry = bdmm(cat0(rp_s[rows, :], gt_s[rows, :]), state[0])
            y_s[rows, :] = ry[:c] + y0_s[rows, :]
            state[0] = ry[c:] + zt_s[rows, :]
            yield

    def output_part(chunks):
        rows = slice(chunks[0] * c, (chunks[-1] + 1) * c)
        y = y_s[rows, :]
        mean = _mm_ones(y, hsum, terms=2) * (1.0 / HEAD_DIM)
        yield
        dy = y - mean
        var = _mm_ones(dy * dy, hsum, terms=2) * (1.0 / HEAD_DIM)
        yield
        yn = dy * lax.rsqrt(var + RWKV_GN_EPS) * lng_ref[...] + lnb_ref[...]
        bonus = _mm_ones(r[rows] * kn[rows] * rk_ref[...], hsum, terms=2) * v[rows]
        yield
        o_ref[rows, :] = ((yn + bonus) * g[rows]).astype(o_ref.dtype)

    groups = [range(g0, min(g0 + WKV_GROUP, nchunk)) for g0 in range(0, nchunk, WKV_GROUP)]
    pending = iter(())
    for chunks in groups:
        for _ in independent_part(chunks):
            next(pending, None)
        for _ in pending:
            pass
        pending = sequential_part(chunks)
    for chunks in groups[:-1]:
        for _ in output_part(chunks):
            next(pending, None)
    for _ in pending:
        pass
    state_s[...] = state[0]
    for _ in output_part(groups[-1]):
        pass


def _chunk_masks(tb, c):
    i = np.arange(tb)
    same = (i[:, None] // c) == (i[None, :] // c)
    ltri = same & (i[None, :] <= i[:, None])
    return jnp.asarray(ltri, BF16), jnp.asarray(same, BF16)


def _head_sum_matrix(width):
    i = np.arange(width)
    return jnp.asarray((i[:, None] // HEAD_DIM) == (i[None, :] // HEAD_DIM), BF16)


def rwkv_time_mix(p, batch, seq, mu, w0, w_up, a0, a_up, g_up, k_k, k_a, r_k, ln_g, ln_b):
    t = batch * seq
    tb = min(WKV_BLOCK, seq)
    nblk = seq // tb
    ltri, same = _chunk_masks(min(WKV_MASK_ROWS, tb), WKV_CHUNK)
    hsum = _head_sum_matrix(RWKV_DIM)
    row = lambda a: a.reshape(1, -1)
    consts = [row(mu), row(w0), w_up, row(a0), a_up, g_up, row(k_k), row(k_a), row(r_k), row(ln_g), row(ln_b),
              ltri, same, hsum]
    buf = lambda: pltpu.VMEM((tb, RWKV_DIM), F32)
    return pl.pallas_call(
        _rwkv_kernel,
        grid=(batch, nblk),
        in_specs=[pl.BlockSpec((tb, RWKV_IN), lambda b, j: (b * nblk + j, 0))]
                 + [_const_spec(a.shape) for a in consts],
        out_specs=pl.BlockSpec((tb, RWKV_DIM), lambda b, j: (b * nblk + j, 0)),
        out_shape=jax.ShapeDtypeStruct((t, RWKV_DIM), BF16),
        scratch_shapes=[pltpu.VMEM((HEAD_DIM, RWKV_DIM), F32), pltpu.VMEM((1, RWKV_IN), F32)]
                       + [buf() for _ in range(13)],
        compiler_params=_params(("parallel", "arbitrary")),
        name="rwkv_time_mix",
    )(p, *consts)


def _attn_kernel(q_ref, k_ref, v_ref, bias_ref, o_ref, acc_s, m_s, l_s):
    seq = q_ref.shape[0]
    w = ATTN_W
    scale = HEAD_DIM ** -0.5

    def rows_of(start, dil):
        return pl.ds(start, w) if dil == 1 else pl.ds(start, w, stride=dil)

    lane = lax.broadcasted_iota(jnp.int32, (w, LANES), 1)
    head0 = lane < HEAD_DIM
    zero = jnp.zeros((), BF16)
    one = jnp.ones((), BF16)

    def group(pi, dil, g, firsts):
        rows_l, q_l, k_l, v_l = [], [], [], []
        for u, first in enumerate(firsts):
            b = g * len(firsts) + u
            start = (b % dil) + (b // dil) * (dil * w)
            rows = rows_of(start, dil)
            q = (q_ref[rows, :] * scale).astype(BF16)
            kk = k_ref[rows, :].astype(BF16)
            vv = v_ref[rows, :].astype(BF16)
            if not first:
                prev = rows_of(start - dil * w, dil)
                kk = jnp.concatenate([k_ref[prev, :].astype(BF16), kk], axis=0)
                vv = jnp.concatenate([v_ref[prev, :].astype(BF16), vv], axis=0)
            rows_l.append(rows)
            q_l.append(q)
            k_l.append(kk)
            v_l.append(vv)
        s = [[_dot_nt(jnp.where(head0 if h == 0 else ~head0, q, zero), kk)
              + (bias_ref[pi, h, :, w:] if first else bias_ref[pi, h])
              for h in range(2)] for q, kk, first in zip(q_l, k_l, firsts)]
        m = [[jnp.max(sh, axis=-1, keepdims=True) for sh in su] for su in s]
        pr = [[jnp.exp(sh - mh).astype(BF16) for sh, mh in zip(su, mu)] for su, mu in zip(s, m)]
        kmask = lambda vv: lax.broadcasted_iota(jnp.int32, vv.shape, 1) < HEAD_DIM
        res = [[_dot(pu[0], jnp.where(kmask(vv), vv, one)), _dot(pu[1], jnp.where(kmask(vv), one, vv))]
               for pu, vv in zip(pr, v_l)]
        for rows, ru, mu in zip(rows_l, res, m):
            acc_s[pi, rows, :] = jnp.where(head0, ru[0], ru[1])
            l_s[pi, rows, :] = jnp.where(head0, ru[1], ru[0])
            m_s[pi, rows, :] = jnp.where(head0, mu[0], mu[1])

    n_groups = (seq // w) // ATTN_UNROLL
    for pi, (window, dil) in enumerate(DILATED_PATTERNS):
        for g in range(n_groups):
            group(pi, dil, g, tuple((g * ATTN_UNROLL + u) < dil for u in range(ATTN_UNROLL)))

    mt = ATTN_MERGE_ROWS

    def merge_body(i, carry):
        rows = pl.ds(i * mt, mt)
        m0, m1, m2 = m_s[0, rows, :], m_s[1, rows, :], m_s[2, rows, :]
        mx = jnp.maximum(jnp.maximum(m0, m1), m2)
        w0, w1, w2 = jnp.exp(m0 - mx), jnp.exp(m1 - mx), jnp.exp(m2 - mx)
        num = w0 * acc_s[0, rows, :] + w1 * acc_s[1, rows, :] + w2 * acc_s[2, rows, :]
        swap = lambda x: pltpu.roll(x, HEAD_DIM, axis=1)
        den = w0 * swap(l_s[0, rows, :]) + w1 * swap(l_s[1, rows, :]) + w2 * swap(l_s[2, rows, :])
        o_ref[rows, :] = (num / den).astype(o_ref.dtype)
        return carry

    for i in range(seq // mt):
        merge_body(i, 0)


def _t5_bucket(dist):
    max_exact = NUM_BUCKETS // 2
    large = max_exact + (np.log(np.maximum(dist, max_exact) / max_exact)
                         / math.log(MAX_DISTANCE / max_exact) * (NUM_BUCKETS - max_exact)).astype(np.int32)
    return np.where(dist < max_exact, dist, np.minimum(large, NUM_BUCKETS - 1)).astype(np.int32)


def _attn_bias(rel_bias):
    w = ATTN_W
    i = np.arange(w)[:, None]
    j = np.arange(2 * w)[None, :]
    rel = i + w - j
    band = (rel >= 0) & (rel <= w)
    tabs = []
    for window, dil in DILATED_PATTERNS:
        bucket = _t5_bucket(np.clip(rel, 0, None) * dil)
        onehot = jnp.asarray(bucket[..., None] == np.arange(NUM_BUCKETS), F32)
        bias = jnp.einsum('ijb,bh->hij', onehot, rel_bias.astype(F32), precision=HI)
        tabs.append(jnp.where(band[None], bias, MASK_VALUE))
    return jnp.stack(tabs)


def dilated_attention(p, batch, seq, rel_bias):
    t = batch * seq
    bias = _attn_bias(rel_bias)
    col0 = RWKV_IN // LANES
    npair = ATTN_DIM // LANES
    spec = lambda off: pl.BlockSpec((seq, LANES), lambda b, hp: (b, col0 + off + hp))
    return pl.pallas_call(
        _attn_kernel,
        grid=(batch, npair),
        in_specs=[spec(0), spec(npair), spec(2 * npair),
                  pl.BlockSpec((3, 2, ATTN_W, 2 * ATTN_W), lambda b, hp: (0, hp, 0, 0))],
        out_specs=pl.BlockSpec((seq, LANES), lambda b, hp: (b, hp)),
        out_shape=jax.ShapeDtypeStruct((t, ATTN_DIM), BF16),
        scratch_shapes=[pltpu.VMEM((3, seq, LANES), F32) for _ in range(3)],
        compiler_params=_params(("parallel", "parallel")),
        name="dilated_attention",
    )(p, p, p, bias)


def _ret_kernel(q_ref, k_ref, v_ref, g_ref, cos_ref, sin_ref, dmat_ref, xi_ref, zeta_ref, gng_ref, gnb_ref,
                hsum_ref, o_ref, state_s):
    c = RET_CHUNK
    tb = q_ref.shape[0]

    @pl.when(pl.program_id(1) == 0)
    def _():
        state_s[...] = jnp.zeros_like(state_s)

    lane = lax.broadcasted_iota(jnp.int32, (c, RET_DIM), 1)
    first_half = (lane % HEAD_DIM) < (HEAD_DIM // 2)

    def rotate(x, cos, sin):
        swapped = jnp.where(first_half, pltpu.roll(x, RET_DIM - HEAD_DIM // 2, axis=1),
                            pltpu.roll(x, HEAD_DIM // 2, axis=1))
        return x * cos + swapped * sin

    heads = range(RET_HEADS)
    hsl = [slice(h * HEAD_DIM, (h + 1) * HEAD_DIM) for h in heads]
    chunk_decay = [(1.0 - 2.0 ** (-5.0 - h)) ** c for h in heads]

    def group_body(g, carry):
        rows_l, qb, kb, qx, kz, vb = [], [], [], [], [], []
        for u in range(RET_UNROLL):
            rows = pl.ds(pl.multiple_of((g * RET_UNROLL + u) * c, c), c)
            cos, sin = cos_ref[rows, :], sin_ref[rows, :]
            q = rotate(q_ref[rows, :], cos, sin)
            k = rotate(k_ref[rows, :], cos, sin) * (HEAD_DIM ** -0.5)
            rows_l.append(rows)
            qb.append(q.astype(BF16))
            kb.append(k.astype(BF16))
            qx.append((q * xi_ref[...]).astype(BF16))
            kz.append((k * zeta_ref[...]).astype(BF16))
            vb.append(v_ref[rows, :].astype(BF16))
        sc = [[(_dot_nt(qb[u][:, sl], kb[u][:, sl]) * dmat_ref[h]).astype(BF16) for h, sl in zip(heads, hsl)]
              for u in range(RET_UNROLL)]
        intra = [[_dot(sc[u][h], vb[u][:, hsl[h]]) for h in heads] for u in range(RET_UNROLL)]
        kv = [[_dot_tn(kz[u][:, sl], vb[u][:, sl]) for sl in hsl] for u in range(RET_UNROLL)]
        states = [state_s[h] for h in heads]
        ys = []
        for u in range(RET_UNROLL):
            ys.append(jnp.concatenate(
                [intra[u][h] + _dot(qx[u][:, hsl[h]], states[h].astype(BF16)) for h in heads], axis=1))
            states = [states[h] * chunk_decay[h] + kv[u][h] for h in heads]
        for h in heads:
            state_s[h] = states[h]
        hsum = hsum_ref[...]
        for rows, y in zip(rows_l, ys):
            mean = _mm_ones(y, hsum, terms=2) * (1.0 / HEAD_DIM)
            dy = y - mean
            var = _mm_ones(dy * dy, hsum, terms=2) * (1.0 / HEAD_DIM)
            yn = dy * lax.rsqrt(var + LN_EPS) * gng_ref[...] + gnb_ref[...]
            gate = g_ref[rows, :]
            o_ref[rows, :] = (gate * _sigmoid(gate) * yn).astype(o_ref.dtype)
        return carry

    lax.fori_loop(0, tb // (c * RET_UNROLL), group_body, 0)


def _ret_tables(seq):
    c = RET_CHUNK
    half = HEAD_DIM // 2
    inv = ROPE_BASE ** (-jnp.arange(half, dtype=F32) / half)
    ang = jnp.arange(seq, dtype=F32)[:, None] * inv
    cos, sin = jnp.cos(ang), jnp.sin(ang)
    cos_t = jnp.tile(jnp.concatenate([cos, cos], axis=1), (1, RET_HEADS))
    sin_t = jnp.tile(jnp.concatenate([-sin, sin], axis=1), (1, RET_HEADS))
    log_g = jnp.log1p(-jnp.exp2(-5.0 - jnp.arange(RET_HEADS, dtype=F32)))
    n = jnp.arange(c, dtype=F32)
    diff = n[:, None] - n[None, :]
    dmat = jnp.where(diff >= 0, jnp.exp(log_g[:, None, None] * jnp.maximum(diff, 0.0)), 0.0)
    zeta = jnp.exp(log_g[:, None] * (c - 1 - n))
    xi = jnp.exp(log_g[:, None] * (n + 1))
    widen = lambda tab: jnp.repeat(tab.T, HEAD_DIM, axis=1)
    return cos_t, sin_t, dmat, widen(xi), widen(zeta)


def retention(p, batch, seq, gn_g, gn_b):
    t = batch * seq
    tb = min(RET_BLOCK, seq)
    nblk = seq // tb
    cos_t, sin_t, dmat, xi, zeta = _ret_tables(seq)
    hsum = _head_sum_matrix(RET_DIM)
    col0 = (RWKV_IN + ATTN_IN) // RET_DIM
    spec = lambda off: pl.BlockSpec((tb, RET_DIM), lambda b, j: (b * nblk + j, col0 + off))
    tab = pl.BlockSpec((tb, RET_DIM), lambda b, j: (j, 0))
    consts = [dmat, xi, zeta, gn_g.reshape(1, -1), gn_b.reshape(1, -1), hsum]
    return pl.pallas_call(
        _ret_kernel,
        grid=(batch, nblk),
        in_specs=[spec(0), spec(1), spec(2), spec(3), tab, tab] + [_const_spec(a.shape) for a in consts],
        out_specs=pl.BlockSpec((tb, RET_DIM), lambda b, j: (b * nblk + j, 0)),
        out_shape=jax.ShapeDtypeStruct((t, RET_DIM), BF16),
        scratch_shapes=[pltpu.VMEM((RET_HEADS, HEAD_DIM, HEAD_DIM), F32)],
        compiler_params=_params(("parallel", "arbitrary")),
        name="retention",
    )(p, p, p, p, cos_t, sin_t, *consts)


def _outproj_kernel(ya_ref, yb_ref, yc_ref, x_ref, w_ref, g_ref, b_ref, *rest, alpha):
    acc = _dot(ya_ref[...], w_ref[0:RWKV_DIM, :])
    acc += _dot(yb_ref[...], w_ref[RWKV_DIM:RWKV_DIM + ATTN_DIM, :])
    acc += _dot(yc_ref[...], w_ref[RWKV_DIM + ATTN_DIM:, :])
    h = _layer_norm(alpha * x_ref[...] + acc, g_ref[...], b_ref[...])
    if len(rest) == 1:
        (o_ref,) = rest
    else:
        router_ref, o_ref, logit_ref = rest
        logit_ref[...] = _dot_nt(router_ref[...].astype(BF16), h.astype(BF16))
    o_ref[...] = h


def out_projection_ln(ya, yb, yc, x2d, w_bf16, layer, g, b, alpha, router=None):
    t, d = x2d.shape
    tm = min(ROW_TILE, t)
    rows = lambda width: pl.BlockSpec((tm, width), lambda i: (i, 0))
    in_specs = [rows(RWKV_DIM), rows(ATTN_DIM), rows(RET_DIM), rows(d), _layer_spec(w_bf16, layer),
                _const_spec((1, d)), _const_spec((1, d))]
    args = [ya, yb, yc, x2d, w_bf16, g.reshape(1, -1), b.reshape(1, -1)]
    out_specs, out_shape = rows(d), jax.ShapeDtypeStruct((t, d), F32)
    if router is not None:
        in_specs.append(_const_spec((N_EXPERTS, d)))
        args.append(router.T)
        out_specs = [out_specs, pl.BlockSpec((N_EXPERTS, tm), lambda i: (0, i))]
        out_shape = [out_shape, jax.ShapeDtypeStruct((N_EXPERTS, t), F32)]
    return pl.pallas_call(
        functools.partial(_outproj_kernel, alpha=alpha),
        grid=(t // tm,),
        in_specs=in_specs,
        out_specs=out_specs,
        out_shape=out_shape,
        compiler_params=_params(("parallel",)),
        name="out_projection_ln",
    )(*args)


def _swiglu(xb, wg, wu, wd, acc, f_chunk):
    ff = wg.shape[-1]
    starts = list(range(0, ff, f_chunk))
    gate_up = lambda f0: (_dot(xb, wg[:, f0:f0 + f_chunk]), _dot(xb, wu[:, f0:f0 + f_chunk]))
    nxt = gate_up(starts[0])
    for n, f0 in enumerate(starts):
        gate, up = nxt
        if n + 1 < len(starts):
            nxt = gate_up(starts[n + 1])
        hid = (gate * _sigmoid(gate) * up).astype(BF16)
        part = _dot(hid, wd[f0:f0 + f_chunk, :])
        acc = part if acc is None else acc + part
    return acc


def _ffn_kernel(x_ref, wg_ref, wu_ref, wd_ref, g_ref, b_ref, o_ref, *, alpha, f_chunk):
    x = x_ref[...]
    acc = _swiglu(x.astype(BF16), wg_ref, wu_ref, wd_ref, alpha * x, f_chunk)
    o_ref[...] = _layer_norm(acc, g_ref[...], b_ref[...])


def dense_ffn_ln(x2d, wg, wu, wd, layer, g, b, alpha):
    t, d = x2d.shape
    tm = min(ROW_TILE, t)
    return pl.pallas_call(
        functools.partial(_ffn_kernel, alpha=alpha, f_chunk=FFN_SUBCHUNK),
        grid=(t // tm,),
        in_specs=[pl.BlockSpec((tm, d), lambda i: (i, 0)), _layer_spec(wg, layer), _layer_spec(wu, layer),
                  _layer_spec(wd, layer), _const_spec((1, d)), _const_spec((1, d))],
        out_specs=pl.BlockSpec((tm, d), lambda i: (i, 0)),
        out_shape=jax.ShapeDtypeStruct((t, d), F32),
        compiler_params=_params(("parallel",)),
        name="dense_ffn_ln",
    )(x2d, wg, wu, wd, g.reshape(1, -1), b.reshape(1, -1))


def _expert_kernel(blk_e_ref, used_ref, x_ref, wg_ref, wu_ref, wd_ref, o_ref, acc_s):
    i, j = pl.program_id(0), pl.program_id(1)
    last = pl.num_programs(1) - 1

    @pl.when(i < used_ref[0])
    def _():
        part = _swiglu(x_ref[...].astype(BF16), wg_ref.at[0], wu_ref.at[0], wd_ref.at[0], None, FFN_SUBCHUNK)

        @pl.when(j == 0)
        def _():
            acc_s[...] = part

        @pl.when((j > 0) & (j < last))
        def _():
            acc_s[...] += part

        @pl.when(j == last)
        def _():
            o_ref[...] = (acc_s[...] + part).astype(o_ref.dtype)


def expert_ffn(xs, blk_e, n_used, wg, wu, wd, layer, f_chunk=MOE_F_BLOCK):
    rows, d = xs.shape
    ff = wg.shape[3]
    tm = MOE_TILE
    assert ff // f_chunk >= 2
    grid_spec = pltpu.PrefetchScalarGridSpec(
        num_scalar_prefetch=2,
        grid=(rows // tm, ff // f_chunk),
        in_specs=[pl.BlockSpec((tm, d), lambda i, j, be, nu: (i, 0)),
                  pl.BlockSpec((None, 1, d, f_chunk), lambda i, j, be, nu: (layer, be[i], 0, j)),
                  pl.BlockSpec((None, 1, d, f_chunk), lambda i, j, be, nu: (layer, be[i], 0, j)),
                  pl.BlockSpec((None, 1, f_chunk, d), lambda i, j, be, nu: (layer, be[i], j, 0))],
        out_specs=pl.BlockSpec((tm, d), lambda i, j, be, nu: (i, 0)),
        scratch_shapes=[pltpu.VMEM((tm, d), F32)],
    )
    return pl.pallas_call(
        _expert_kernel,
        grid_spec=grid_spec,
        out_shape=jax.ShapeDtypeStruct((rows, d), BF16),
        compiler_params=_params(("parallel", "arbitrary")),
        name="expert_ffn",
    )(blk_e, n_used, xs, wg, wu, wd)


def _combine_ln_kernel(x_ref, y0_ref, y1_ref, gate_ref, g_ref, b_ref, o_ref, *, alpha):
    gates = gate_ref[...]
    f = y0_ref[...] * gates[:, 0:1] + y1_ref[...] * gates[:, 1:2]
    o_ref[...] = _layer_norm(alpha * x_ref[...] + f, g_ref[...], b_ref[...])


def combine_ln(x2d, y0, y1, gates, g, b, alpha):
    t, d = x2d.shape
    tm = min(ROW_TILE, t)
    rows = pl.BlockSpec((tm, d), lambda i: (i, 0))
    return pl.pallas_call(
        functools.partial(_combine_ln_kernel, alpha=alpha),
        grid=(t // tm,),
        in_specs=[rows, rows, rows, pl.BlockSpec((tm, TOP_K), lambda i: (i, 0)),
                  _const_spec((1, d)), _const_spec((1, d))],
        out_specs=rows,
        out_shape=jax.ShapeDtypeStruct((t, d), F32),
        compiler_params=_params(("parallel",)),
        name="combine_ln",
    )(x2d, y0, y1, gates, g.reshape(1, -1), b.reshape(1, -1))


def moe_ffn_ln(x2d, logits_t, wg, wu, wd, layer, g, b, alpha):
    t, d = x2d.shape
    tm = MOE_TILE
    logits = logits_t.T
    top_val, top_idx = lax.top_k(logits, TOP_K)
    gates = jax.nn.softmax(top_val, axis=-1)
    member = (top_idx[:, :, None] == jnp.arange(N_EXPERTS)[None, None, :]).any(axis=1)
    counts = member.sum(axis=0).astype(jnp.int32)
    rank = jnp.cumsum(member.astype(jnp.int32), axis=0) - member.astype(jnp.int32)
    padded = (counts + tm - 1) // tm * tm
    pad_ends = jnp.cumsum(padded)
    pad_starts = pad_ends - padded
    pos = jnp.take_along_axis(pad_starts[None, :] + rank, top_idx, axis=1)
    rows = t * TOP_K + N_EXPERTS * tm
    order = jnp.argsort(top_idx.reshape(-1), stable=True).astype(jnp.int32)
    tok_sorted = order // TOP_K
    starts = jnp.cumsum(counts) - counts
    r = jnp.arange(rows, dtype=jnp.int32)
    row_e = jnp.minimum(jnp.searchsorted(pad_ends, r, side='right'), N_EXPERTS - 1).astype(jnp.int32)
    within = r - pad_starts[row_e]
    src = jnp.where(within < counts[row_e], tok_sorted[jnp.minimum(starts[row_e] + within, t * TOP_K - 1)], 0)
    blk_e = row_e[::tm]
    n_used = (pad_ends[-1:] // tm).astype(jnp.int32)
    ys = expert_ffn(x2d[src], blk_e, n_used, wg, wu, wd, layer)
    return combine_ln(x2d, ys[pos[:, 0]], ys[pos[:, 1]], gates, g, b, alpha)


def kernel(x, w_in, w_out, rwkv_mu, rwkv_w0, rwkv_w_up, rwkv_a0, rwkv_a_up, rwkv_g_up, rwkv_k_k, rwkv_k_a,
           rwkv_r_k, rwkv_ln_g, rwkv_ln_b, ret_gn_g, ret_gn_b, rel_bias, ln_g, ln_b, ffn_w_gate, ffn_w_up,
           ffn_w_down, moe_router, moe_w_gate, moe_w_up, moe_w_down):
    batch, seq, d = x.shape
    depth = w_in.shape[0]
    alpha = (2 * depth) ** 0.25
    h = x.reshape(batch * seq, d)
    w_in, w_out = w_in.astype(BF16), w_out.astype(BF16)
    ffn_w_gate, ffn_w_up, ffn_w_down = ffn_w_gate.astype(BF16), ffn_w_up.astype(BF16), ffn_w_down.astype(BF16)
    for layer in range(depth):
        p = in_projection(h, w_in, layer)
        ya = rwkv_time_mix(p, batch, seq, rwkv_mu[layer], rwkv_w0[layer], rwkv_w_up[layer], rwkv_a0[layer],
                           rwkv_a_up[layer], rwkv_g_up[layer], rwkv_k_k[layer], rwkv_k_a[layer],
                           rwkv_r_k[layer], rwkv_ln_g[layer], rwkv_ln_b[layer])
        yb = dilated_attention(p, batch, seq, rel_bias)
        yc = retention(p, batch, seq, ret_gn_g[layer], ret_gn_b[layer])
        j = layer // 2
        if layer % 2 == 0:
            h = out_projection_ln(ya, yb, yc, h, w_out, layer, ln_g[layer, 0], ln_b[layer, 0], alpha)
            h = dense_ffn_ln(h, ffn_w_gate, ffn_w_up, ffn_w_down, j, ln_g[layer, 1], ln_b[layer, 1], alpha)
        else:
            h, logits_t = out_projection_ln(ya, yb, yc, h, w_out, layer, ln_g[layer, 0], ln_b[layer, 0], alpha,
                                            router=moe_router[j])
            h = moe_ffn_ln(h, logits_t, layer_to_bf16(moe_w_gate, j), layer_to_bf16(moe_w_up, j),
                           layer_to_bf16(moe_w_down, j), 0, ln_g[layer, 1], ln_b[layer, 1], alpha)
    return h.reshape(batch, seq, d)
```

```python
import functools
import math

import numpy as np
import jax
import jax.numpy as jnp
from jax import lax
from jax.experimental import pallas as pl
from jax.experimental.pallas import tpu as pltpu

F32 = jnp.float32
BF16 = jnp.bfloat16
HI = lax.Precision.HIGHEST

HEAD_DIM = 64
RWKV_HEADS = 4
ATTN_HEADS = 8
RET_HEADS = 4
RWKV_DIM = RWKV_HEADS * HEAD_DIM
ATTN_DIM = ATTN_HEADS * HEAD_DIM
RET_DIM = RET_HEADS * HEAD_DIM
DECAY_LORA = 64
ICL_LORA = 64
GATE_LORA = 128
RWKV_IN = 3 * RWKV_DIM + DECAY_LORA + ICL_LORA + GATE_LORA
ATTN_IN = 3 * ATTN_DIM
RET_IN = 4 * RET_DIM
RWKV_GN_EPS = 64e-5
DECAY_SCALE = math.exp(-0.5)
DILATED_PATTERNS = ((128, 1), (512, 4), (2048, 16))
NUM_BUCKETS = 32
MAX_DISTANCE = 2048
ROPE_BASE = 10000.0
N_EXPERTS = 8
TOP_K = 2
LN_EPS = 1e-5

LANES = 128
WKV_CHUNK = 64
WKV_BLOCK = 1024
WKV_MASK_ROWS = 256
WKV_GROUP = 4
ATTN_W = 128
ATTN_UNROLL = 4
ATTN_MERGE_ROWS = 256
INPROJ_N_CHUNK = 512
RET_CHUNK = 128
RET_BLOCK = 1024
RET_UNROLL = 4
ROW_TILE = 512
MOE_TILE = 512
MOE_F_BLOCK = 1792
FFN_SUBCHUNK = 256
MASK_VALUE = -1e30
VMEM_LIMIT = 56 * 1024 * 1024


_DIMS = {"nn": (((1,), (0,)), ((), ())), "nt": (((1,), (1,)), ((), ())), "tn": (((0,), (0,)), ((), ()))}


def _dot(a, b):
    return jnp.dot(a, b, preferred_element_type=F32)


def _dot_nt(a, b):
    return lax.dot_general(a, b, _DIMS["nt"], preferred_element_type=F32)


def _dot_tn(a, b):
    return lax.dot_general(a, b, _DIMS["tn"], preferred_element_type=F32)


def _split(x, terms):
    parts = []
    for _ in range(terms - 1):
        hi = x.astype(BF16)
        parts.append(hi)
        x = x - hi.astype(F32)
    parts.append(x.astype(BF16))
    return parts


def _mm(a, b, kind="nn", passes=3):
    dg = lambda p, q: lax.dot_general(p, q, _DIMS[kind], preferred_element_type=F32)
    if passes == 1:
        return dg(a.astype(BF16), b.astype(BF16))
    ah, al = _split(a, 2)
    bh, bl = _split(b, 2)
    return dg(ah, bh) + (dg(al, bh) + dg(ah, bl))


def _mm_ones(x, ones_bf16, ones_first=False, terms=3):
    parts = _split(x, terms)
    m, n = x.shape
    if ones_first:
        full = _dot(ones_bf16, jnp.concatenate(parts, axis=1))
        out = [full[:, i * n:(i + 1) * n] for i in range(terms)]
    else:
        full = _dot(jnp.concatenate(parts, axis=0), ones_bf16)
        out = [full[i * m:(i + 1) * m] for i in range(terms)]
    acc = out[-1]
    for o in reversed(out[:-1]):
        acc = acc + o
    return acc


def _sigmoid(x):
    return 1.0 / (1.0 + jnp.exp(-x))


def _layer_norm(z, g, b):
    mu = jnp.mean(z, axis=-1, keepdims=True)
    d = z - mu
    var = jnp.mean(d * d, axis=-1, keepdims=True)
    return d * lax.rsqrt(var + LN_EPS) * g + b


def _params(sem, vmem=VMEM_LIMIT):
    return pltpu.CompilerParams(dimension_semantics=sem, vmem_limit_bytes=vmem)


def _const_spec(shape):
    nd = len(shape)
    return pl.BlockSpec(shape, lambda *_: (0,) * nd)


def _layer_spec(stacked, layer):
    nd = stacked.ndim - 1
    return pl.BlockSpec((None,) + stacked.shape[1:], lambda *_: (layer,) + (0,) * nd)


CAST_BLOCK_BYTES = 8 * 1024 * 1024


def _cast_kernel(x_ref, o_ref):
    o_ref[...] = x_ref[...].astype(o_ref.dtype)


def layer_to_bf16(w, layer):
    shape = w.shape
    cols = shape[-1]
    rows = w[0].size // cols
    w3 = w.reshape(shape[0], rows, cols)
    tr = 1 << int(math.log2(max(16, min(rows, CAST_BLOCK_BYTES // (4 * cols)))))
    while rows % tr:
        tr //= 2
    assert tr % 16 == 0
    out = pl.pallas_call(
        _cast_kernel,
        grid=(rows // tr,),
        in_specs=[pl.BlockSpec((None, tr, cols), lambda i: (layer, i, 0))],
        out_specs=pl.BlockSpec((tr, cols), lambda i: (i, 0)),
        out_shape=jax.ShapeDtypeStruct((rows, cols), BF16),
        compiler_params=_params(("parallel",)),
        name="layer_to_bf16",
    )(w3)
    return out.reshape((1,) + shape[1:])


def _inproj_kernel(x_ref, w_ref, o_ref, *, n_chunk):
    xb = x_ref[...].astype(BF16)
    for n0 in range(0, o_ref.shape[1], n_chunk):
        o_ref[:, n0:n0 + n_chunk] = _dot(xb, w_ref[:, n0:n0 + n_chunk])


def in_projection(x2d, w_bf16, layer):
    t, d = x2d.shape
    n = w_bf16.shape[2]
    tm = min(ROW_TILE, t)
    return pl.pallas_call(
        functools.partial(_inproj_kernel, n_chunk=INPROJ_N_CHUNK),
        grid=(t // tm,),
        in_specs=[pl.BlockSpec((tm, d), lambda i: (i, 0)), _layer_spec(w_bf16, layer)],
        out_specs=pl.BlockSpec((tm, n), lambda i: (i, 0)),
        out_shape=jax.ShapeDtypeStruct((t, n), F32),
        compiler_params=_params(("parallel",)),
        name="in_projection",
    )(x2d, w_bf16)


def _rwkv_kernel(p_ref, mu_ref, w0_ref, wup_ref, a0_ref, aup_ref, gup_ref, kk_ref, ka_ref, rk_ref,
                 lng_ref, lnb_ref, ltri_ref, same_ref, hsum_ref, o_ref,
                 state_s, prev_s, kt_s, rt_s, bt_s, kn_s, v_s, btg_s, kng_s, etot_s, y_s, rp_s, y0_s, gt_s, zt_s):
    c = WKV_CHUNK
    tb = p_ref.shape[0]
    d = RWKV_DIM
    assert c == HEAD_DIM

    @pl.when(pl.program_id(1) == 0)
    def _():
        state_s[...] = jnp.zeros_like(state_s)
        prev_s[...] = jnp.zeros_like(prev_s)

    p = p_ref[...]
    row = lax.broadcasted_iota(jnp.int32, p.shape, 0)
    shifted = jnp.where(row == 0, prev_s[...], pltpu.roll(p, 1, axis=0))
    prev_s[...] = p[tb - 1:tb, :]
    ps = p + (shifted - p) * mu_ref[...]
    r = ps[:, 0:d]
    k = ps[:, d:2 * d]
    v = ps[:, 2 * d:3 * d]
    xw = ps[:, 3 * d:3 * d + DECAY_LORA]
    xa = ps[:, 3 * d + DECAY_LORA:3 * d + DECAY_LORA + ICL_LORA]
    xg = ps[:, 3 * d + DECAY_LORA + ICL_LORA:]

    hsum = hsum_ref[...]
    logw = -DECAY_SCALE * _sigmoid(w0_ref[...] + _mm(jnp.tanh(xw), wup_ref[...]))
    a = _sigmoid(a0_ref[...] + _mm(xa, aup_ref[...]))
    g = _dot(_sigmoid(xg).astype(BF16), gup_ref[...].astype(BF16))
    kap = k * kk_ref[...]
    kap = kap * lax.rsqrt(jnp.maximum(_mm_ones(kap * kap, hsum), 1e-24))
    kn = k * (1.0 + (a - 1.0) * ka_ref[...])
    mb = ltri_ref.shape[0]
    ones2 = jnp.concatenate([ltri_ref[...], same_ref[...]], axis=0)
    sums = [_mm_ones(logw[s0:s0 + mb], ones2, ones_first=True) for s0 in range(0, tb, mb)]
    cum = jnp.concatenate([s[:mb] for s in sums], axis=0)
    tot = jnp.concatenate([s[mb:] for s in sums], axis=0)
    e_neg = jnp.exp(-cum)
    e_rem = jnp.exp(tot - cum)
    nb = -(a * kap)
    kt_s[...] = kap * jnp.exp(cum - logw)
    rt_s[...] = r * jnp.exp(cum)
    bt_s[...] = nb * e_neg
    kn_s[...] = kn * e_neg
    btg_s[...] = nb * e_rem
    kng_s[...] = kn * e_rem
    etot_s[...] = jnp.exp(tot)
    v_s[...] = v

    nh = RWKV_HEADS
    ri = lax.broadcasted_iota(jnp.int32, (c, d), 0)
    ci = lax.broadcasted_iota(jnp.int32, (c, d), 1) % HEAD_DIM
    strict = ci < ri
    incl = ci <= ri
    diag = ci == ri
    eye = diag.astype(F32)
    bi = lax.broadcasted_iota(jnp.int32, (d, d), 0) // HEAD_DIM
    bj = lax.broadcasted_iota(jnp.int32, (d, d), 1) // HEAD_DIM
    blocks = bi == bj
    zero16 = jnp.zeros((), BF16)

    def expand(x16):
        return jnp.where(blocks, jnp.concatenate([x16] * nh, axis=0), zero16)

    def bdmm(a, y, kind="nn", passes=3):
        dg = lambda p_, q_: lax.dot_general(p_, q_, _DIMS[kind], preferred_element_type=F32)
        if passes == 1:
            return dg(a.astype(BF16), expand(y.astype(BF16)))
        ah, al = _split(a, 2)
        yh, yl = _split(y, 2)
        m = a.shape[0]
        both = dg(jnp.concatenate([ah, al], axis=0), expand(yh))
        return both[:m] + (both[m:] + dg(ah, expand(yl)))

    def block_diagonal_of(full):
        outs = []
        for n0 in range(0, full.shape[1], d):
            m = jnp.where(blocks, full[:, n0:n0 + d], 0.0)
            acc = m[0:c]
            for h in range(1, nh):
                acc = acc + m[h * c:(h + 1) * c]
            outs.append(acc)
        return outs

    nchunk = tb // c
    cat0 = lambda x, y: jnp.concatenate([x, y], axis=0)
    cat1 = lambda x, y: jnp.concatenate([x, y], axis=1)
    levels = int(math.log2(c)) - 1

    def independent_part(chunks):
        get = lambda ref: [ref[j * c:(j + 1) * c, :] for j in chunks]
        kt, rt, vv, btg = get(kt_s), get(rt_s), get(v_s), get(btg_s)
        lhs = [cat0(k_, r_) for k_, r_ in zip(kt, rt)]
        a_b = [bdmm(l_, b_, "nt") for l_, b_ in zip(lhs, get(bt_s))]
        yield
        a_k = [bdmm(l_, n_, "nt") for l_, n_ in zip(lhs, get(kn_s))]
        yield
        a_ab = [jnp.where(strict, m[:c], 0.0) for m in a_b]
        a_rb = [jnp.where(incl, m[c:], 0.0) for m in a_b]
        a_kr = [cat0(jnp.where(strict, m[:c], 0.0), jnp.where(incl, m[c:], 0.0)) for m in a_k]
        inv = [eye + m for m in a_ab]
        pw = [bdmm(m, m, passes=1) for m in a_ab]
        yield
        for lvl in range(levels):
            if lvl < levels - 1:
                both = [bdmm(cat0(x_, p_), p_, passes=1) for x_, p_ in zip(inv, pw)]
                inv = [x_ + b_[:c] for x_, b_ in zip(inv, both)]
                pw = [b_[c:] for b_ in both]
            else:
                inv = [x_ + bdmm(x_, p_, passes=1) for x_, p_ in zip(inv, pw)]
            yield
        av = [bdmm(m, v_) for m, v_ in zip(a_kr, vv)]
        yield
        wmat = [bdmm(x_, k_, passes=1) for x_, k_ in zip(inv, kt)]
        umat = [bdmm(x_, a_[:c], passes=1) for x_, a_ in zip(inv, av)]
        yield
        rw = [bdmm(m, w_, passes=1) for m, w_ in zip(a_rb, wmat)]
        ru = [bdmm(m, u_, passes=1) for m, u_ in zip(a_rb, umat)]
        yield
        gz = [block_diagonal_of(_mm(b_, cat1(w_, u_), "tn", passes=1))
              for b_, w_, u_ in zip(btg, wmat, umat)]
        kv = [block_diagonal_of(_mm(n_, v_, "tn"))[0] for n_, v_ in zip(get(kng_s), vv)]
        for i, j in enumerate(chunks):
            rows = slice(j * c, (j + 1) * c)
            rp_s[rows, :] = rt[i] + rw[i]
            y0_s[rows, :] = ru[i] + av[i][c:]
            g_diag = jnp.where(diag, jnp.broadcast_to(etot_s[j * c:j * c + 1, :], (c, d)), 0.0)
            gt_s[rows, :] = g_diag + gz[i][0]
            zt_s[rows, :] = gz[i][1] + kv[i]

    state = [state_s[...]]

    def sequential_part(chunks):
        for j in chunks:
            rows = slice(j * c, (j + 1) * c)
            ry = bdmm(cat0(rp_s[rows, :], gt_s[rows, :]), state[0])
            y_s[rows, :] = ry[:c] + y0_s[rows, :]
            state[0] = ry[c:] + zt_s[rows, :]
            yield

    def output_part(chunks):
        rows = slice(chunks[0] * c, (chunks[-1] + 1) * c)
        y = y_s[rows, :]
        mean = _mm_ones(y, hsum, terms=2) * (1.0 / HEAD_DIM)
        yield
        dy = y - mean
        var = _mm_ones(dy * dy, hsum, terms=2) * (1.0 / HEAD_DIM)
        yield
        yn = dy * lax.rsqrt(var + RWKV_GN_EPS) * lng_ref[...] + lnb_ref[...]
        bonus = _mm_ones(r[rows] * kn[rows] * rk_ref[...], hsum, terms=2) * v[rows]
        yield
        o_ref[rows, :] = ((yn + bonus) * g[rows]).astype(o_ref.dtype)

    groups = [range(g0, min(g0 + WKV_GROUP, nchunk)) for g0 in range(0, nchunk, WKV_GROUP)]
    pending = iter(())
    for chunks in groups:
        for _ in independent_part(chunks):
            next(pending, None)
        for _ in pending:
            pass
        pending = sequential_part(chunks)
    for chunks in groups[:-1]:
        for _ in output_part(chunks):
            next(pending, None)
    for _ in pending:
        pass
    state_s[...] = state[0]
    for _ in output_part(groups[-1]):
        pass


def _chunk_masks(tb, c):
    i = np.arange(tb)
    same = (i[:, None] // c) == (i[None, :] // c)
    ltri = same & (i[None, :] <= i[:, None])
    return jnp.asarray(ltri, BF16), jnp.asarray(same, BF16)


def _head_sum_matrix(width):
    i = np.arange(width)
    return jnp.asarray((i[:, None] // HEAD_DIM) == (i[None, :] // HEAD_DIM), BF16)


def rwkv_time_mix(p, batch, seq, mu, w0, w_up, a0, a_up, g_up, k_k, k_a, r_k, ln_g, ln_b):
    t = batch * seq
    tb = min(WKV_BLOCK, seq)
    nblk = seq // tb
    ltri, same = _chunk_masks(min(WKV_MASK_ROWS, tb), WKV_CHUNK)
    hsum = _head_sum_matrix(RWKV_DIM)
    row = lambda a: a.reshape(1, -1)
    consts = [row(mu), row(w0), w_up, row(a0), a_up, g_up, row(k_k), row(k_a), row(r_k), row(ln_g), row(ln_b),
              ltri, same, hsum]
    buf = lambda: pltpu.VMEM((tb, RWKV_DIM), F32)
    return pl.pallas_call(
        _rwkv_kernel,
        grid=(batch, nblk),
        in_specs=[pl.BlockSpec((tb, RWKV_IN), lambda b, j: (b * nblk + j, 0))]
                 + [_const_spec(a.shape) for a in consts],
        out_specs=pl.BlockSpec((tb, RWKV_DIM), lambda b, j: (b * nblk + j, 0)),
        out_shape=jax.ShapeDtypeStruct((t, RWKV_DIM), BF16),
        scratch_shapes=[pltpu.VMEM((HEAD_DIM, RWKV_DIM), F32), pltpu.VMEM((1, RWKV_IN), F32)]
                       + [buf() for _ in range(13)],
        compiler_params=_params(("parallel", "arbitrary")),
        name="rwkv_time_mix",
    )(p, *consts)


def _attn_kernel(q_ref, k_ref, v_ref, bias_ref, o_ref, acc_s, m_s, l_s):
    seq = q_ref.shape[0]
    w = ATTN_W
    scale = HEAD_DIM ** -0.5

    def rows_of(start, dil):
        return pl.ds(start, w) if dil == 1 else pl.ds(start, w, stride=dil)

    lane = lax.broadcasted_iota(jnp.int32, (w, LANES), 1)
    head0 = lane < HEAD_DIM
    zero = jnp.zeros((), BF16)
    one = jnp.ones((), BF16)

    def group(pi, dil, g, firsts):
        rows_l, q_l, k_l, v_l = [], [], [], []
        for u, first in enumerate(firsts):
            b = g * len(firsts) + u
            start = (b % dil) + (b // dil) * (dil * w)
            rows = rows_of(start, dil)
            q = (q_ref[rows, :] * scale).astype(BF16)
            kk = k_ref[rows, :].astype(BF16)
            vv = v_ref[rows, :].astype(BF16)
            if not first:
                prev = rows_of(start - dil * w, dil)
                kk = jnp.concatenate([k_ref[prev, :].astype(BF16), kk], axis=0)
                vv = jnp.concatenate([v_ref[prev, :].astype(BF16), vv], axis=0)
            rows_l.append(rows)
            q_l.append(q)
            k_l.append(kk)
            v_l.append(vv)
        s = [[_dot_nt(jnp.where(head0 if h == 0 else ~head0, q, zero), kk)
              + (bias_ref[pi, h, :, w:] if first else bias_ref[pi, h])
              for h in range(2)] for q, kk, first in zip(q_l, k_l, firsts)]
        m = [[jnp.max(sh, axis=-1, keepdims=True) for sh in su] for su in s]
        pr = [[jnp.exp(sh - mh).astype(BF16) for sh, mh in zip(su, mu)] for su, mu in zip(s, m)]
        kmask = lambda vv: lax.broadcasted_iota(jnp.int32, vv.shape, 1) < HEAD_DIM
        res = [[_dot(pu[0], jnp.where(kmask(vv), vv, one)), _dot(pu[1], jnp.where(kmask(vv), one, vv))]
               for pu, vv in zip(pr, v_l)]
        for rows, ru, mu in zip(rows_l, res, m):
            acc_s[pi, rows, :] = jnp.where(head0, ru[0], ru[1])
            l_s[pi, rows, :] = jnp.where(head0, ru[1], ru[0])
            m_s[pi, rows, :] = jnp.where(head0, mu[0], mu[1])

    n_groups = (seq // w) // ATTN_UNROLL
    for pi, (window, dil) in enumerate(DILATED_PATTERNS):
        for g in range(n_groups):
            group(pi, dil, g, tuple((g * ATTN_UNROLL + u) < dil for u in range(ATTN_UNROLL)))

    mt = ATTN_MERGE_ROWS

    def merge_body(i, carry):
        rows = pl.ds(i * mt, mt)
        m0, m1, m2 = m_s[0, rows, :], m_s[1, rows, :], m_s[2, rows, :]
        mx = jnp.maximum(jnp.maximum(m0, m1), m2)
        w0, w1, w2 = jnp.exp(m0 - mx), jnp.exp(m1 - mx), jnp.exp(m2 - mx)
        num = w0 * acc_s[0, rows, :] + w1 * acc_s[1, rows, :] + w2 * acc_s[2, rows, :]
        swap = lambda x: pltpu.roll(x, HEAD_DIM, axis=1)
        den = w0 * swap(l_s[0, rows, :]) + w1 * swap(l_s[1, rows, :]) + w2 * swap(l_s[2, rows, :])
        o_ref[rows, :] = (num / den).astype(o_ref.dtype)
        return carry

    for i in range(seq // mt):
        merge_body(i, 0)


def _t5_bucket(dist):
    max_exact = NUM_BUCKETS // 2
    large = max_exact + (np.log(np.maximum(dist, max_exact) / max_exact)
                         / math.log(MAX_DISTANCE / max_exact) * (NUM_BUCKETS - max_exact)).astype(np.int32)
    return np.where(dist < max_exact, dist, np.minimum(large, NUM_BUCKETS - 1)).astype(np.int32)


def _attn_bias(rel_bias):
    w = ATTN_W
    i = np.arange(w)[:, None]
    j = np.arange(2 * w)[None, :]
    rel = i + w - j
    band = (rel >= 0) & (rel <= w)
    tabs = []
    for window, dil in DILATED_PATTERNS:
        bucket = _t5_bucket(np.clip(rel, 0, None) * dil)
        onehot = jnp.asarray(bucket[..., None] == np.arange(NUM_BUCKETS), F32)
        bias = jnp.einsum('ijb,bh->hij', onehot, rel_bias.astype(F32), precision=HI)
        tabs.append(jnp.where(band[None], bias, MASK_VALUE))
    return jnp.stack(tabs)


def dilated_attention(p, batch, seq, rel_bias):
    t = batch * seq
    bias = _attn_bias(rel_bias)
    col0 = RWKV_IN // LANES
    npair = ATTN_DIM // LANES
    spec = lambda off: pl.BlockSpec((seq, LANES), lambda b, hp: (b, col0 + off + hp))
    return pl.pallas_call(
        _attn_kernel,
        grid=(batch, npair),
        in_specs=[spec(0), spec(npair), spec(2 * npair),
                  pl.BlockSpec((3, 2, ATTN_W, 2 * ATTN_W), lambda b, hp: (0, hp, 0, 0))],
        out_specs=pl.BlockSpec((seq, LANES), lambda b, hp: (b, hp)),
        out_shape=jax.ShapeDtypeStruct((t, ATTN_DIM), BF16),
        scratch_shapes=[pltpu.VMEM((3, seq, LANES), F32) for _ in range(3)],
        compiler_params=_params(("parallel", "parallel")),
        name="dilated_attention",
    )(p, p, p, bias)


def _ret_kernel(q_ref, k_ref, v_ref, g_ref, cos_ref, sin_ref, dmat_ref, xi_ref, zeta_ref, gng_ref, gnb_ref,
                hsum_ref, o_ref, state_s):
    c = RET_CHUNK
    tb = q_ref.shape[0]

    @pl.when(pl.program_id(1) == 0)
    def _():
        state_s[...] = jnp.zeros_like(state_s)

    lane = lax.broadcasted_iota(jnp.int32, (c, RET_DIM), 1)
    first_half = (lane % HEAD_DIM) < (HEAD_DIM // 2)

    def rotate(x, cos, sin):
        swapped = jnp.where(first_half, pltpu.roll(x, RET_DIM - HEAD_DIM // 2, axis=1),
                            pltpu.roll(x, HEAD_DIM // 2, axis=1))
        return x * cos + swapped * sin

    heads = range(RET_HEADS)
    hsl = [slice(h * HEAD_DIM, (h + 1) * HEAD_DIM) for h in heads]
    chunk_decay = [(1.0 - 2.0 ** (-5.0 - h)) ** c for h in heads]

    def group_body(g, carry):
        rows_l, qb, kb, qx, kz, vb = [], [], [], [], [], []
        for u in range(RET_UNROLL):
            rows = pl.ds((g * RET_UNROLL + u) * c, c)
            cos, sin = cos_ref[rows, :], sin_ref[rows, :]
            q = rotate(q_ref[rows, :], cos, sin)
            k = rotate(k_ref[rows, :], cos, sin) * (HEAD_DIM ** -0.5)
            rows_l.append(rows)
            qb.append(q.astype(BF16))
            kb.append(k.astype(BF16))
            qx.append((q * xi_ref[...]).astype(BF16))
            kz.append((k * zeta_ref[...]).astype(BF16))
            vb.append(v_ref[rows, :].astype(BF16))
        sc = [[(_dot_nt(qb[u][:, sl], kb[u][:, sl]) * dmat_ref[h]).astype(BF16) for h, sl in zip(heads, hsl)]
              for u in range(RET_UNROLL)]
        intra = [[_dot(sc[u][h], vb[u][:, hsl[h]]) for h in heads] for u in range(RET_UNROLL)]
        kv = [[_dot_tn(kz[u][:, sl], vb[u][:, sl]) for sl in hsl] for u in range(RET_UNROLL)]
        states = [state_s[h] for h in heads]
        ys = []
        for u in range(RET_UNROLL):
            ys.append(jnp.concatenate(
                [intra[u][h] + _dot(qx[u][:, hsl[h]], states[h].astype(BF16)) for h in heads], axis=1))
            states = [states[h] * chunk_decay[h] + kv[u][h] for h in heads]
        for h in heads:
            state_s[h] = states[h]
        hsum = hsum_ref[...]
        for rows, y in zip(rows_l, ys):
            mean = _mm_ones(y, hsum, terms=2) * (1.0 / HEAD_DIM)
            dy = y - mean
            var = _mm_ones(dy * dy, hsum, terms=2) * (1.0 / HEAD_DIM)
            yn = dy * lax.rsqrt(var + LN_EPS) * gng_ref[...] + gnb_ref[...]
            gate = g_ref[rows, :]
            o_ref[rows, :] = (gate * _sigmoid(gate) * yn).astype(o_ref.dtype)
        return carry

    for g in range(tb // (c * RET_UNROLL)):
        group_body(g, 0)


def _ret_tables(seq):
    c = RET_CHUNK
    half = HEAD_DIM // 2
    inv = ROPE_BASE ** (-jnp.arange(half, dtype=F32) / half)
    ang = jnp.arange(seq, dtype=F32)[:, None] * inv
    cos, sin = jnp.cos(ang), jnp.sin(ang)
    cos_t = jnp.tile(jnp.concatenate([cos, cos], axis=1), (1, RET_HEADS))
    sin_t = jnp.tile(jnp.concatenate([-sin, sin], axis=1), (1, RET_HEADS))
    log_g = jnp.log1p(-jnp.exp2(-5.0 - jnp.arange(RET_HEADS, dtype=F32)))
    n = jnp.arange(c, dtype=F32)
    diff = n[:, None] - n[None, :]
    dmat = jnp.where(diff >= 0, jnp.exp(log_g[:, None, None] * jnp.maximum(diff, 0.0)), 0.0)
    zeta = jnp.exp(log_g[:, None] * (c - 1 - n))
    xi = jnp.exp(log_g[:, None] * (n + 1))
    widen = lambda tab: jnp.repeat(tab.T, HEAD_DIM, axis=1)
    return cos_t, sin_t, dmat, widen(xi), widen(zeta)


def retention(p, batch, seq, gn_g, gn_b):
    t = batch * seq
    tb = min(RET_BLOCK, seq)
    nblk = seq // tb
    cos_t, sin_t, dmat, xi, zeta = _ret_tables(seq)
    hsum = _head_sum_matrix(RET_DIM)
    col0 = (RWKV_IN + ATTN_IN) // RET_DIM
    spec = lambda off: pl.BlockSpec((tb, RET_DIM), lambda b, j: (b * nblk + j, col0 + off))
    tab = pl.BlockSpec((tb, RET_DIM), lambda b, j: (j, 0))
    consts = [dmat, xi, zeta, gn_g.reshape(1, -1), gn_b.reshape(1, -1), hsum]
    return pl.pallas_call(
        _ret_kernel,
        grid=(batch, nblk),
        in_specs=[spec(0), spec(1), spec(2), spec(3), tab, tab] + [_const_spec(a.shape) for a in consts],
        out_specs=pl.BlockSpec((tb, RET_DIM), lambda b, j: (b * nblk + j, 0)),
        out_shape=jax.ShapeDtypeStruct((t, RET_DIM), BF16),
        scratch_shapes=[pltpu.VMEM((RET_HEADS, HEAD_DIM, HEAD_DIM), F32)],
        compiler_params=_params(("parallel", "arbitrary")),
        name="retention",
    )(p, p, p, p, cos_t, sin_t, *consts)


def _outproj_kernel(ya_ref, yb_ref, yc_ref, x_ref, w_ref, g_ref, b_ref, *rest, alpha):
    acc = _dot(ya_ref[...], w_ref[0:RWKV_DIM, :])
    acc += _dot(yb_ref[...], w_ref[RWKV_DIM:RWKV_DIM + ATTN_DIM, :])
    acc += _dot(yc_ref[...], w_ref[RWKV_DIM + ATTN_DIM:, :])
    h = _layer_norm(alpha * x_ref[...] + acc, g_ref[...], b_ref[...])
    if len(rest) == 1:
        (o_ref,) = rest
    else:
        router_ref, o_ref, logit_ref = rest
        logit_ref[...] = _dot_nt(router_ref[...].astype(BF16), h.astype(BF16))
    o_ref[...] = h


def out_projection_ln(ya, yb, yc, x2d, w_bf16, layer, g, b, alpha, router=None):
    t, d = x2d.shape
    tm = min(ROW_TILE, t)
    rows = lambda width: pl.BlockSpec((tm, width), lambda i: (i, 0))
    in_specs = [rows(RWKV_DIM), rows(ATTN_DIM), rows(RET_DIM), rows(d), _layer_spec(w_bf16, layer),
                _const_spec((1, d)), _const_spec((1, d))]
    args = [ya, yb, yc, x2d, w_bf16, g.reshape(1, -1), b.reshape(1, -1)]
    out_specs, out_shape = rows(d), jax.ShapeDtypeStruct((t, d), F32)
    if router is not None:
        in_specs.append(_const_spec((N_EXPERTS, d)))
        args.append(router.T)
        out_specs = [out_specs, pl.BlockSpec((N_EXPERTS, tm), lambda i: (0, i))]
        out_shape = [out_shape, jax.ShapeDtypeStruct((N_EXPERTS, t), F32)]
    return pl.pallas_call(
        functools.partial(_outproj_kernel, alpha=alpha),
        grid=(t // tm,),
        in_specs=in_specs,
        out_specs=out_specs,
        out_shape=out_shape,
        compiler_params=_params(("parallel",)),
        name="out_projection_ln",
    )(*args)


def _swiglu(xb, wg, wu, wd, acc, f_chunk):
    ff = wg.shape[-1]
    starts = list(range(0, ff, f_chunk))
    gate_up = lambda f0: (_dot(xb, wg[:, f0:f0 + f_chunk]), _dot(xb, wu[:, f0:f0 + f_chunk]))
    nxt = gate_up(starts[0])
    for n, f0 in enumerate(starts):
        gate, up = nxt
        if n + 1 < len(starts):
            nxt = gate_up(starts[n + 1])
        hid = (gate * _sigmoid(gate) * up).astype(BF16)
        part = _dot(hid, wd[f0:f0 + f_chunk, :])
        acc = part if acc is None else acc + part
    return acc


def _ffn_kernel(x_ref, wg_ref, wu_ref, wd_ref, g_ref, b_ref, o_ref, *, alpha, f_chunk):
    x = x_ref[...]
    acc = _swiglu(x.astype(BF16), wg_ref, wu_ref, wd_ref, alpha * x, f_chunk)
    o_ref[...] = _layer_norm(acc, g_ref[...], b_ref[...])


def dense_ffn_ln(x2d, wg, wu, wd, layer, g, b, alpha):
    t, d = x2d.shape
    tm = min(ROW_TILE, t)
    return pl.pallas_call(
        functools.partial(_ffn_kernel, alpha=alpha, f_chunk=FFN_SUBCHUNK),
        grid=(t // tm,),
        in_specs=[pl.BlockSpec((tm, d), lambda i: (i, 0)), _layer_spec(wg, layer), _layer_spec(wu, layer),
                  _layer_spec(wd, layer), _const_spec((1, d)), _const_spec((1, d))],
        out_specs=pl.BlockSpec((tm, d), lambda i: (i, 0)),
        out_shape=jax.ShapeDtypeStruct((t, d), F32),
        compiler_params=_params(("parallel",)),
        name="dense_ffn_ln",
    )(x2d, wg, wu, wd, g.reshape(1, -1), b.reshape(1, -1))


def _expert_kernel(blk_e_ref, used_ref, x_ref, wg_ref, wu_ref, wd_ref, o_ref, acc_s):
    i, j = pl.program_id(0), pl.program_id(1)
    last = pl.num_programs(1) - 1

    @pl.when(i < used_ref[0])
    def _():
        part = _swiglu(x_ref[...].astype(BF16), wg_ref.at[0], wu_ref.at[0], wd_ref.at[0], None, FFN_SUBCHUNK)

        @pl.when(j == 0)
        def _():
            acc_s[...] = part

        @pl.when((j > 0) & (j < last))
        def _():
            acc_s[...] += part

        @pl.when(j == last)
        def _():
            o_ref[...] = (acc_s[...] + part).astype(o_ref.dtype)


def expert_ffn(xs, blk_e, n_used, wg, wu, wd, layer, f_chunk=MOE_F_BLOCK):
    rows, d = xs.shape
    ff = wg.shape[3]
    tm = MOE_TILE
    assert ff // f_chunk >= 2
    grid_spec = pltpu.PrefetchScalarGridSpec(
        num_scalar_prefetch=2,
        grid=(rows // tm, ff // f_chunk),
        in_specs=[pl.BlockSpec((tm, d), lambda i, j, be, nu: (i, 0)),
                  pl.BlockSpec((None, 1, d, f_chunk), lambda i, j, be, nu: (layer, be[i], 0, j)),
                  pl.BlockSpec((None, 1, d, f_chunk), lambda i, j, be, nu: (layer, be[i], 0, j)),
                  pl.BlockSpec((None, 1, f_chunk, d), lambda i, j, be, nu: (layer, be[i], j, 0))],
        out_specs=pl.BlockSpec((tm, d), lambda i, j, be, nu: (i, 0)),
        scratch_shapes=[pltpu.VMEM((tm, d), F32)],
    )
    return pl.pallas_call(
        _expert_kernel,
        grid_spec=grid_spec,
        out_shape=jax.ShapeDtypeStruct((rows, d), BF16),
        compiler_params=_params(("parallel", "arbitrary")),
        name="expert_ffn",
    )(blk_e, n_used, xs, wg, wu, wd)


def _combine_ln_kernel(x_ref, y0_ref, y1_ref, gate_ref, g_ref, b_ref, o_ref, *, alpha):
    gates = gate_ref[...]
    f = y0_ref[...] * gates[:, 0:1] + y1_ref[...] * gates[:, 1:2]
    o_ref[...] = _layer_norm(alpha * x_ref[...] + f, g_ref[...], b_ref[...])


def combine_ln(x2d, y0, y1, gates, g, b, alpha):
    t, d = x2d.shape
    tm = min(ROW_TILE, t)
    rows = pl.BlockSpec((tm, d), lambda i: (i, 0))
    return pl.pallas_call(
        functools.partial(_combine_ln_kernel, alpha=alpha),
        grid=(t // tm,),
        in_specs=[rows, rows, rows, pl.BlockSpec((tm, TOP_K), lambda i: (i, 0)),
                  _const_spec((1, d)), _const_spec((1, d))],
        out_specs=rows,
        out_shape=jax.ShapeDtypeStruct((t, d), F32),
        compiler_params=_params(("parallel",)),
        name="combine_ln",
    )(x2d, y0, y1, gates, g.reshape(1, -1), b.reshape(1, -1))


def moe_ffn_ln(x2d, logits_t, wg, wu, wd, layer, g, b, alpha):
    t, d = x2d.shape
    tm = MOE_TILE
    logits = logits_t.T
    top_val, top_idx = lax.top_k(logits, TOP_K)
    gates = jax.nn.softmax(top_val, axis=-1)
    member = (top_idx[:, :, None] == jnp.arange(N_EXPERTS)[None, None, :]).any(axis=1)
    counts = member.sum(axis=0).astype(jnp.int32)
    rank = jnp.cumsum(member.astype(jnp.int32), axis=0) - member.astype(jnp.int32)
    padded = (counts + tm - 1) // tm * tm
    pad_ends = jnp.cumsum(padded)
    pad_starts = pad_ends - padded
    pos = jnp.take_along_axis(pad_starts[None, :] + rank, top_idx, axis=1)
    rows = t * TOP_K + N_EXPERTS * tm
    order = jnp.argsort(top_idx.reshape(-1), stable=True).astype(jnp.int32)
    tok_sorted = order // TOP_K
    starts = jnp.cumsum(counts) - counts
    r = jnp.arange(rows, dtype=jnp.int32)
    row_e = jnp.minimum(jnp.searchsorted(pad_ends, r, side='right'), N_EXPERTS - 1).astype(jnp.int32)
    within = r - pad_starts[row_e]
    src = jnp.where(within < counts[row_e], tok_sorted[jnp.minimum(starts[row_e] + within, t * TOP_K - 1)], 0)
    blk_e = row_e[::tm]
    n_used = (pad_ends[-1:] // tm).astype(jnp.int32)
    ys = expert_ffn(x2d[src], blk_e, n_used, wg, wu, wd, layer)
    return combine_ln(x2d, ys[pos[:, 0]], ys[pos[:, 1]], gates, g, b, alpha)


def kernel(x, w_in, w_out, rwkv_mu, rwkv_w0, rwkv_w_up, rwkv_a0, rwkv_a_up, rwkv_g_up, rwkv_k_k, rwkv_k_a,
           rwkv_r_k, rwkv_ln_g, rwkv_ln_b, ret_gn_g, ret_gn_b, rel_bias, ln_g, ln_b, ffn_w_gate, ffn_w_up,
           ffn_w_down, moe_router, moe_w_gate, moe_w_up, moe_w_down):
    batch, seq, d = x.shape
    depth = w_in.shape[0]
    alpha = (2 * depth) ** 0.25
    h = x.reshape(batch * seq, d)
    w_in, w_out = w_in.astype(BF16), w_out.astype(BF16)
    ffn_w_gate, ffn_w_up, ffn_w_down = ffn_w_gate.astype(BF16), ffn_w_up.astype(BF16), ffn_w_down.astype(BF16)
    for layer in range(depth):
        p = in_projection(h, w_in, layer)
        ya = rwkv_time_mix(p, batch, seq, rwkv_mu[layer], rwkv_w0[layer], rwkv_w_up[layer], rwkv_a0[layer],
                           rwkv_a_up[layer], rwkv_g_up[layer], rwkv_k_k[layer], rwkv_k_a[layer],
                           rwkv_r_k[layer], rwkv_ln_g[layer], rwkv_ln_b[layer])
        yb = dilated_attention(p, batch, seq, rel_bias)
        yc = retention(p, batch, seq, ret_gn_g[layer], ret_gn_b[layer])
        j = layer // 2
        if layer % 2 == 0:
            h = out_projection_ln(ya, yb, yc, h, w_out, layer, ln_g[layer, 0], ln_b[layer, 0], alpha)
            h = dense_ffn_ln(h, ffn_w_gate, ffn_w_up, ffn_w_down, j, ln_g[layer, 1], ln_b[layer, 1], alpha)
        else:
            h, logits_t = out_projection_ln(ya, yb, yc, h, w_out, layer, ln_g[layer, 0], ln_b[layer, 0], alpha,
                                            router=moe_router[j])
            h = moe_ffn_ln(h, logits_t, layer_to_bf16(moe_w_gate, j), layer_to_bf16(moe_w_up, j),
                           layer_to_bf16(moe_w_down, j), 0, ln_g[layer, 1], ln_b[layer, 1], alpha)
    return h.reshape(batch, seq, d)
```
